```python
import math
import jax, jax.numpy as jnp
from jax import lax
import numpy as np

D_MODEL = 1024
BATCH = 2
SEQ = 8192
DEPTH = 1

HEAD_DIM = 64
MOBA_HEADS = 8
MOBA_BLOCK = 256
MOBA_TOPK = 3
NSA_HEADS = 8
NSA_KV_GROUPS = 2
NSA_HPG = NSA_HEADS // NSA_KV_GROUPS
NSA_CMP_BLOCK = 32
NSA_CMP_STRIDE = 16
NSA_CMP_HIDDEN = 256
NSA_SLC_BLOCK = 64
NSA_SLC_TOPN = 16
NSA_WINDOW = 512
REL_BUCKETS = 32
REL_MAX_DIST = 128
MEM_LEN = 256
XATTN_HEADS = 4
XATTN_HEAD_DIM = 128
D_FF = 2816
CONV_WIDTH = 3
Q_BLOCK = 128
RMS_EPS = 1e-6
NEG_BIG = -1e30

MOBA_W = MOBA_HEADS * HEAD_DIM
NSA_W = NSA_HEADS * HEAD_DIM
NSA_KV_W = NSA_KV_GROUPS * HEAD_DIM
XATTN_W = XATTN_HEADS * XATTN_HEAD_DIM
IN_SPLITS = (MOBA_W, MOBA_W, MOBA_W, NSA_W, NSA_KV_W, NSA_KV_W, NSA_KV_W, NSA_KV_W, NSA_KV_W, NSA_KV_W, 3 * NSA_HEADS, D_MODEL, D_MODEL)
IN_WIDTH = sum(IN_SPLITS)

kernel_name = 'hybrid_moba_nsa_gated_block'


def rms_norm(x, g):
    xf = x.astype(jnp.float32)
    y = xf * lax.rsqrt(jnp.mean(xf * xf, axis=-1, keepdims=True) + RMS_EPS)
    return (y * g.astype(jnp.float32)).astype(x.dtype)


def t5_bucket(dist):
    n = jnp.maximum(dist, 0)
    exact = REL_BUCKETS // 2
    nf = jnp.maximum(n, 1).astype(jnp.float32)
    large = exact + (jnp.log(nf / exact) / math.log(REL_MAX_DIST / exact) * (REL_BUCKETS - exact)).astype(jnp.int32)
    large = jnp.minimum(large, REL_BUCKETS - 1)
    return jnp.where(n < exact, n, large)


def masked_softmax(s, mask):
    s = jnp.where(mask, s.astype(jnp.float32), NEG_BIG)
    p = jax.nn.softmax(s, axis=-1)
    return jnp.where(jnp.any(mask, axis=-1, keepdims=True), p, 0.0)


def to_heads(t, n, dh):
    b, s, _ = t.shape
    return t.reshape(b, s, n, dh).transpose(0, 2, 1, 3)


def from_heads(t):
    b, n, s, dh = t.shape
    return t.transpose(0, 2, 1, 3).reshape(b, s, n * dh)


def moba_attention(q, k, v, rel_bias):
    b, h, s, dh = q.shape
    nb = -(-s // MOBA_BLOCK)
    sp = nb * MOBA_BLOCK
    pad = ((0, 0), (0, 0), (0, sp - s), (0, 0))
    k = jnp.pad(k, pad)
    v = jnp.pad(v, pad)
    scale = dh ** -0.5
    topk = min(MOBA_TOPK, nb)
    k_blk = k.reshape(b, h, nb, MOBA_BLOCK, dh)
    v_blk = v.reshape(b, h, nb, MOBA_BLOCK, dh)
    k_mean = jnp.mean(k_blk.astype(jnp.float32), axis=3).astype(k.dtype)
    bi = jnp.arange(b)[:, None, None, None]
    hi = jnp.arange(h)[None, :, None, None]
    blk_off = jnp.arange(MOBA_BLOCK)
    kb = topk * MOBA_BLOCK

    def query_block(ci):
        start = ci * Q_BLOCK
        qc = lax.dynamic_slice_in_dim(q, start, Q_BLOCK, axis=2)
        t = start + jnp.arange(Q_BLOCK)
        cur = start // MOBA_BLOCK
        gate = jnp.einsum('bhqd,bhnd->bhqn', qc, k_mean).astype(jnp.float32)
        gate = jnp.where(jnp.arange(nb) < cur, gate, -jnp.inf)
        _, sel = lax.top_k(gate, topk)
        k_sel = k_blk[bi, hi, sel]
        v_sel = v_blk[bi, hi, sel]
        pos_sel = sel[..., None] * MOBA_BLOCK + blk_off
        s_sel = jnp.einsum('bhqd,bhqjkd->bhqjk', qc, k_sel).astype(jnp.float32) * scale
        s_sel = s_sel + rel_bias[t5_bucket(t[:, None, None] - pos_sel), hi[..., None]]
        m_sel = jnp.broadcast_to((sel < cur)[..., None], pos_sel.shape)
        k_own = lax.dynamic_slice_in_dim(k, cur * MOBA_BLOCK, MOBA_BLOCK, axis=2)
        v_own = lax.dynamic_slice_in_dim(v, cur * MOBA_BLOCK, MOBA_BLOCK, axis=2)
        pos_own = cur * MOBA_BLOCK + blk_off
        s_own = jnp.einsum('bhqd,bhkd->bhqk', qc, k_own).astype(jnp.float32) * scale
        s_own = s_own + rel_bias[t5_bucket(t[:, None] - pos_own[None, :])].transpose(2, 0, 1)[None]
        m_own = jnp.broadcast_to(pos_own[None, :] <= t[:, None], s_own.shape)
        scores = jnp.concatenate([s_sel.reshape(b, h, Q_BLOCK, kb), s_own], axis=-1)
        mask = jnp.concatenate([m_sel.reshape(b, h, Q_BLOCK, kb), m_own], axis=-1)
        p = masked_softmax(scores, mask).astype(v.dtype)
        o = jnp.einsum('bhqjk,bhqjkd->bhqd', p[..., :kb].reshape(b, h, Q_BLOCK, topk, MOBA_BLOCK), v_sel)
        return o + jnp.einsum('bhqk,bhkd->bhqd', p[..., kb:], v_own)

    out = lax.map(query_block, jnp.arange(s // Q_BLOCK))
    return out.transpose(1, 2, 0, 3, 4).reshape(b, h, s, dh)


def compress_blocks(kv, pos_emb, w1, w2):
    b, g, s, dh = kv.shape
    nc = (s - NSA_CMP_BLOCK) // NSA_CMP_STRIDE + 1
    idx = jnp.arange(nc)[:, None] * NSA_CMP_STRIDE + jnp.arange(NSA_CMP_BLOCK)[None, :]
    blocks = (kv[:, :, idx] + pos_emb).reshape(b, g, nc, NSA_CMP_BLOCK * dh)
    return jax.nn.gelu(blocks @ w1) @ w2


def nsa_compressed_selected(qg, k_c, v_c, k_s, v_s, rb, pos_k, w1_k, w2_k, pos_v, w1_v, w2_v):
    b, g, r, s, dh = qg.shape
    scale = dh ** -0.5
    kc = compress_blocks(k_c, pos_k, w1_k, w2_k)
    vc = compress_blocks(v_c, pos_v, w1_v, w2_v)
    nc = kc.shape[2]
    c_end = jnp.arange(nc) * NSA_CMP_STRIDE + NSA_CMP_BLOCK - 1
    nsb = s // NSA_SLC_BLOCK
    topn = min(NSA_SLC_TOPN, nsb)
    ratio = NSA_SLC_BLOCK // NSA_CMP_STRIDE
    cover_off = np.array([m - n for m in range(ratio) for n in range(NSA_CMP_BLOCK // NSA_CMP_STRIDE)], dtype=np.int32)
    cidx = jnp.arange(nsb)[:, None] * ratio + jnp.asarray(cover_off)[None, :]
    c_ok = (cidx >= 0) & (cidx < nc)
    cidx = jnp.clip(cidx, 0, nc - 1)
    ks_blk = k_s.reshape(b, g, nsb, NSA_SLC_BLOCK, dh)
    vs_blk = v_s.reshape(b, g, nsb, NSA_SLC_BLOCK, dh)
    bi = jnp.arange(b)[:, None, None, None]
    gi = jnp.arange(g)[None, :, None, None]
    gi6 = jnp.arange(g)[None, :, None, None, None, None]
    ri6 = jnp.arange(r)[None, None, :, None, None, None]
    slc_off = jnp.arange(NSA_SLC_BLOCK)
    jblk = jnp.arange(nsb)
    nk = topn * NSA_SLC_BLOCK

    def query_block(ci):
        start = ci * Q_BLOCK
        qc = lax.dynamic_slice_in_dim(qg, start, Q_BLOCK, axis=3)
        t = start + jnp.arange(Q_BLOCK)
        s_c = jnp.einsum('bgrqd,bgnd->bgrqn', qc, kc).astype(jnp.float32) * scale
        p_c = masked_softmax(s_c, c_end[None, :] <= t[:, None])
        o_c = jnp.einsum('bgrqn,bgnd->bgrqd', p_c.astype(vc.dtype), vc)
        p_grp = jnp.sum(p_c, axis=2)
        imp = jnp.sum(jnp.where(c_ok, p_grp[..., cidx], 0.0), axis=-1)
        cur = t // NSA_SLC_BLOCK
        forced = (jblk[None, :] == 0) | (jblk[None, :] == cur[:, None]) | (jblk[None, :] == cur[:, None] - 1)
        imp = jnp.where(forced, jnp.inf, jnp.where(jblk[None, :] <= cur[:, None], imp, -jnp.inf))
        _, sel = lax.top_k(imp, topn)
        k_sel = ks_blk[bi, gi, sel]
        v_sel = vs_blk[bi, gi, sel]
        pos = sel[..., None] * NSA_SLC_BLOCK + slc_off
        dist = t[:, None, None] - pos
        s_s = jnp.einsum('bgrqd,bgqjkd->bgrqjk', qc, k_sel).astype(jnp.float32) * scale
        s_s = s_s + rb[t5_bucket(dist)[:, :, None], gi6, ri6]
        m_s = ((sel <= cur[:, None])[..., None] & (dist >= 0))[:, :, None]
        m_s = jnp.broadcast_to(m_s, s_s.shape)
        p_s = masked_softmax(s_s.reshape(b, g, r, Q_BLOCK, nk), m_s.reshape(b, g, r, Q_BLOCK, nk)).astype(v_s.dtype)
        o_s = jnp.einsum('bgrqjk,bgqjkd->bgrqd', p_s.reshape(b, g, r, Q_BLOCK, topn, NSA_SLC_BLOCK), v_sel)
        return o_c, o_s

    o_c, o_s = lax.map(query_block, jnp.arange(s // Q_BLOCK))
    o_c = o_c.transpose(1, 2, 3, 0, 4, 5).reshape(b, g, r, s, dh)
    o_s = o_s.transpose(1, 2, 3, 0, 4, 5).reshape(b, g, r, s, dh)
    return o_c, o_s


def sliding_window_attention(qg, k, v, rb):
    b, g, r, s, dh = qg.shape
    nqb = s // Q_BLOCK
    span = NSA_WINDOW + Q_BLOCK
    pad = ((0, 0), (0, 0), (NSA_WINDOW, 0), (0, 0))
    idx = jnp.arange(nqb)[:, None] * Q_BLOCK + jnp.arange(span)[None, :]
    kb = jnp.pad(k, pad)[:, :, idx]
    vb = jnp.pad(v, pad)[:, :, idx]
    qb = qg.reshape(b, g, r, nqb, Q_BLOCK, dh)
    t = jnp.arange(nqb)[:, None] * Q_BLOCK + jnp.arange(Q_BLOCK)[None, :]
    pos = idx - NSA_WINDOW
    dist = t[:, :, None] - pos[:, None, :]
    mask = (pos[:, None, :] >= 0) & (dist >= 0) & (dist < NSA_WINDOW)
    bias = rb[t5_bucket(dist)].transpose(3, 4, 0, 1, 2)
    sc = jnp.einsum('bgrnqd,bgnkd->bgrnqk', qb, kb).astype(jnp.float32) * (dh ** -0.5) + bias
    p = masked_softmax(sc, mask).astype(v.dtype)
    return jnp.einsum('bgrnqk,bgnkd->bgrnqd', p, vb).reshape(b, g, r, s, dh)


def causal_depthwise_conv(u, w, bias):
    f = u.shape[-1]
    y = lax.conv_general_dilated(u, w[:, None, :].astype(u.dtype), window_strides=(1,),
                                 padding=[(CONV_WIDTH - 1, 0)], dimension_numbers=('NWC', 'WIO', 'NWC'),
                                 feature_group_count=f)
    return y + bias


def hybrid_layer(x, mem, rel_bias, norm_mix_g, w_in, cmp_pos_k, cmp_w1_k, cmp_w2_k, cmp_pos_v, cmp_w1_v, cmp_w2_v,
                 w_branch_a, w_branch_b, w_out, norm_xattn_g, norm_mem_g, w_xq, w_xkv, w_xo,
                 norm_ffn_g, w_gate, w_up, conv_w, conv_b, w_down):
    b, s, _ = x.shape
    g, r, dh = NSA_KV_GROUPS, NSA_HPG, HEAD_DIM
    h = rms_norm(x, norm_mix_g)
    cuts = [int(c) for c in np.cumsum(IN_SPLITS)[:-1]]
    (mq, mk, mv, nq, nkc, nvc, nks, nvs, nkw, nvw, ngate, gate_a, gate_b) = jnp.split(h @ w_in, cuts, axis=-1)
    rb_moba = rel_bias[:, :MOBA_HEADS]
    rb_nsa = rel_bias[:, MOBA_HEADS:].reshape(REL_BUCKETS, g, r)
    o_a = moba_attention(to_heads(mq, MOBA_HEADS, dh), to_heads(mk, MOBA_HEADS, dh), to_heads(mv, MOBA_HEADS, dh), rb_moba)
    qg = to_heads(nq, NSA_HEADS, dh).reshape(b, g, r, s, dh)
    o_cmp, o_slc = nsa_compressed_selected(qg, to_heads(nkc, g, dh), to_heads(nvc, g, dh), to_heads(nks, g, dh),
                                           to_heads(nvs, g, dh), rb_nsa, cmp_pos_k, cmp_w1_k, cmp_w2_k,
                                           cmp_pos_v, cmp_w1_v, cmp_w2_v)
    o_win = sliding_window_attention(qg, to_heads(nkw, g, dh), to_heads(nvw, g, dh), rb_nsa)
    gb = jax.nn.sigmoid(ngate).reshape(b, s, 3, g, r).transpose(2, 0, 3, 4, 1)[..., None]
    o_b = (gb[0] * o_cmp + gb[1] * o_slc + gb[2] * o_win).reshape(b, NSA_HEADS, s, dh)
    merged = jax.nn.sigmoid(gate_a) * (from_heads(o_a) @ w_branch_a) + jax.nn.sigmoid(gate_b) * (from_heads(o_b) @ w_branch_b)
    x = x + merged @ w_out
    hq = rms_norm(x, norm_xattn_g)
    hm = rms_norm(mem, norm_mem_g)
    xq = to_heads(hq @ w_xq, XATTN_HEADS, XATTN_HEAD_DIM)
    xk, xv = jnp.split(hm @ w_xkv, 2, axis=-1)
    xk = to_heads(xk, XATTN_HEADS, XATTN_HEAD_DIM)
    xv = to_heads(xv, XATTN_HEADS, XATTN_HEAD_DIM)
    sc = jnp.einsum('bhqd,bhkd->bhqk', xq, xk).astype(jnp.float32) * (XATTN_HEAD_DIM ** -0.5)
    p = jax.nn.softmax(sc, axis=-1).astype(xv.dtype)
    x = x + from_heads(jnp.einsum('bhqk,bhkd->bhqd', p, xv)) @ w_xo
    hf = rms_norm(x, norm_ffn_g)
    a = causal_depthwise_conv(hf @ w_gate, conv_w, conv_b)
    return x + (jax.nn.gelu(a) * (hf @ w_up)) @ w_down


def setup_inputs(seed: int = 0) -> dict:
    key = jax.random.key(seed)
    ks = jax.random.split(key, 26)
    L = DEPTH
    f32 = jnp.float32

    def normal(k, shape, scale):
        return jax.random.normal(k, shape, f32) * scale

    def gain(k, shape):
        return 1.0 + 0.01 * jax.random.normal(k, shape, f32)

    cmp_in = NSA_CMP_BLOCK * HEAD_DIM
    return {
        'x': normal(ks[0], (BATCH, SEQ, D_MODEL), 1.0),
        'mem': normal(ks[1], (BATCH, MEM_LEN, D_MODEL), 1.0),
        'rel_bias': normal(ks[2], (REL_BUCKETS, MOBA_HEADS + NSA_HEADS), 0.1),
        'norm_mix_g': gain(ks[3], (L, D_MODEL)),
        'w_in': normal(ks[4], (L, D_MODEL, IN_WIDTH), D_MODEL ** -0.5),
        'cmp_pos_k': normal(ks[5], (L, NSA_CMP_BLOCK, HEAD_DIM), 0.1),
        'cmp_w1_k': normal(ks[6], (L, cmp_in, NSA_CMP_HIDDEN), cmp_in ** -0.5),
        'cmp_w2_k': normal(ks[7], (L, NSA_CMP_HIDDEN, HEAD_DIM), NSA_CMP_HIDDEN ** -0.5),
        'cmp_pos_v': normal(ks[8], (L, NSA_CMP_BLOCK, HEAD_DIM), 0.1),
        'cmp_w1_v': normal(ks[9], (L, cmp_in, NSA_CMP_HIDDEN), cmp_in ** -0.5),
        'cmp_w2_v': normal(ks[10], (L, NSA_CMP_HIDDEN, HEAD_DIM), NSA_CMP_HIDDEN ** -0.5),
        'w_branch_a': normal(ks[11], (L, MOBA_W, D_MODEL), MOBA_W ** -0.5),
        'w_branch_b': normal(ks[12], (L, NSA_W, D_MODEL), NSA_W ** -0.5),
        'w_out': normal(ks[13], (L, D_MODEL, D_MODEL), D_MODEL ** -0.5),
        'norm_xattn_g': gain(ks[14], (L, D_MODEL)),
        'norm_mem_g': gain(ks[15], (L, D_MODEL)),
        'w_xq': normal(ks[16], (L, D_MODEL, XATTN_W), D_MODEL ** -0.5),
        'w_xkv': normal(ks[17], (L, D_MODEL, 2 * XATTN_W), D_MODEL ** -0.5),
        'w_xo': normal(ks[18], (L, XATTN_W, D_MODEL), XATTN_W ** -0.5),
        'norm_ffn_g': gain(ks[19], (L, D_MODEL)),
        'w_gate': normal(ks[20], (L, D_MODEL, D_FF), D_MODEL ** -0.5),
        'w_up': normal(ks[21], (L, D_MODEL, D_FF), D_MODEL ** -0.5),
        'conv_w': normal(ks[22], (L, CONV_WIDTH, D_FF), CONV_WIDTH ** -0.5),
        'conv_b': normal(ks[23], (L, D_FF), 0.01),
        'w_down': normal(ks[24], (L, D_FF, D_MODEL), D_FF ** -0.5),
        'norm_final_g': gain(ks[25], (D_MODEL,)),
    }


def reference(x, mem, rel_bias, norm_mix_g, w_in, cmp_pos_k, cmp_w1_k, cmp_w2_k, cmp_pos_v, cmp_w1_v, cmp_w2_v,
              w_branch_a, w_branch_b, w_out, norm_xattn_g, norm_mem_g, w_xq, w_xkv, w_xo,
              norm_ffn_g, w_gate, w_up, conv_w, conv_b, w_down, norm_final_g):
    for l in range(DEPTH):
        x = hybrid_layer(x, mem, rel_bias, norm_mix_g[l], w_in[l], cmp_pos_k[l], cmp_w1_k[l], cmp_w2_k[l],
                         cmp_pos_v[l], cmp_w1_v[l], cmp_w2_v[l], w_branch_a[l], w_branch_b[l], w_out[l],
                         norm_xattn_g[l], norm_mem_g[l], w_xq[l], w_xkv[l], w_xo[l],
                         norm_ffn_g[l], w_gate[l], w_up[l], conv_w[l], conv_b[l], w_down[l])
    return rms_norm(x, norm_final_g)
```

```python
import functools
import math

import numpy as np
import jax
import jax.numpy as jnp
from jax import lax
from jax.experimental import pallas as pl
from jax.experimental.pallas import tpu as pltpu

F32 = jnp.float32
BF16 = jnp.bfloat16

D_MODEL = 1024
HEAD_DIM = 64
MOBA_HEADS = 8
MOBA_BLOCK = 256
MOBA_TOPK = 3
NSA_HEADS = 8
NSA_GROUPS = 2
NSA_HPG = NSA_HEADS // NSA_GROUPS
CMP_BLOCK = 32
CMP_STRIDE = 16
CMP_HIDDEN = 256
SLC_BLOCK = 64
SLC_TOPN = 16
WINDOW = 512
REL_BUCKETS = 32
REL_MAX_DIST = 128
XATTN_HEADS = 4
XATTN_HEAD_DIM = 128
D_FF = 2816
CONV_WIDTH = 3
Q_BLOCK = 128
RMS_EPS = 1e-6
NEG_BIG = -1e30

LANES = 128
KV_TILE = 256
VMEM_LIMIT = 56 * 1024 * 1024

MOBA_W = MOBA_HEADS * HEAD_DIM
NSA_W = NSA_HEADS * HEAD_DIM
NSA_KV_W = NSA_GROUPS * HEAD_DIM
XATTN_W = XATTN_HEADS * XATTN_HEAD_DIM
GATE_PAD = LANES
PROJ_W = 3 * MOBA_W + NSA_W + 6 * NSA_KV_W + GATE_PAD + 2 * D_MODEL
COL_GA = 0
COL_GB = COL_GA + D_MODEL
COL_MQ = COL_GB + D_MODEL
COL_MK = COL_MQ + MOBA_W
COL_MV = COL_MK + MOBA_W
COL_NQ = COL_MV + MOBA_W
COL_NKV = COL_NQ + NSA_W
COL_NGATE = COL_NKV + 6 * NSA_KV_W
assert COL_NGATE + GATE_PAD == PROJ_W

N_BIAS_TILES = 5


def _bucket_thresholds():
    n = np.arange(0, 4 * REL_MAX_DIST)
    exact = REL_BUCKETS // 2
    nf = np.maximum(n, 1).astype(np.float64)
    large = exact + (np.log(nf / exact) / math.log(REL_MAX_DIST / exact) * (REL_BUCKETS - exact)).astype(np.int64)
    large = np.minimum(large, REL_BUCKETS - 1)
    b = np.where(n < exact, n, large)
    return [int(np.argmax(b >= k)) for k in range(REL_BUCKETS)]


_THR = _bucket_thresholds()
assert _THR[REL_BUCKETS - 1] <= LANES


def _dot(a, b):
    return jnp.dot(a, b, preferred_element_type=F32)


def _dot_nt(a, b):
    return lax.dot_general(a, b, (((1,), (1,)), ((), ())), preferred_element_type=F32)


def _rms(x, g):
    return x * lax.rsqrt(jnp.mean(x * x, axis=-1, keepdims=True) + RMS_EPS) * g


def _params(n_axes):
    return pltpu.CompilerParams(dimension_semantics=("arbitrary",) * n_axes,
                                vmem_limit_bytes=VMEM_LIMIT)


def _bias_kernel(rb_ref, out_ref):
    h = pl.program_id(0)
    i = lax.broadcasted_iota(jnp.int32, (LANES, LANES), 0)
    j = lax.broadcasted_iota(jnp.int32, (LANES, LANES), 1)

    def table(d):
        t = jnp.full((LANES, LANES), rb_ref[0, h], F32)
        for k in range(1, REL_BUCKETS):
            t = jnp.where(d >= _THR[k], rb_ref[k, h], t)
        return t

    neg = jnp.full((LANES, LANES), NEG_BIG, F32)
    far = jnp.full((LANES, LANES), rb_ref[REL_BUCKETS - 1, h], F32)
    d0 = jnp.where(i >= j, table(i - j), neg)
    d1 = table(i - j + LANES)
    up = jnp.where(j > i, far, neg)
    tiles = ((d0, neg), (d1, d0), (far, d1), (far, far), (up, far))
    for t, (lo, hi) in enumerate(tiles):
        out_ref[0, t, :, :LANES] = lo
        out_ref[0, t, :, LANES:] = hi


def _bias_tiles(rel_bias):
    nh = rel_bias.shape[1]
    return pl.pallas_call(
        _bias_kernel,
        grid=(nh,),
        in_specs=[pl.BlockSpec(memory_space=pltpu.SMEM)],
        out_specs=pl.BlockSpec((1, N_BIAS_TILES, LANES, KV_TILE), lambda h: (h, 0, 0, 0)),
        out_shape=jax.ShapeDtypeStruct((nh, N_BIAS_TILES, LANES, KV_TILE), F32),
        compiler_params=_params(1),
        name="bias_tiles",
    )(rel_bias)


def _inproj_kernel(x_ref, g_ref, w_ref, o_ref, *, chunk):
    h = _rms(x_ref[...], g_ref[...]).astype(BF16)
    n = w_ref.shape[1]
    for c0 in range(0, n, chunk):
        c1 = min(c0 + chunk, n)
        o_ref[:, c0:c1] = _dot(h, w_ref[:, c0:c1]).astype(o_ref.dtype)


def _norm_matmul(x2d, g, w, tm, out_dtype=BF16, name="norm_matmul"):
    m, d = x2d.shape
    n = w.shape[1]
    return pl.pallas_call(
        functools.partial(_inproj_kernel, chunk=512),
        grid=(m // tm,),
        in_specs=[pl.BlockSpec((tm, d), lambda i: (i, 0)),
                  pl.BlockSpec((1, d), lambda i: (0, 0)),
                  pl.BlockSpec((d, n), lambda i: (0, 0))],
        out_specs=pl.BlockSpec((tm, n), lambda i: (i, 0)),
        out_shape=jax.ShapeDtypeStruct((m, n), out_dtype),
        compiler_params=_params(1),
        name=name,
    )(x2d, g.reshape(1, d), w)


def _kmean_kernel(k_ref, o_ref):
    nb = o_ref.shape[0]
    for n in range(nb):
        blk = k_ref[n * MOBA_BLOCK:(n + 1) * MOBA_BLOCK, :].astype(F32)
        o_ref[n:n + 1, :] = jnp.mean(blk, axis=0, keepdims=True)


def _moba_kmean(proj):
    b, s, _ = proj.shape
    nb = s // MOBA_BLOCK
    return pl.pallas_call(
        _kmean_kernel,
        grid=(b,),
        in_specs=[pl.BlockSpec((None, s, MOBA_W), lambda i: (i, 0, COL_MK // MOBA_W))],
        out_specs=pl.BlockSpec((None, nb, MOBA_W), lambda i: (i, 0, 0)),
        out_shape=jax.ShapeDtypeStruct((b, nb, MOBA_W), F32),
        compiler_params=_params(1),
        name="moba_kmean",
    )(proj)


def _fill_masked_v(v_ref, v0_ref, v1_ref, rows=512):
    s = v_ref.shape[0]
    lane = lax.broadcasted_iota(jnp.int32, (rows, LANES), 1)
    one = jnp.ones((rows, LANES), BF16)

    def body(c, carry):
        r0 = pl.multiple_of(c * rows, rows)
        v = v_ref[pl.ds(r0, rows), :]
        v0_ref[pl.ds(r0, rows), :] = jnp.where(lane < HEAD_DIM, v, one)
        v1_ref[pl.ds(r0, rows), :] = jnp.where(lane >= HEAD_DIM, v, one)
        return carry

    lax.fori_loop(0, s // rows, body, 0)


def _attn_step(q, k_tile, v_tile, bias, m_ref, acc_ref, h):
    s = _dot_nt(q, k_tile) + bias
    m_prev = m_ref[h]
    m_next = jnp.maximum(m_prev, jnp.max(s, axis=1, keepdims=True))
    alpha = jnp.exp(m_prev - m_next)
    reps = s.shape[1] // LANES
    m_wide = m_next if reps == 1 else jnp.concatenate([m_next] * reps, axis=1)
    p = jnp.exp(s - m_wide)
    acc_ref[h] = alpha * acc_ref[h] + _dot(p.astype(BF16), v_tile)
    m_ref[h] = m_next


def _pair_output(acc_lo, acc_hi):
    lane = lax.broadcasted_iota(jnp.int32, acc_lo.shape, 1)
    lo = acc_lo / pltpu.roll(acc_lo, HEAD_DIM, axis=1)
    hi = acc_hi / pltpu.roll(acc_hi, HEAD_DIM, axis=1)
    return jnp.where(lane < HEAD_DIM, lo, hi)


def _moba_kernel(q_ref, k_ref, v_ref, km_ref, bt_ref, o_ref,
                 v0_ref, v1_ref, mb_ref, qm_ref, m_ref, acc_ref):
    ci = pl.program_id(2)
    nb = km_ref.shape[0]

    @pl.when(ci == 0)
    def _():
        _fill_masked_v(v_ref, v0_ref, v1_ref)

    cur = lax.shift_right_logical(ci, 1)
    qoff = ci & 1
    lane = lax.broadcasted_iota(jnp.int32, (Q_BLOCK, LANES), 1)
    q = q_ref[...]
    zero = jnp.zeros_like(q)
    km = km_ref[...].astype(BF16)
    n_iota = lax.broadcasted_iota(jnp.int32, (Q_BLOCK, nb), 1).astype(F32)
    curf = cur.astype(F32)
    v_refs = (v0_ref, v1_ref)

    for hh in range(2):
        qh = jnp.where((lane < HEAD_DIM) if hh == 0 else (lane >= HEAD_DIM), q, zero)
        qm_ref[hh] = qh * jnp.asarray(HEAD_DIM ** -0.5, BF16)
        gate = _dot_nt(qh, km)
        g = jnp.where(n_iota < curf, gate, -jnp.inf)
        chosen = jnp.zeros((Q_BLOCK, nb), F32)
        for _ in range(min(MOBA_TOPK, nb)):
            mx = jnp.max(g, axis=1, keepdims=True)
            idx = jnp.min(jnp.where(g == mx, n_iota, float(nb)), axis=1, keepdims=True)
            pick = n_iota == idx
            chosen = jnp.where(pick, 1.0, chosen)
            g = jnp.where(pick, -jnp.inf, g)
        mbias = jnp.where((chosen > 0.5) & (n_iota < curf), 0.0, NEG_BIG)
        for n in range(nb):
            mb_ref[hh, n] = jnp.broadcast_to(mbias[:, n:n + 1], (Q_BLOCK, LANES))
        m_ref[hh] = jnp.full((Q_BLOCK, LANES), NEG_BIG, F32)
        acc_ref[hh] = jnp.zeros((Q_BLOCK, LANES), F32)

    ks = pl.multiple_of(cur * MOBA_BLOCK, MOBA_BLOCK)
    k_tile = k_ref[pl.ds(ks, MOBA_BLOCK), :]
    for hh in range(2):
        _attn_step(qm_ref[hh], k_tile, v_refs[hh][pl.ds(ks, MOBA_BLOCK), :], bt_ref[hh, qoff],
                   m_ref, acc_ref, hh)

    def past(it, carry):
        n = cur - 1 - it
        ks = pl.multiple_of(n * MOBA_BLOCK, MOBA_BLOCK)
        k_tile = k_ref[pl.ds(ks, MOBA_BLOCK), :]
        tile_id = jnp.where(it == 0, 2 + qoff, 3)
        for hh in range(2):
            mb = mb_ref[hh, n]
            bias = bt_ref[hh, tile_id] + jnp.concatenate([mb, mb], axis=1)
            _attn_step(qm_ref[hh], k_tile, v_refs[hh][pl.ds(ks, MOBA_BLOCK), :], bias,
                       m_ref, acc_ref, hh)
        return carry

    lax.fori_loop(0, cur, past, 0)
    o_ref[...] = _pair_output(acc_ref[0], acc_ref[1]).astype(o_ref.dtype)


def _moba_attention(proj, kmean, bias_tiles):
    b, s, _ = proj.shape
    nb = s // MOBA_BLOCK
    npair = MOBA_HEADS // 2
    return pl.pallas_call(
        _moba_kernel,
        grid=(b, npair, s // Q_BLOCK),
        in_specs=[
            pl.BlockSpec((None, Q_BLOCK, LANES), lambda i, p, c: (i, c, COL_MQ // LANES + p)),
            pl.BlockSpec((None, s, LANES), lambda i, p, c: (i, 0, COL_MK // LANES + p)),
            pl.BlockSpec((None, s, LANES), lambda i, p, c: (i, 0, COL_MV // LANES + p)),
            pl.BlockSpec((None, nb, LANES), lambda i, p, c: (i, 0, p)),
            pl.BlockSpec((2, N_BIAS_TILES, LANES, KV_TILE), lambda i, p, c: (p, 0, 0, 0)),
        ],
        out_specs=pl.BlockSpec((None, Q_BLOCK, LANES), lambda i, p, c: (i, c, p)),
        out_shape=jax.ShapeDtypeStruct((b, s, MOBA_W), BF16),
        scratch_shapes=[
            pltpu.VMEM((s, LANES), BF16), pltpu.VMEM((s, LANES), BF16),
            pltpu.VMEM((2, nb, Q_BLOCK, LANES), F32),
            pltpu.VMEM((2, Q_BLOCK, LANES), BF16),
            pltpu.VMEM((2, Q_BLOCK, LANES), F32),
            pltpu.VMEM((2, Q_BLOCK, LANES), F32),
        ],
        compiler_params=_params(3),
        name="moba_attention",
    )(proj, proj, proj, kmean, bias_tiles)


def _compress_kernel(ck_ref, cv_ref, pk_ref, pv_ref, w1k_ref, w1v_ref, w2k_ref, w2v_ref,
                     kc_ref, vct_ref):
    nch = ck_ref.shape[1]
    half = CMP_STRIDE * HEAD_DIM

    def run(c_ref, pos_ref, w1_ref, w2_ref):
        acc = jnp.zeros((nch, LANES), F32)
        for g in range(NSA_GROUPS):
            c = c_ref[g].astype(F32)
            lo = (c + pos_ref[0:1, :]).astype(BF16)
            hi = (c + pos_ref[1:2, :]).astype(BF16)
            pre = _dot(lo, w1_ref[:half, :]) + pltpu.roll(_dot(hi, w1_ref[half:, :]), nch - 1, axis=0)
            acc = acc + _dot(jax.nn.gelu(pre).astype(BF16), w2_ref[g])
        return acc

    kc_ref[...] = run(ck_ref, pk_ref, w1k_ref, w2k_ref).astype(kc_ref.dtype)
    vct_ref[...] = jnp.transpose(run(cv_ref, pv_ref, w1v_ref, w2v_ref)).astype(vct_ref.dtype)


def _compress(ck, cv, pos_k, pos_v, w1k, w1v, w2k, w2v):
    b, g, nch, width = ck.shape
    full = lambda shape: pl.BlockSpec(shape, lambda i: (0,) * len(shape))
    return pl.pallas_call(
        _compress_kernel,
        grid=(b,),
        in_specs=[pl.BlockSpec((None, g, nch, width), lambda i: (i, 0, 0, 0)),
                  pl.BlockSpec((None, g, nch, width), lambda i: (i, 0, 0, 0)),
                  full(pos_k.shape), full(pos_v.shape), full(w1k.shape), full(w1v.shape),
                  full(w2k.shape), full(w2v.shape)],
        out_specs=[pl.BlockSpec((None, nch, LANES), lambda i: (i, 0, 0)),
                   pl.BlockSpec((None, LANES, nch), lambda i: (i, 0, 0))],
        out_shape=[jax.ShapeDtypeStruct((b, nch, LANES), BF16),
                   jax.ShapeDtypeStruct((b, LANES, nch), BF16)],
        compiler_params=_params(1),
        name="nsa_compress",
    )(ck, cv, pos_k, pos_v, w1k, w1v, w2k, w2v)


def _cmp_select_kernel(q_ref, kc_ref, vct_ref, mimp_ref, o_ref, sel_ref, *, nsb):
    ci = pl.program_id(1)
    nch = kc_ref.shape[0]
    nsbp = mimp_ref.shape[0]
    qs = ci * Q_BLOCK
    n_row = lax.broadcasted_iota(jnp.int32, (nch, Q_BLOCK), 0)
    t_col = qs + lax.broadcasted_iota(jnp.int32, (nch, Q_BLOCK), 1)
    visible = (n_row * CMP_STRIDE + (CMP_BLOCK - 1)) <= t_col
    any_visible = t_col[0:1, :] >= (CMP_BLOCK - 1)
    kc = kc_ref[...]
    vct = vct_ref[...]
    klane = lax.broadcasted_iota(jnp.int32, kc.shape, 1)
    vrow = lax.broadcasted_iota(jnp.int32, vct.shape, 0)
    mimp = mimp_ref[...]
    scale = jnp.asarray(HEAD_DIM ** -0.5, BF16)

    j_row = lax.broadcasted_iota(jnp.int32, (nsbp, Q_BLOCK), 0)
    t_sel = qs + lax.broadcasted_iota(jnp.int32, (nsbp, Q_BLOCK), 1)
    cur = lax.shift_right_logical(t_sel, int(math.log2(SLC_BLOCK)))
    j_f = j_row.astype(F32)

    out_t = [jnp.zeros((LANES, Q_BLOCK), F32) for _ in range(NSA_HPG)]
    for g in range(NSA_GROUPS):
        in_g = (klane < HEAD_DIM) if g == 0 else (klane >= HEAD_DIM)
        kc_g = jnp.where(in_g, kc, jnp.zeros_like(kc))
        vin_g = (vrow < HEAD_DIM) if g == 0 else (vrow >= HEAD_DIM)
        vct_g = jnp.where(vin_g, vct, jnp.zeros_like(vct))
        p_grp = jnp.zeros((nch, Q_BLOCK), F32)
        for r in range(NSA_HPG):
            q = q_ref[:, r * LANES:(r + 1) * LANES] * scale
            s = jnp.where(visible, _dot_nt(kc_g, q), NEG_BIG)
            e = jnp.exp(s - jnp.max(s, axis=0, keepdims=True))
            p = e / jnp.sum(e, axis=0, keepdims=True)
            p = jnp.where(any_visible, p, 0.0)
            p_grp = p_grp + p
            out_t[r] = out_t[r] + _dot(vct_g, p.astype(BF16))
        p_hi = p_grp.astype(BF16)
        p_lo = (p_grp - p_hi.astype(F32)).astype(BF16)
        imp = _dot(mimp, p_hi) + _dot(mimp, p_lo)
        forced = (j_row == 0) | (j_row == cur) | (j_row == cur - 1)
        allowed = (j_row <= cur) & (j_row < nsb)
        val = jnp.where(forced & allowed, jnp.inf, jnp.where(allowed, imp, -jnp.inf))
        chosen = jnp.zeros((nsbp, Q_BLOCK), F32)
        for _ in range(min(SLC_TOPN, nsb)):
            mx = jnp.max(val, axis=0, keepdims=True)
            idx = jnp.min(jnp.where(val == mx, j_f, float(nsbp)), axis=0, keepdims=True)
            pick = j_f == idx
            chosen = jnp.where(pick, 1.0, chosen)
            val = jnp.where(pick, -jnp.inf, val)
        chosen = jnp.where(allowed, chosen, 0.0)
        sel_ref[g] = jnp.transpose(chosen).astype(sel_ref.dtype)
    for r in range(NSA_HPG):
        o_ref[:, r * LANES:(r + 1) * LANES] = jnp.transpose(out_t[r]).astype(o_ref.dtype)


def _cmp_select(proj, kc, vct, mimp_t, nsb):
    b, s, _ = proj.shape
    nch = kc.shape[1]
    nsbp = mimp_t.shape[0]
    return pl.pallas_call(
        functools.partial(_cmp_select_kernel, nsb=nsb),
        grid=(b, s // Q_BLOCK),
        in_specs=[pl.BlockSpec((None, Q_BLOCK, NSA_W), lambda i, c: (i, c, COL_NQ // NSA_W)),
                  pl.BlockSpec((None, nch, LANES), lambda i, c: (i, 0, 0)),
                  pl.BlockSpec((None, LANES, nch), lambda i, c: (i, 0, 0)),
                  pl.BlockSpec((nsbp, nch), lambda i, c: (0, 0))],
        out_specs=[pl.BlockSpec((None, Q_BLOCK, NSA_W), lambda i, c: (i, c, 0)),
                   pl.BlockSpec((None, NSA_GROUPS, Q_BLOCK, nsbp), lambda i, c: (i, 0, c, 0))],
        out_shape=[jax.ShapeDtypeStruct((b, s, NSA_W), BF16),
                   jax.ShapeDtypeStruct((b, NSA_GROUPS, s, nsbp), BF16)],
        compiler_params=_params(2),
        name="nsa_cmp_select",
    )(proj, kc, vct, mimp_t)


def _slc_win_kernel(q_ref, ks_ref, vs_ref, kw_ref, vw_ref, sel_ref, e_ref, bt_ref,
                    oslc_ref, owin_ref,
                    vs0_ref, vs1_ref, vw0_ref, vw1_ref, qm_ref, m_ref, acc_ref):
    ci = pl.program_id(1)

    @pl.when(ci == 0)
    def _():
        _fill_masked_v(vs_ref, vs0_ref, vs1_ref)
        _fill_masked_v(vw_ref, vw0_ref, vw1_ref)

    qs = ci * Q_BLOCK
    cur_t = lax.shift_right_logical(ci, 1)
    qoff = ci & 1
    lane = lax.broadcasted_iota(jnp.int32, (Q_BLOCK, LANES), 1)
    scale = jnp.asarray(HEAD_DIM ** -0.5, BF16)
    nh = NSA_HEADS
    for r in range(NSA_HPG):
        q = q_ref[:, r * LANES:(r + 1) * LANES] * scale
        zero = jnp.zeros_like(q)
        qm_ref[r] = jnp.where(lane < HEAD_DIM, q, zero)
        qm_ref[NSA_HPG + r] = jnp.where(lane >= HEAD_DIM, q, zero)
    for h in range(2 * nh):
        m_ref[h] = jnp.full((Q_BLOCK, LANES), NEG_BIG, F32)
        acc_ref[h] = jnp.zeros((Q_BLOCK, LANES), F32)
    vs_refs = (vs0_ref, vs1_ref)
    vw_refs = (vw0_ref, vw1_ref)

    def sel_tile(it, carry):
        n = cur_t - it
        ks = pl.multiple_of(n * KV_TILE, KV_TILE)
        k_tile = ks_ref[pl.ds(ks, KV_TILE), :]
        tile_id = jnp.where(it == 0, qoff, jnp.where(it == 1, 2 + qoff, 3))
        expand = e_ref[n]
        for g in range(NSA_GROUPS):
            chosen = _dot(sel_ref[g], expand)
            mbias = (chosen - 1.0) * (-NEG_BIG)
            v_tile = vs_refs[g][pl.ds(ks, KV_TILE), :]
            for r in range(NSA_HPG):
                h = g * NSA_HPG + r
                _attn_step(qm_ref[h], k_tile, v_tile, bt_ref[h, tile_id] + mbias, m_ref, acc_ref, h)
        return carry

    lax.fori_loop(0, cur_t + 1, sel_tile, 0)

    win_tiles = ((0, 0), (1, 1), (2, 3), (3, 3), (4, 4))
    for off, tile_id in win_tiles:
        @pl.when(ci >= off)
        def _(off=off, tile_id=tile_id):
            ks = pl.multiple_of(qs - off * Q_BLOCK, Q_BLOCK)
            k_tile = kw_ref[pl.ds(ks, Q_BLOCK), :]
            for g in range(NSA_GROUPS):
                v_tile = vw_refs[g][pl.ds(ks, Q_BLOCK), :]
                for r in range(NSA_HPG):
                    h = g * NSA_HPG + r
                    _attn_step(qm_ref[h], k_tile, v_tile, bt_ref[h, tile_id, :, :LANES],
                               m_ref, acc_ref, nh + h)

    for r in range(NSA_HPG):
        cols = slice(r * LANES, (r + 1) * LANES)
        oslc_ref[:, cols] = _pair_output(acc_ref[r], acc_ref[NSA_HPG + r]).astype(oslc_ref.dtype)
        owin_ref[:, cols] = _pair_output(acc_ref[nh + r], acc_ref[nh + NSA_HPG + r]).astype(owin_ref.dtype)


def _slc_win(proj, sel, expand, bias_tiles):
    b, s, _ = proj.shape
    nsbp = sel.shape[-1]
    nkt = expand.shape[0]
    kv_col = COL_NKV // LANES
    kv = lambda k: pl.BlockSpec((None, s, LANES), lambda i, c, k=k: (i, 0, kv_col + k))
    return pl.pallas_call(
        _slc_win_kernel,
        grid=(b, s // Q_BLOCK),
        in_specs=[pl.BlockSpec((None, Q_BLOCK, NSA_W), lambda i, c: (i, c, COL_NQ // NSA_W)),
                  kv(2), kv(3), kv(4), kv(5),
                  pl.BlockSpec((None, NSA_GROUPS, Q_BLOCK, nsbp), lambda i, c: (i, 0, c, 0)),
                  pl.BlockSpec((nkt, nsbp, KV_TILE), lambda i, c: (0, 0, 0)),
                  pl.BlockSpec((NSA_HEADS, N_BIAS_TILES, LANES, KV_TILE), lambda i, c: (1, 0, 0, 0))],
        out_specs=[pl.BlockSpec((None, Q_BLOCK, NSA_W), lambda i, c: (i, c, 0)),
                   pl.BlockSpec((None, Q_BLOCK, NSA_W), lambda i, c: (i, c, 0))],
        out_shape=[jax.ShapeDtypeStruct((b, s, NSA_W), BF16),
                   jax.ShapeDtypeStruct((b, s, NSA_W), BF16)],
        scratch_shapes=[pltpu.VMEM((s, LANES), BF16)] * 4 + [
            pltpu.VMEM((NSA_HEADS, Q_BLOCK, LANES), BF16),
            pltpu.VMEM((2 * NSA_HEADS, Q_BLOCK, LANES), F32),
            pltpu.VMEM((2 * NSA_HEADS, Q_BLOCK, LANES), F32)],
        compiler_params=_params(2),
        name="nsa_slc_win",
    )(proj, proj, proj, proj, proj, sel, expand, bias_tiles)


def _merge_kernel(x_ref, oa_ref, oc_ref, os_ref, ow_ref, ng_ref, ga_ref, gb_ref,
                  eg_ref, wa_ref, wb_ref, wo_ref, o_ref):
    sig = jax.nn.sigmoid(ng_ref[...].astype(F32))
    s_hi = sig.astype(BF16)
    s_lo = (sig - s_hi.astype(F32)).astype(BF16)
    eg = eg_ref[...]
    gexp = _dot(s_hi, eg) + _dot(s_lo, eg)
    o_b = (gexp[:, :NSA_W] * oc_ref[...].astype(F32)
           + gexp[:, NSA_W:2 * NSA_W] * os_ref[...].astype(F32)
           + gexp[:, 2 * NSA_W:] * ow_ref[...].astype(F32))
    a = _dot(oa_ref[...], wa_ref[...])
    bb = _dot(o_b.astype(BF16), wb_ref[...])
    merged = (jax.nn.sigmoid(ga_ref[...].astype(F32)) * a
              + jax.nn.sigmoid(gb_ref[...].astype(F32)) * bb)
    o_ref[...] = x_ref[...] + _dot(merged.astype(BF16), wo_ref[...])


def _merge(x2d, proj2d, o_a, o_c, o_s, o_w, egate, wa, wb, wo, tm):
    m, d = x2d.shape
    row = lambda w, col=0: pl.BlockSpec((tm, w), lambda i, col=col: (i, col))
    full = lambda a: pl.BlockSpec(a.shape, lambda i: (0, 0))
    return pl.pallas_call(
        _merge_kernel,
        grid=(m // tm,),
        in_specs=[row(d), row(MOBA_W), row(NSA_W), row(NSA_W), row(NSA_W),
                  row(GATE_PAD, COL_NGATE // GATE_PAD),
                  row(D_MODEL, COL_GA // D_MODEL), row(D_MODEL, COL_GB // D_MODEL),
                  full(egate), full(wa), full(wb), full(wo)],
        out_specs=row(d),
        out_shape=jax.ShapeDtypeStruct((m, d), F32),
        compiler_params=_params(1),
        name="merge_out",
    )(x2d, o_a, o_c, o_s, o_w, proj2d, proj2d, proj2d, egate, wa, wb, wo)


def _xattn_kernel(x_ref, g_ref, wq_ref, kv_ref, wo_ref, o_ref):
    x = x_ref[...]
    q = _dot(_rms(x, g_ref[...]).astype(BF16), wq_ref[...]).astype(BF16)
    heads = []
    for h in range(XATTN_HEADS):
        cols = slice(h * XATTN_HEAD_DIM, (h + 1) * XATTN_HEAD_DIM)
        k = kv_ref[:, cols]
        v = kv_ref[:, XATTN_W + h * XATTN_HEAD_DIM:XATTN_W + (h + 1) * XATTN_HEAD_DIM]
        s = _dot_nt(q[:, cols], k) * (XATTN_HEAD_DIM ** -0.5)
        e = jnp.exp(s - jnp.max(s, axis=1, keepdims=True))
        p = e / jnp.sum(e, axis=1, keepdims=True)
        heads.append(_dot(p.astype(BF16), v).astype(BF16))
    o_ref[...] = x + _dot(jnp.concatenate(heads, axis=1), wo_ref[...])


def _xattn(x3d, g, wq, memkv, wo, tm):
    b, s, d = x3d.shape
    mem_len = memkv.shape[1]
    return pl.pallas_call(
        _xattn_kernel,
        grid=(b, s // tm),
        in_specs=[pl.BlockSpec((None, tm, d), lambda i, j: (i, j, 0)),
                  pl.BlockSpec((1, d), lambda i, j: (0, 0)),
                  pl.BlockSpec(wq.shape, lambda i, j: (0, 0)),
                  pl.BlockSpec((None, mem_len, 2 * XATTN_W), lambda i, j: (i, 0, 0)),
                  pl.BlockSpec(wo.shape, lambda i, j: (0, 0))],
        out_specs=pl.BlockSpec((None, tm, d), lambda i, j: (i, j, 0)),
        out_shape=jax.ShapeDtypeStruct((b, s, d), F32),
        compiler_params=_params(2),
        name="xattn",
    )(x3d, g.reshape(1, d), wq, memkv, wo)


def _ffn_kernel(x_ref, halo_ref, g_ref, wg_ref, wu_ref, cw_ref, cb_ref, wd_ref, gf_ref, o_ref,
                acc_ref, *, chunk, halo_rows):
    j = pl.program_id(1)
    x = x_ref[...]
    tm = x.shape[0]
    hf = _rms(x, g_ref[...]).astype(BF16)
    keep = jnp.where(j > 0, 1.0, 0.0)
    hh = _rms(halo_ref[...], g_ref[...]).astype(BF16)
    row = lax.broadcasted_iota(jnp.int32, (tm, chunk), 0)
    acc_ref[...] = jnp.zeros_like(acc_ref)
    for c0 in range(0, D_FF, chunk):
        cols = slice(c0, c0 + chunk)
        u = _dot(hf, wg_ref[:, cols])
        uh = _dot(hh, wg_ref[:, cols]) * keep
        prev1 = uh[halo_rows - 1:halo_rows, :]
        prev2 = uh[halo_rows - 2:halo_rows - 1, :]
        u1 = jnp.where(row == 0, prev1, pltpu.roll(u, 1, axis=0))
        u2 = jnp.where(row == 0, prev2, jnp.where(row == 1, prev1, pltpu.roll(u, 2, axis=0)))
        a = cw_ref[0:1, cols] * u2 + cw_ref[1:2, cols] * u1 + cw_ref[2:3, cols] * u + cb_ref[:, cols]
        y = jax.nn.gelu(a) * _dot(hf, wu_ref[:, cols])
        acc_ref[...] += _dot(y.astype(BF16), wd_ref[cols, :])
    o_ref[...] = _rms(x + acc_ref[...], gf_ref[...])


def _ffn(x3d, g, wg, wu, cw, cb, wd, gf, tm):
    b, s, d = x3d.shape
    halo_rows = 8
    per = tm // halo_rows
    return pl.pallas_call(
        functools.partial(_ffn_kernel, chunk=256, halo_rows=halo_rows),
        grid=(b, s // tm),
        in_specs=[pl.BlockSpec((None, tm, d), lambda i, j: (i, j, 0)),
                  pl.BlockSpec((None, halo_rows, d), lambda i, j: (i, jnp.maximum(j * per - 1, 0), 0)),
                  pl.BlockSpec((1, d), lambda i, j: (0, 0)),
                  pl.BlockSpec(wg.shape, lambda i, j: (0, 0)),
                  pl.BlockSpec(wu.shape, lambda i, j: (0, 0)),
                  pl.BlockSpec(cw.shape, lambda i, j: (0, 0)),
                  pl.BlockSpec((1, D_FF), lambda i, j: (0, 0)),
                  pl.BlockSpec(wd.shape, lambda i, j: (0, 0)),
                  pl.BlockSpec((1, d), lambda i, j: (0, 0))],
        out_specs=pl.BlockSpec((None, tm, d), lambda i, j: (i, j, 0)),
        out_shape=jax.ShapeDtypeStruct((b, s, d), F32),
        scratch_shapes=[pltpu.VMEM((tm, d), F32)],
        compiler_params=_params(2),
        name="conv_ffn",
    )(x3d, x3d, g.reshape(1, d), wg, wu, cw, cb.reshape(1, D_FF), wd, gf.reshape(1, d))


def _nq_perm():
    idx = []
    for r in range(NSA_HPG):
        for g in range(NSA_GROUPS):
            h = g * NSA_HPG + r
            idx.extend(range(h * HEAD_DIM, (h + 1) * HEAD_DIM))
    return np.asarray(idx, np.int32)


def _gate_expand():
    e = np.zeros((GATE_PAD, 3 * NSA_W), np.float32)
    for br in range(3):
        for g in range(NSA_GROUPS):
            for r in range(NSA_HPG):
                c0 = br * NSA_W + r * LANES + g * HEAD_DIM
                e[br * NSA_HEADS + g * NSA_HPG + r, c0:c0 + HEAD_DIM] = 1.0
    return e


def _importance_matrix(nch, nsb, nsbp):
    ratio = SLC_BLOCK // CMP_STRIDE
    nc = nch - 1
    m = np.zeros((nsbp, nch), np.float32)
    for j in range(nsb):
        for a in range(ratio):
            for bb in range(CMP_BLOCK // CMP_STRIDE):
                c = j * ratio + a - bb
                if 0 <= c < nc:
                    m[j, c] += 1.0
    return m


def _block_expand(nkt, nsbp):
    e = np.zeros((nkt, nsbp, KV_TILE), np.float32)
    per = KV_TILE // SLC_BLOCK
    for kt in range(nkt):
        for c in range(KV_TILE):
            e[kt, kt * per + c // SLC_BLOCK, c] = 1.0
    return e


def _layer(x, mem, bias_tiles, norm_mix_g, w_in, cmp_pos_k, cmp_w1_k, cmp_w2_k, cmp_pos_v, cmp_w1_v,
           cmp_w2_v, w_branch_a, w_branch_b, w_out, norm_xattn_g, norm_mem_g, w_xq, w_xkv, w_xo,
           norm_ffn_g, w_gate, w_up, conv_w, conv_b, w_down, norm_out_g):
    b, s, d = x.shape
    assert s % (2 * KV_TILE) == 0 and d == D_MODEL
    rows = b * s
    nch = s // CMP_STRIDE
    nsb = s // SLC_BLOCK
    nsbp = max(LANES, nsb)
    assert nsbp == LANES, "selection masks are one 128-lane tile wide"

    cuts = np.cumsum([MOBA_W, MOBA_W, MOBA_W, NSA_W, 6 * NSA_KV_W, 3 * NSA_HEADS, D_MODEL])
    mqkv, nq, nkv, ngate, gates = (w_in[:, :cuts[2]], w_in[:, cuts[2]:cuts[3]], w_in[:, cuts[3]:cuts[4]],
                                   w_in[:, cuts[4]:cuts[5]], w_in[:, cuts[5]:])
    w_proj = jnp.concatenate(
        [gates, mqkv, nq[:, _nq_perm()], nkv, jnp.pad(ngate, ((0, 0), (0, GATE_PAD - 3 * NSA_HEADS)))],
        axis=1).astype(BF16)
    wb_perm = w_branch_b[_nq_perm(), :].astype(BF16)

    proj2d = _norm_matmul(x.reshape(rows, d), norm_mix_g, w_proj, tm=512, name="in_proj")
    proj = proj2d.reshape(b, s, PROJ_W)

    kmean = _moba_kmean(proj)
    o_a = _moba_attention(proj, kmean, bias_tiles)

    def chunks(col):
        t = proj[:, :, col:col + NSA_KV_W].reshape(b, s, NSA_GROUPS, HEAD_DIM)
        return t.transpose(0, 2, 1, 3).reshape(b, NSA_GROUPS, nch, CMP_STRIDE * HEAD_DIM)

    def w2_pad(w2):
        z = jnp.zeros_like(w2)
        return jnp.stack([jnp.concatenate([w2, z], axis=1), jnp.concatenate([z, w2], axis=1)]).astype(BF16)

    kc, vct = _compress(chunks(COL_NKV), chunks(COL_NKV + NSA_KV_W),
                        cmp_pos_k.reshape(2, -1), cmp_pos_v.reshape(2, -1),
                        cmp_w1_k.astype(BF16), cmp_w1_v.astype(BF16), w2_pad(cmp_w2_k), w2_pad(cmp_w2_v))
    mimp_t = jnp.asarray(_importance_matrix(nch, nsb, nsbp), BF16)
    o_c, sel = _cmp_select(proj, kc, vct, mimp_t, nsb)
    expand = jnp.asarray(_block_expand(s // KV_TILE, nsbp), BF16)
    o_s, o_w = _slc_win(proj, sel, expand, bias_tiles)

    x1 = _merge(x.reshape(rows, d), proj2d, o_a.reshape(rows, MOBA_W), o_c.reshape(rows, NSA_W),
                o_s.reshape(rows, NSA_W), o_w.reshape(rows, NSA_W), jnp.asarray(_gate_expand(), BF16),
                w_branch_a.astype(BF16), wb_perm, w_out.astype(BF16), tm=512)

    mem_len = mem.shape[1]
    memkv = _norm_matmul(mem.reshape(b * mem_len, d), norm_mem_g, w_xkv.astype(BF16), tm=mem_len,
                         name="mem_kv").reshape(b, mem_len, 2 * XATTN_W)
    x2 = _xattn(x1.reshape(b, s, d), norm_xattn_g, w_xq.astype(BF16), memkv, w_xo.astype(BF16), tm=512)

    return _ffn(x2, norm_ffn_g, w_gate.astype(BF16), w_up.astype(BF16), conv_w, conv_b,
                w_down.astype(BF16), norm_out_g, tm=512)


def kernel(x, mem, rel_bias, norm_mix_g, w_in, cmp_pos_k, cmp_w1_k, cmp_w2_k, cmp_pos_v, cmp_w1_v, cmp_w2_v, w_branch_a, w_branch_b, w_out, norm_xattn_g, norm_mem_g, w_xq, w_xkv, w_xo, norm_ffn_g, w_gate, w_up, conv_w, conv_b, w_down, norm_final_g):
    depth = w_in.shape[0]
    assert depth == 1, "the final norm is fused into the last layer's FFN kernel"
    bias_tiles = _bias_tiles(rel_bias)
    l = 0
    return _layer(x, mem, bias_tiles, norm_mix_g[l], w_in[l], cmp_pos_k[l], cmp_w1_k[l], cmp_w2_k[l],
                  cmp_pos_v[l], cmp_w1_v[l], cmp_w2_v[l], w_branch_a[l], w_branch_b[l], w_out[l],
                  norm_xattn_g[l], norm_mem_g[l], w_xq[l], w_xkv[l], w_xo[l],
                  norm_ffn_g[l], w_gate[l], w_up[l], conv_w[l], conv_b[l], w_down[l], norm_final_g)
```

```python
import functools
import math

import numpy as np
import jax
import jax.numpy as jnp
from jax import lax
from jax.experimental import pallas as pl
from jax.experimental.pallas import tpu as pltpu

F32 = jnp.float32
BF16 = jnp.bfloat16

D_MODEL = 1024
HEAD_DIM = 64
MOBA_HEADS = 8
MOBA_BLOCK = 256
MOBA_TOPK = 3
NSA_HEADS = 8
NSA_GROUPS = 2
NSA_HPG = NSA_HEADS // NSA_GROUPS
CMP_BLOCK = 32
CMP_STRIDE = 16
CMP_HIDDEN = 256
SLC_BLOCK = 64
SLC_TOPN = 16
WINDOW = 512
REL_BUCKETS = 32
REL_MAX_DIST = 128
XATTN_HEADS = 4
XATTN_HEAD_DIM = 128
D_FF = 2816
CONV_WIDTH = 3
Q_BLOCK = 128
RMS_EPS = 1e-6
NEG_BIG = -1e30

LANES = 128
TQ = 256
KT = 256
ONES_ROWS = 16
VROWS = HEAD_DIM + ONES_ROWS
VMEM_LIMIT = 56 * 1024 * 1024

MOBA_W = MOBA_HEADS * HEAD_DIM
NSA_W = NSA_HEADS * HEAD_DIM
NSA_KV_W = NSA_GROUPS * HEAD_DIM
XATTN_W = XATTN_HEADS * XATTN_HEAD_DIM
GATE_PAD = LANES

COL_GA = 0
COL_GB = COL_GA + D_MODEL
COL_MK = COL_GB + D_MODEL
COL_NKC = COL_MK + MOBA_W
COL_NVC = COL_NKC + NSA_KV_W
COL_NKS = COL_NVC + NSA_KV_W
COL_NKW = COL_NKS + NSA_KV_W
COL_NGATE = COL_NKW + NSA_KV_W
TOK_W = COL_NGATE + GATE_PAD
ROW_MQ = 0
ROW_MV = ROW_MQ + MOBA_W
ROW_NQ = ROW_MV + MOBA_W
ROW_NVS = ROW_NQ + NSA_W
ROW_NVW = ROW_NVS + NSA_KV_W
FM_W = ROW_NVW + NSA_KV_W

BT_OWN, BT_PREV, BT_UPPER = 0, 1, 2
N_BIAS_TILES = 3


def _bucket_thresholds():
    n = np.arange(0, 4 * REL_MAX_DIST)
    exact = REL_BUCKETS // 2
    nf = np.maximum(n, 1).astype(np.float64)
    large = exact + (np.log(nf / exact) / math.log(REL_MAX_DIST / exact) * (REL_BUCKETS - exact)).astype(np.int64)
    large = np.minimum(large, REL_BUCKETS - 1)
    b = np.where(n < exact, n, large)
    return [int(np.argmax(b >= k)) for k in range(REL_BUCKETS)]


_THR = _bucket_thresholds()
assert _THR[REL_BUCKETS - 1] <= LANES


def _dot(a, b):
    return jnp.dot(a, b, preferred_element_type=F32)


def _dot_nt(a, b):
    return lax.dot_general(a, b, (((1,), (1,)), ((), ())), preferred_element_type=F32)


def _rms(x, g):
    return x * lax.rsqrt(jnp.mean(x * x, axis=-1, keepdims=True) + RMS_EPS) * g


def _params(n_axes):
    return pltpu.CompilerParams(dimension_semantics=("arbitrary",) * n_axes,
                                vmem_limit_bytes=VMEM_LIMIT)


_SMEM_SPEC = pl.BlockSpec(memory_space=pltpu.SMEM)


def _bias_kernel(rb_ref, out_ref):
    h = pl.program_id(0)
    j = lax.broadcasted_iota(jnp.int32, (KT, TQ), 0)
    i = lax.broadcasted_iota(jnp.int32, (KT, TQ), 1)

    def table(d):
        t = jnp.full((KT, TQ), rb_ref[0, h], F32)
        for k in range(1, REL_BUCKETS):
            t = jnp.where(d >= _THR[k], rb_ref[k, h], t)
        return t

    neg = jnp.full((KT, TQ), NEG_BIG, F32)
    far = jnp.full((KT, TQ), rb_ref[REL_BUCKETS - 1, h], F32)
    out_ref[0, BT_OWN] = jnp.where(i >= j, table(i - j), neg)
    out_ref[0, BT_PREV] = table(i - j + KT)
    out_ref[0, BT_UPPER] = jnp.where(j > i, far, neg)


def _bias_tiles(rel_bias):
    nh = rel_bias.shape[1]
    return pl.pallas_call(
        _bias_kernel,
        grid=(nh,),
        in_specs=[_SMEM_SPEC],
        out_specs=pl.BlockSpec((1, N_BIAS_TILES, KT, TQ), lambda h: (h, 0, 0, 0)),
        out_shape=jax.ShapeDtypeStruct((nh, N_BIAS_TILES, KT, TQ), F32),
        compiler_params=_params(1),
        name="bias_tiles",
    )(rel_bias)


def _inproj_kernel(x_ref, g_ref, wt_ref, wf_ref, tok_ref, fm_ref, *, chunk):
    h = _rms(x_ref[...], g_ref[...]).astype(BF16)
    n = wt_ref.shape[1]
    for c0 in range(0, n, chunk):
        c1 = min(c0 + chunk, n)
        tok_ref[:, c0:c1] = _dot(h, wt_ref[:, c0:c1]).astype(tok_ref.dtype)
    n = wf_ref.shape[0]
    for c0 in range(0, n, chunk):
        c1 = min(c0 + chunk, n)
        fm_ref[c0:c1, :] = _dot_nt(wf_ref[c0:c1, :], h).astype(fm_ref.dtype)


def _in_proj(x, g, w_tok, w_fm_t, tm):
    b, s, d = x.shape
    return pl.pallas_call(
        functools.partial(_inproj_kernel, chunk=512),
        grid=(b, s // tm),
        in_specs=[pl.BlockSpec((None, tm, d), lambda i, j: (i, j, 0)),
                  pl.BlockSpec((1, d), lambda i, j: (0, 0)),
                  pl.BlockSpec(w_tok.shape, lambda i, j: (0, 0)),
                  pl.BlockSpec(w_fm_t.shape, lambda i, j: (0, 0))],
        out_specs=[pl.BlockSpec((None, tm, TOK_W), lambda i, j: (i, j, 0)),
                   pl.BlockSpec((None, FM_W, tm), lambda i, j: (i, 0, j))],
        out_shape=[jax.ShapeDtypeStruct((b, s, TOK_W), BF16),
                   jax.ShapeDtypeStruct((b, FM_W, s), BF16)],
        compiler_params=_params(2),
        name="in_proj",
    )(x, g.reshape(1, d), w_tok, w_fm_t)


def _norm_matmul_kernel(x_ref, g_ref, w_ref, o_ref):
    o_ref[...] = _dot(_rms(x_ref[...], g_ref[...]).astype(BF16), w_ref[...]).astype(o_ref.dtype)


def _norm_matmul(x2d, g, w, tm, name):
    m, d = x2d.shape
    n = w.shape[1]
    return pl.pallas_call(
        _norm_matmul_kernel,
        grid=(m // tm,),
        in_specs=[pl.BlockSpec((tm, d), lambda i: (i, 0)),
                  pl.BlockSpec((1, d), lambda i: (0, 0)),
                  pl.BlockSpec((d, n), lambda i: (0, 0))],
        out_specs=pl.BlockSpec((tm, n), lambda i: (i, 0)),
        out_shape=jax.ShapeDtypeStruct((m, n), BF16),
        compiler_params=_params(1),
        name=name,
    )(x2d, g.reshape(1, d), w)


def _kmean_kernel(k_ref, o_ref):
    nb = o_ref.shape[0]
    for n in range(nb):
        blk = k_ref[n * MOBA_BLOCK:(n + 1) * MOBA_BLOCK, :].astype(F32)
        o_ref[n:n + 1, :] = jnp.mean(blk, axis=0, keepdims=True)


def _moba_kmean(tok):
    b, s, _ = tok.shape
    nb = s // MOBA_BLOCK
    return pl.pallas_call(
        _kmean_kernel,
        grid=(b,),
        in_specs=[pl.BlockSpec((None, s, MOBA_W), lambda i: (i, 0, COL_MK // MOBA_W))],
        out_specs=pl.BlockSpec((None, nb, MOBA_W), lambda i: (i, 0, 0)),
        out_shape=jax.ShapeDtypeStruct((b, nb, MOBA_W), F32),
        compiler_params=_params(1),
        name="moba_kmean",
    )(tok)


def _fill_v_with_ones(vt_ref, lo_ref, hi_ref, cols=1024):
    s = vt_ref.shape[1]
    ones = jnp.ones((ONES_ROWS, cols), BF16)

    def body(c, carry):
        c0 = pl.multiple_of(c * cols, cols)
        lo_ref[0:HEAD_DIM, pl.ds(c0, cols)] = vt_ref[0:HEAD_DIM, pl.ds(c0, cols)]
        lo_ref[HEAD_DIM:VROWS, pl.ds(c0, cols)] = ones
        hi_ref[0:HEAD_DIM, pl.ds(c0, cols)] = vt_ref[HEAD_DIM:2 * HEAD_DIM, pl.ds(c0, cols)]
        hi_ref[HEAD_DIM:VROWS, pl.ds(c0, cols)] = ones
        return carry

    lax.fori_loop(0, s // cols, body, 0)


def _split_heads_t(q_t):
    row = lax.broadcasted_iota(jnp.int32, q_t.shape, 0)
    zero = jnp.zeros_like(q_t)
    return jnp.where(row < HEAD_DIM, q_t, zero), jnp.where(row >= HEAD_DIM, q_t, zero)


def _attn_step(q_t, k_tile, vt_tile, bias, m_ref, acc_ref, h):
    s = _dot(k_tile, q_t) + bias
    m_prev = m_ref[h]
    m_next = jnp.maximum(m_prev, jnp.max(s, axis=0, keepdims=True))
    alpha = jnp.exp(m_prev - m_next)
    p = jnp.exp(s - m_next)
    acc_ref[h] = alpha * acc_ref[h] + _dot(vt_tile, p.astype(BF16))
    m_ref[h] = m_next


def _reset_state(m_ref, acc_ref):
    m_ref[...] = jnp.full(m_ref.shape, NEG_BIG, F32)
    acc_ref[...] = jnp.zeros(acc_ref.shape, F32)


def _pair_output(acc_lo, acc_hi):
    o_lo = acc_lo[0:HEAD_DIM, :] / acc_lo[HEAD_DIM:HEAD_DIM + 1, :]
    o_hi = acc_hi[0:HEAD_DIM, :] / acc_hi[HEAD_DIM:HEAD_DIM + 1, :]
    return jnp.transpose(jnp.concatenate([o_lo, o_hi], axis=0))


def _top_k_rows(val, n_f, k, sentinel):
    chosen = jnp.zeros(val.shape, F32)
    for _ in range(k):
        mx = jnp.max(val, axis=0, keepdims=True)
        idx = jnp.min(jnp.where(val == mx, n_f, sentinel), axis=0, keepdims=True)
        pick = n_f == idx
        chosen = jnp.where(pick, 1.0, chosen)
        val = jnp.where(pick, -jnp.inf, val)
    return chosen


def _moba_kernel(rb_ref, qt_ref, k_ref, vt_ref, km_ref, bt_ref, o_ref,
                 v0_ref, v1_ref, mb_ref, qs_ref, m_ref, acc_ref):
    pair = pl.program_id(1)
    cur = pl.program_id(2)
    nb = km_ref.shape[0]

    @pl.when(cur == 0)
    def _():
        _fill_v_with_ones(vt_ref, v0_ref, v1_ref)

    km = km_ref[...].astype(BF16)
    n_f = lax.broadcasted_iota(jnp.int32, (nb, TQ), 0).astype(F32)
    past = n_f < cur.astype(F32)
    v_refs = (v0_ref, v1_ref)
    far = []
    for hh, qh in enumerate(_split_heads_t(qt_ref[...])):
        gate = jnp.where(past, _dot(km, qh), -jnp.inf)
        chosen = _top_k_rows(gate, n_f, min(MOBA_TOPK, nb), float(nb))
        mb_ref[hh] = jnp.where((chosen > 0.5) & past, 0.0, NEG_BIG)
        qs_ref[hh] = qh * jnp.asarray(HEAD_DIM ** -0.5, BF16)
        far.append(rb_ref[REL_BUCKETS - 1, 2 * pair + hh])
    _reset_state(m_ref, acc_ref)

    def tile(n, hh, bias):
        ks = pl.multiple_of(n * KT, KT)
        _attn_step(qs_ref[hh], k_ref[pl.ds(ks, KT), :], v_refs[hh][:, pl.ds(ks, KT)], bias, m_ref, acc_ref, hh)

    for hh in range(2):
        tile(cur, hh, bt_ref[hh, BT_OWN])

    @pl.when(cur >= 1)
    def _():
        for hh in range(2):
            tile(cur - 1, hh, bt_ref[hh, BT_PREV] + mb_ref[hh, pl.ds(cur - 1, 1), :])

    n_far = jnp.maximum(cur - 1, 0)

    @pl.when((n_far & 1) == 1)
    def _():
        for hh in range(2):
            tile(n_far - 1, hh, mb_ref[hh, pl.ds(n_far - 1, 1), :] + far[hh])

    def far_pair(it, carry):
        n0 = 2 * it
        ks = pl.multiple_of(n0 * KT, 2 * KT)
        k_tile = k_ref[pl.ds(ks, 2 * KT), :]
        for hh in range(2):
            bias = jnp.concatenate(
                [jnp.broadcast_to(mb_ref[hh, pl.ds(n0 + i, 1), :] + far[hh], (KT, TQ)) for i in range(2)],
                axis=0)
            _attn_step(qs_ref[hh], k_tile, v_refs[hh][:, pl.ds(ks, 2 * KT)], bias, m_ref, acc_ref, hh)
        return carry

    lax.fori_loop(0, lax.shift_right_logical(n_far, 1), far_pair, 0)
    o_ref[...] = _pair_output(acc_ref[0], acc_ref[1]).astype(o_ref.dtype)


def _moba_attention(rel_bias, tok, fm, kmean, bias_tiles):
    b, s, _ = tok.shape
    nb = s // MOBA_BLOCK
    npair = MOBA_HEADS // 2
    return pl.pallas_call(
        _moba_kernel,
        grid=(b, npair, s // TQ),
        in_specs=[
            _SMEM_SPEC,
            pl.BlockSpec((None, LANES, TQ), lambda i, p, c: (i, ROW_MQ // LANES + p, c)),
            pl.BlockSpec((None, s, LANES), lambda i, p, c: (i, 0, COL_MK // LANES + p)),
            pl.BlockSpec((None, LANES, s), lambda i, p, c: (i, ROW_MV // LANES + p, 0)),
            pl.BlockSpec((None, nb, LANES), lambda i, p, c: (i, 0, p)),
            pl.BlockSpec((2, N_BIAS_TILES, KT, TQ), lambda i, p, c: (p, 0, 0, 0)),
        ],
        out_specs=pl.BlockSpec((None, TQ, LANES), lambda i, p, c: (i, c, p)),
        out_shape=jax.ShapeDtypeStruct((b, s, MOBA_W), BF16),
        scratch_shapes=[
            pltpu.VMEM((VROWS, s), BF16), pltpu.VMEM((VROWS, s), BF16),
            pltpu.VMEM((2, nb, TQ), F32),
            pltpu.VMEM((2, LANES, TQ), BF16),
            pltpu.VMEM((2, 1, TQ), F32),
            pltpu.VMEM((2, VROWS, TQ), F32),
        ],
        compiler_params=_params(3),
        name="moba_attention",
    )(rel_bias, fm, tok, fm, kmean, bias_tiles)


def _compress_kernel(ck_ref, cv_ref, pk_ref, pv_ref, w1k_ref, w1v_ref, w2k_ref, w2v_ref,
                     kc_ref, vct_ref):
    nch = ck_ref.shape[1]
    half = CMP_STRIDE * HEAD_DIM

    def run(c_ref, pos_ref, w1_ref, w2_ref):
        acc = jnp.zeros((nch, LANES), F32)
        for g in range(NSA_GROUPS):
            c = c_ref[g].astype(F32)
            lo = (c + pos_ref[0:1, :]).astype(BF16)
            hi = (c + pos_ref[1:2, :]).astype(BF16)
            pre = _dot(lo, w1_ref[:half, :]) + pltpu.roll(_dot(hi, w1_ref[half:, :]), nch - 1, axis=0)
            acc = acc + _dot(jax.nn.gelu(pre).astype(BF16), w2_ref[g])
        return acc

    kc_ref[...] = run(ck_ref, pk_ref, w1k_ref, w2k_ref).astype(kc_ref.dtype)
    vct_ref[...] = jnp.transpose(run(cv_ref, pv_ref, w1v_ref, w2v_ref)).astype(vct_ref.dtype)


def _compress(ck, cv, pos_k, pos_v, w1k, w1v, w2k, w2v):
    b, g, nch, width = ck.shape
    full = lambda shape: pl.BlockSpec(shape, lambda i: (0,) * len(shape))
    return pl.pallas_call(
        _compress_kernel,
        grid=(b,),
        in_specs=[pl.BlockSpec((None, g, nch, width), lambda i: (i, 0, 0, 0)),
                  pl.BlockSpec((None, g, nch, width), lambda i: (i, 0, 0, 0)),
                  full(pos_k.shape), full(pos_v.shape), full(w1k.shape), full(w1v.shape),
                  full(w2k.shape), full(w2v.shape)],
        out_specs=[pl.BlockSpec((None, nch, LANES), lambda i: (i, 0, 0)),
                   pl.BlockSpec((None, LANES, nch), lambda i: (i, 0, 0))],
        out_shape=[jax.ShapeDtypeStruct((b, nch, LANES), BF16),
                   jax.ShapeDtypeStruct((b, LANES, nch), BF16)],
        compiler_params=_params(1),
        name="nsa_compress",
    )(ck, cv, pos_k, pos_v, w1k, w1v, w2k, w2v)


def _cmp_select_kernel(qt_ref, kc_ref, vct_ref, mimp_ref, o_ref, sel_ref, *, nsb):
    ci = pl.program_id(1)
    nch = kc_ref.shape[0]
    nsbp = mimp_ref.shape[0]
    qs = ci * TQ
    n_row = lax.broadcasted_iota(jnp.int32, (nch, TQ), 0)
    t_col = qs + lax.broadcasted_iota(jnp.int32, (nch, TQ), 1)
    visible = (n_row * CMP_STRIDE + (CMP_BLOCK - 1)) <= t_col
    any_visible = t_col[0:1, :] >= (CMP_BLOCK - 1)
    kc = kc_ref[...]
    vct = vct_ref[...]
    vrow = lax.broadcasted_iota(jnp.int32, vct.shape, 0)
    mimp = mimp_ref[...]
    scale = jnp.asarray(HEAD_DIM ** -0.5, BF16)

    j_row = lax.broadcasted_iota(jnp.int32, (nsbp, TQ), 0)
    t_sel = qs + lax.broadcasted_iota(jnp.int32, (nsbp, TQ), 1)
    cur = lax.shift_right_logical(t_sel, int(math.log2(SLC_BLOCK)))
    j_f = j_row.astype(F32)
    forced = (j_row == 0) | (j_row == cur) | (j_row == cur - 1)
    allowed = (j_row <= cur) & (j_row < nsb)

    q_heads = [_split_heads_t(qt_ref[r * LANES:(r + 1) * LANES, :] * scale) for r in range(NSA_HPG)]
    out_t = [jnp.zeros((LANES, TQ), F32) for _ in range(NSA_HPG)]
    for g in range(NSA_GROUPS):
        vin_g = (vrow < HEAD_DIM) if g == 0 else (vrow >= HEAD_DIM)
        vct_g = jnp.where(vin_g, vct, jnp.zeros_like(vct))
        p_grp = jnp.zeros((nch, TQ), F32)
        for r in range(NSA_HPG):
            s = jnp.where(visible, _dot(kc, q_heads[r][g]), NEG_BIG)
            e = jnp.exp(s - jnp.max(s, axis=0, keepdims=True))
            p = e / jnp.sum(e, axis=0, keepdims=True)
            p = jnp.where(any_visible, p, 0.0)
            p_grp = p_grp + p
            out_t[r] = out_t[r] + _dot(vct_g, p.astype(BF16))
        p_hi = p_grp.astype(BF16)
        p_lo = (p_grp - p_hi.astype(F32)).astype(BF16)
        imp = _dot(mimp, p_hi) + _dot(mimp, p_lo)
        val = jnp.where(forced & allowed, jnp.inf, jnp.where(allowed, imp, -jnp.inf))
        chosen = _top_k_rows(val, j_f, min(SLC_TOPN, nsb), float(nsbp))
        sel_ref[g] = jnp.where(allowed & (chosen > 0.5), 0.0, NEG_BIG)
    for r in range(NSA_HPG):
        o_ref[:, r * LANES:(r + 1) * LANES] = jnp.transpose(out_t[r]).astype(o_ref.dtype)


def _cmp_select(fm, kc, vct, mimp_t, nsb):
    b, _, s = fm.shape
    nch = kc.shape[1]
    nsbp = mimp_t.shape[0]
    return pl.pallas_call(
        functools.partial(_cmp_select_kernel, nsb=nsb),
        grid=(b, s // TQ),
        in_specs=[pl.BlockSpec((None, NSA_W, TQ), lambda i, c: (i, ROW_NQ // NSA_W, c)),
                  pl.BlockSpec((None, nch, LANES), lambda i, c: (i, 0, 0)),
                  pl.BlockSpec((None, LANES, nch), lambda i, c: (i, 0, 0)),
                  pl.BlockSpec((nsbp, nch), lambda i, c: (0, 0))],
        out_specs=[pl.BlockSpec((None, TQ, NSA_W), lambda i, c: (i, c, 0)),
                   pl.BlockSpec((None, NSA_GROUPS, nsbp, TQ), lambda i, c: (i, 0, 0, c))],
        out_shape=[jax.ShapeDtypeStruct((b, s, NSA_W), BF16),
                   jax.ShapeDtypeStruct((b, NSA_GROUPS, nsbp, s), F32)],
        compiler_params=_params(2),
        name="nsa_cmp_select",
    )(fm, kc, vct, mimp_t)


def _slc_win_kernel(rb_ref, qt_ref, ks_ref, vst_ref, kw_ref, vwt_ref, sel_ref, bt_ref,
                    oslc_ref, owin_ref,
                    vs0_ref, vs1_ref, vw0_ref, vw1_ref, qs_ref, m_ref, acc_ref):
    cur = pl.program_id(1)

    @pl.when(cur == 0)
    def _():
        _fill_v_with_ones(vst_ref, vs0_ref, vs1_ref)
        _fill_v_with_ones(vwt_ref, vw0_ref, vw1_ref)

    scale = jnp.asarray(HEAD_DIM ** -0.5, BF16)
    nh = NSA_HEADS
    per_tile = KT // SLC_BLOCK
    for r in range(NSA_HPG):
        lo, hi = _split_heads_t(qt_ref[r * LANES:(r + 1) * LANES, :] * scale)
        qs_ref[r] = lo
        qs_ref[NSA_HPG + r] = hi
    _reset_state(m_ref, acc_ref)
    far = [rb_ref[REL_BUCKETS - 1, MOBA_HEADS + h] for h in range(nh)]
    vs_refs = (vs0_ref, vs1_ref)
    vw_refs = (vw0_ref, vw1_ref)

    def sel_rows(g, n):
        return jnp.concatenate(
            [jnp.broadcast_to(sel_ref[g, pl.ds(n * per_tile + i, 1), :], (SLC_BLOCK, TQ))
             for i in range(per_tile)], axis=0)

    def sel_tile(n, tile_id):
        ks = pl.multiple_of(n * KT, KT)
        k_tile = ks_ref[pl.ds(ks, KT), :]
        for g in range(NSA_GROUPS):
            mask = sel_rows(g, n)
            vt_tile = vs_refs[g][:, pl.ds(ks, KT)]
            for r in range(NSA_HPG):
                h = g * NSA_HPG + r
                bias = mask + (far[h] if tile_id is None else bt_ref[h, tile_id])
                _attn_step(qs_ref[h], k_tile, vt_tile, bias, m_ref, acc_ref, h)

    def win_tile(n, tile_id):
        ks = pl.multiple_of(n * KT, KT)
        k_tile = kw_ref[pl.ds(ks, KT), :]
        for g in range(NSA_GROUPS):
            vt_tile = vw_refs[g][:, pl.ds(ks, KT)]
            for r in range(NSA_HPG):
                h = g * NSA_HPG + r
                _attn_step(qs_ref[h], k_tile, vt_tile, bt_ref[h, tile_id], m_ref, acc_ref, nh + h)

    sel_tile(cur, BT_OWN)
    win_tile(cur, BT_OWN)

    @pl.when(cur >= 1)
    def _():
        sel_tile(cur - 1, BT_PREV)
        win_tile(cur - 1, BT_PREV)

    @pl.when(cur >= 2)
    def _():
        win_tile(cur - 2, BT_UPPER)

    def far_tile(it, carry):
        sel_tile(cur - 2 - it, None)
        return carry

    lax.fori_loop(0, jnp.maximum(cur - 1, 0), far_tile, 0)

    for r in range(NSA_HPG):
        cols = slice(r * LANES, (r + 1) * LANES)
        oslc_ref[:, cols] = _pair_output(acc_ref[r], acc_ref[NSA_HPG + r]).astype(oslc_ref.dtype)
        owin_ref[:, cols] = _pair_output(acc_ref[nh + r], acc_ref[nh + NSA_HPG + r]).astype(owin_ref.dtype)


def _slc_win(rel_bias, tok, fm, sel, bias_tiles):
    b, s, _ = tok.shape
    nsbp = sel.shape[2]
    return pl.pallas_call(
        _slc_win_kernel,
        grid=(b, s // TQ),
        in_specs=[_SMEM_SPEC,
                  pl.BlockSpec((None, NSA_W, TQ), lambda i, c: (i, ROW_NQ // NSA_W, c)),
                  pl.BlockSpec((None, s, LANES), lambda i, c: (i, 0, COL_NKS // LANES)),
                  pl.BlockSpec((None, LANES, s), lambda i, c: (i, ROW_NVS // LANES, 0)),
                  pl.BlockSpec((None, s, LANES), lambda i, c: (i, 0, COL_NKW // LANES)),
                  pl.BlockSpec((None, LANES, s), lambda i, c: (i, ROW_NVW // LANES, 0)),
                  pl.BlockSpec((None, NSA_GROUPS, nsbp, TQ), lambda i, c: (i, 0, 0, c)),
                  pl.BlockSpec((NSA_HEADS, N_BIAS_TILES, KT, TQ), lambda i, c: (1, 0, 0, 0))],
        out_specs=[pl.BlockSpec((None, TQ, NSA_W), lambda i, c: (i, c, 0)),
                   pl.BlockSpec((None, TQ, NSA_W), lambda i, c: (i, c, 0))],
        out_shape=[jax.ShapeDtypeStruct((b, s, NSA_W), BF16),
                   jax.ShapeDtypeStruct((b, s, NSA_W), BF16)],
        scratch_shapes=[pltpu.VMEM((VROWS, s), BF16)] * 4 + [
            pltpu.VMEM((NSA_HEADS, LANES, TQ), BF16),
            pltpu.VMEM((2 * NSA_HEADS, 1, TQ), F32),
            pltpu.VMEM((2 * NSA_HEADS, VROWS, TQ), F32)],
        compiler_params=_params(2),
        name="nsa_slc_win",
    )(rel_bias, fm, tok, fm, tok, fm, sel, bias_tiles)


def _merge_kernel(x_ref, oa_ref, oc_ref, os_ref, ow_ref, ng_ref, ga_ref, gb_ref,
                  eg_ref, wa_ref, wb_ref, wo_ref, o_ref):
    sig = jax.nn.sigmoid(ng_ref[...].astype(F32))
    s_hi = sig.astype(BF16)
    s_lo = (sig - s_hi.astype(F32)).astype(BF16)
    eg = eg_ref[...]
    gexp = _dot(s_hi, eg) + _dot(s_lo, eg)
    o_b = (gexp[:, :NSA_W] * oc_ref[...].astype(F32)
           + gexp[:, NSA_W:2 * NSA_W] * os_ref[...].astype(F32)
           + gexp[:, 2 * NSA_W:] * ow_ref[...].astype(F32))
    a = _dot(oa_ref[...], wa_ref[...])
    bb = _dot(o_b.astype(BF16), wb_ref[...])
    merged = (jax.nn.sigmoid(ga_ref[...].astype(F32)) * a
              + jax.nn.sigmoid(gb_ref[...].astype(F32)) * bb)
    o_ref[...] = x_ref[...] + _dot(merged.astype(BF16), wo_ref[...])


def _merge(x2d, tok2d, o_a, o_c, o_s, o_w, egate, wa, wb, wo, tm):
    m, d = x2d.shape
    row = lambda w, col=0: pl.BlockSpec((tm, w), lambda i, col=col: (i, col))
    full = lambda a: pl.BlockSpec(a.shape, lambda i: (0, 0))
    return pl.pallas_call(
        _merge_kernel,
        grid=(m // tm,),
        in_specs=[row(d), row(MOBA_W), row(NSA_W), row(NSA_W), row(NSA_W),
                  row(GATE_PAD, COL_NGATE // GATE_PAD),
                  row(D_MODEL, COL_GA // D_MODEL), row(D_MODEL, COL_GB // D_MODEL),
                  full(egate), full(wa), full(wb), full(wo)],
        out_specs=row(d),
        out_shape=jax.ShapeDtypeStruct((m, d), F32),
        compiler_params=_params(1),
        name="merge_out",
    )(x2d, o_a, o_c, o_s, o_w, tok2d, tok2d, tok2d, egate, wa, wb, wo)


def _xattn_kernel(x_ref, g_ref, wq_ref, kv_ref, wo_ref, o_ref):
    x = x_ref[...]
    q = _dot(_rms(x, g_ref[...]).astype(BF16), wq_ref[...]).astype(BF16)
    heads = []
    for h in range(XATTN_HEADS):
        cols = slice(h * XATTN_HEAD_DIM, (h + 1) * XATTN_HEAD_DIM)
        k = kv_ref[:, cols]
        v = kv_ref[:, XATTN_W + h * XATTN_HEAD_DIM:XATTN_W + (h + 1) * XATTN_HEAD_DIM]
        s = _dot_nt(q[:, cols], k) * (XATTN_HEAD_DIM ** -0.5)
        e = jnp.exp(s - jnp.max(s, axis=1, keepdims=True))
        p = e / jnp.sum(e, axis=1, keepdims=True)
        heads.append(_dot(p.astype(BF16), v).astype(BF16))
    o_ref[...] = x + _dot(jnp.concatenate(heads, axis=1), wo_ref[...])


def _xattn(x3d, g, wq, memkv, wo, tm):
    b, s, d = x3d.shape
    mem_len = memkv.shape[1]
    return pl.pallas_call(
        _xattn_kernel,
        grid=(b, s // tm),
        in_specs=[pl.BlockSpec((None, tm, d), lambda i, j: (i, j, 0)),
                  pl.BlockSpec((1, d), lambda i, j: (0, 0)),
                  pl.BlockSpec(wq.shape, lambda i, j: (0, 0)),
                  pl.BlockSpec((None, mem_len, 2 * XATTN_W), lambda i, j: (i, 0, 0)),
                  pl.BlockSpec(wo.shape, lambda i, j: (0, 0))],
        out_specs=pl.BlockSpec((None, tm, d), lambda i, j: (i, j, 0)),
        out_shape=jax.ShapeDtypeStruct((b, s, d), F32),
        compiler_params=_params(2),
        name="xattn",
    )(x3d, g.reshape(1, d), wq, memkv, wo)


def _ffn_kernel(x_ref, halo_ref, g_ref, wg_ref, wu_ref, cw_ref, cb_ref, wd_ref, gf_ref, o_ref,
                acc_ref, *, chunk, halo_rows):
    j = pl.program_id(1)
    x = x_ref[...]
    tm = x.shape[0]
    hf = _rms(x, g_ref[...]).astype(BF16)
    keep = jnp.where(j > 0, 1.0, 0.0)
    hh = _rms(halo_ref[...], g_ref[...]).astype(BF16)
    row = lax.broadcasted_iota(jnp.int32, (tm, chunk), 0)
    acc_ref[...] = jnp.zeros_like(acc_ref)
    for c0 in range(0, D_FF, chunk):
        cols = slice(c0, c0 + chunk)
        u = _dot(hf, wg_ref[:, cols])
        uh = _dot(hh, wg_ref[:, cols]) * keep
        prev1 = uh[halo_rows - 1:halo_rows, :]
        prev2 = uh[halo_rows - 2:halo_rows - 1, :]
        u1 = jnp.where(row == 0, prev1, pltpu.roll(u, 1, axis=0))
        u2 = jnp.where(row == 0, prev2, jnp.where(row == 1, prev1, pltpu.roll(u, 2, axis=0)))
        a = cw_ref[0:1, cols] * u2 + cw_ref[1:2, cols] * u1 + cw_ref[2:3, cols] * u + cb_ref[:, cols]
        y = jax.nn.gelu(a) * _dot(hf, wu_ref[:, cols])
        acc_ref[...] += _dot(y.astype(BF16), wd_ref[cols, :])
    o_ref[...] = _rms(x + acc_ref[...], gf_ref[...])


def _ffn(x3d, g, wg, wu, cw, cb, wd, gf, tm):
    b, s, d = x3d.shape
    halo_rows = 8
    per = tm // halo_rows
    return pl.pallas_call(
        functools.partial(_ffn_kernel, chunk=256, halo_rows=halo_rows),
        grid=(b, s // tm),
        in_specs=[pl.BlockSpec((None, tm, d), lambda i, j: (i, j, 0)),
                  pl.BlockSpec((None, halo_rows, d), lambda i, j: (i, jnp.maximum(j * per - 1, 0), 0)),
                  pl.BlockSpec((1, d), lambda i, j: (0, 0)),
                  pl.BlockSpec(wg.shape, lambda i, j: (0, 0)),
                  pl.BlockSpec(wu.shape, lambda i, j: (0, 0)),
                  pl.BlockSpec(cw.shape, lambda i, j: (0, 0)),
                  pl.BlockSpec((1, D_FF), lambda i, j: (0, 0)),
                  pl.BlockSpec(wd.shape, lambda i, j: (0, 0)),
                  pl.BlockSpec((1, d), lambda i, j: (0, 0))],
        out_specs=pl.BlockSpec((None, tm, d), lambda i, j: (i, j, 0)),
        out_shape=jax.ShapeDtypeStruct((b, s, d), F32),
        scratch_shapes=[pltpu.VMEM((tm, d), F32)],
        compiler_params=_params(2),
        name="conv_ffn",
    )(x3d, x3d, g.reshape(1, d), wg, wu, cw, cb.reshape(1, D_FF), wd, gf.reshape(1, d))


def _nq_perm():
    idx = []
    for r in range(NSA_HPG):
        for g in range(NSA_GROUPS):
            h = g * NSA_HPG + r
            idx.extend(range(h * HEAD_DIM, (h + 1) * HEAD_DIM))
    return np.asarray(idx, np.int32)


def _gate_expand():
    e = np.zeros((GATE_PAD, 3 * NSA_W), np.float32)
    for br in range(3):
        for g in range(NSA_GROUPS):
            for r in range(NSA_HPG):
                c0 = br * NSA_W + r * LANES + g * HEAD_DIM
                e[br * NSA_HEADS + g * NSA_HPG + r, c0:c0 + HEAD_DIM] = 1.0
    return e


def _importance_matrix(nch, nsb, nsbp):
    ratio = SLC_BLOCK // CMP_STRIDE
    nc = nch - 1
    m = np.zeros((nsbp, nch), np.float32)
    for j in range(nsb):
        for a in range(ratio):
            for bb in range(CMP_BLOCK // CMP_STRIDE):
                c = j * ratio + a - bb
                if 0 <= c < nc:
                    m[j, c] += 1.0
    return m


def _split_w_in(w_in):
    widths = [MOBA_W] * 3 + [NSA_W] + [NSA_KV_W] * 6 + [3 * NSA_HEADS, D_MODEL, D_MODEL]
    cuts = np.cumsum(widths)[:-1]
    mq, mk, mv, nq, nkc, nvc, nks, nvs, nkw, nvw, ngate, ga, gb = jnp.split(w_in, cuts, axis=1)
    ngate = jnp.pad(ngate, ((0, 0), (0, GATE_PAD - 3 * NSA_HEADS)))
    w_tok = jnp.concatenate([ga, gb, mk, nkc, nvc, nks, nkw, ngate], axis=1)
    w_fm = jnp.concatenate([mq, mv, nq[:, _nq_perm()], nvs, nvw], axis=1)
    return w_tok.astype(BF16), jnp.transpose(w_fm).astype(BF16)


def _layer(x, mem, rel_bias, bias_tiles, norm_mix_g, w_in, cmp_pos_k, cmp_w1_k, cmp_w2_k, cmp_pos_v,
           cmp_w1_v, cmp_w2_v, w_branch_a, w_branch_b, w_out, norm_xattn_g, norm_mem_g, w_xq, w_xkv, w_xo,
           norm_ffn_g, w_gate, w_up, conv_w, conv_b, w_down, norm_out_g):
    b, s, d = x.shape
    assert s % 1024 == 0 and d == D_MODEL
    rows = b * s
    nch = s // CMP_STRIDE
    nsb = s // SLC_BLOCK
    nsbp = max(LANES, nsb)

    w_tok, w_fm_t = _split_w_in(w_in)
    tok, fm = _in_proj(x, norm_mix_g, w_tok, w_fm_t, tm=512)
    tok2d = tok.reshape(rows, TOK_W)

    kmean = _moba_kmean(tok)
    o_a = _moba_attention(rel_bias, tok, fm, kmean, bias_tiles)

    def chunks(col):
        t = tok[:, :, col:col + NSA_KV_W].reshape(b, s, NSA_GROUPS, HEAD_DIM)
        return t.transpose(0, 2, 1, 3).reshape(b, NSA_GROUPS, nch, CMP_STRIDE * HEAD_DIM)

    def w2_pad(w2):
        z = jnp.zeros_like(w2)
        return jnp.stack([jnp.concatenate([w2, z], axis=1), jnp.concatenate([z, w2], axis=1)]).astype(BF16)

    kc, vct = _compress(chunks(COL_NKC), chunks(COL_NVC),
                        cmp_pos_k.reshape(2, -1), cmp_pos_v.reshape(2, -1),
                        cmp_w1_k.astype(BF16), cmp_w1_v.astype(BF16), w2_pad(cmp_w2_k), w2_pad(cmp_w2_v))
    mimp_t = jnp.asarray(_importance_matrix(nch, nsb, nsbp), BF16)
    o_c, sel = _cmp_select(fm, kc, vct, mimp_t, nsb)
    o_s, o_w = _slc_win(rel_bias, tok, fm, sel, bias_tiles)

    x1 = _merge(x.reshape(rows, d), tok2d, o_a.reshape(rows, MOBA_W), o_c.reshape(rows, NSA_W),
                o_s.reshape(rows, NSA_W), o_w.reshape(rows, NSA_W), jnp.asarray(_gate_expand(), BF16),
                w_branch_a.astype(BF16), w_branch_b[_nq_perm(), :].astype(BF16), w_out.astype(BF16), tm=512)

    mem_len = mem.shape[1]
    memkv = _norm_matmul(mem.reshape(b * mem_len, d), norm_mem_g, w_xkv.astype(BF16), tm=mem_len,
                         name="mem_kv").reshape(b, mem_len, 2 * XATTN_W)
    x2 = _xattn(x1.reshape(b, s, d), norm_xattn_g, w_xq.astype(BF16), memkv, w_xo.astype(BF16), tm=512)

    return _ffn(x2, norm_ffn_g, w_gate.astype(BF16), w_up.astype(BF16), conv_w, conv_b,
                w_down.astype(BF16), norm_out_g, tm=512)


def kernel(x, mem, rel_bias, norm_mix_g, w_in, cmp_pos_k, cmp_w1_k, cmp_w2_k, cmp_pos_v, cmp_w1_v, cmp_w2_v, w_branch_a, w_branch_b, w_out, norm_xattn_g, norm_mem_g, w_xq, w_xkv, w_xo, norm_ffn_g, w_gate, w_up, conv_w, conv_b, w_down, norm_final_g):
    depth = w_in.shape[0]
    assert depth == 1, "the final norm is fused into the last layer's FFN kernel"
    bias_tiles = _bias_tiles(rel_bias)
    l = 0
    return _layer(x, mem, rel_bias, bias_tiles, norm_mix_g[l], w_in[l], cmp_pos_k[l], cmp_w1_k[l],
                  cmp_w2_k[l], cmp_pos_v[l], cmp_w1_v[l], cmp_w2_v[l], w_branch_a[l], w_branch_b[l], w_out[l],
                  norm_xattn_g[l], norm_mem_g[l], w_xq[l], w_xkv[l], w_xo[l],
                  norm_ffn_g[l], w_gate[l], w_up[l], conv_w[l], conv_b[l], w_down[l], norm_final_g)
```

```python
import functools
import math

import numpy as np
import jax
import jax.numpy as jnp
from jax import lax
from jax.experimental import pallas as pl
from jax.experimental.pallas import tpu as pltpu

F32 = jnp.float32
BF16 = jnp.bfloat16

D_MODEL = 1024
HEAD_DIM = 64
MOBA_HEADS = 8
MOBA_BLOCK = 256
MOBA_TOPK = 3
NSA_HEADS = 8
NSA_GROUPS = 2
NSA_HPG = NSA_HEADS // NSA_GROUPS
CMP_BLOCK = 32
CMP_STRIDE = 16
CMP_HIDDEN = 256
SLC_BLOCK = 64
SLC_TOPN = 16
WINDOW = 512
REL_BUCKETS = 32
REL_MAX_DIST = 128
XATTN_HEADS = 4
XATTN_HEAD_DIM = 128
D_FF = 2816
CONV_WIDTH = 3
Q_BLOCK = 128
RMS_EPS = 1e-6
NEG_BIG = -1e30

LANES = 128
TQ = 256
KT = 256
STRIP = 128
ONES_ROWS = 16
VROWS = HEAD_DIM + ONES_ROWS
VMEM_LIMIT = 56 * 1024 * 1024

MOBA_W = MOBA_HEADS * HEAD_DIM
NSA_W = NSA_HEADS * HEAD_DIM
NSA_KV_W = NSA_GROUPS * HEAD_DIM
XATTN_W = XATTN_HEADS * XATTN_HEAD_DIM
GATE_PAD = LANES

COL_GA = 0
COL_GB = COL_GA + D_MODEL
COL_MK = COL_GB + D_MODEL
COL_NKC = COL_MK + MOBA_W
COL_NVC = COL_NKC + NSA_KV_W
COL_NKS = COL_NVC + NSA_KV_W
COL_NKW = COL_NKS + NSA_KV_W
COL_NGATE = COL_NKW + NSA_KV_W
TOK_W = COL_NGATE + GATE_PAD
ROW_MQ = 0
ROW_MV = ROW_MQ + MOBA_W
ROW_NQ = ROW_MV + MOBA_W
ROW_NVS = ROW_NQ + NSA_W
ROW_NVW = ROW_NVS + NSA_KV_W
FM_W = ROW_NVW + NSA_KV_W

BT_OWN, BT_PREV, BT_UPPER, BT_MASKED = 0, 1, 2, 3
N_BIAS_TILES = 4


def _bucket_thresholds():
    n = np.arange(0, 4 * REL_MAX_DIST)
    exact = REL_BUCKETS // 2
    nf = np.maximum(n, 1).astype(np.float64)
    large = exact + (np.log(nf / exact) / math.log(REL_MAX_DIST / exact) * (REL_BUCKETS - exact)).astype(np.int64)
    large = np.minimum(large, REL_BUCKETS - 1)
    b = np.where(n < exact, n, large)
    return [int(np.argmax(b >= k)) for k in range(REL_BUCKETS)]


_THR = _bucket_thresholds()
assert _THR[REL_BUCKETS - 1] <= LANES


def _dot(a, b):
    return jnp.dot(a, b, preferred_element_type=F32)


def _dot_nt(a, b):
    return lax.dot_general(a, b, (((1,), (1,)), ((), ())), preferred_element_type=F32)


def _rms(x, g):
    return x * lax.rsqrt(jnp.mean(x * x, axis=-1, keepdims=True) + RMS_EPS) * g


def _params(n_axes):
    return pltpu.CompilerParams(dimension_semantics=("arbitrary",) * n_axes,
                                vmem_limit_bytes=VMEM_LIMIT)


def _bias_kernel(rb_ref, out_ref):
    h = pl.program_id(0)
    j = lax.broadcasted_iota(jnp.int32, (KT, TQ), 0)
    i = lax.broadcasted_iota(jnp.int32, (KT, TQ), 1)
    far = rb_ref[REL_BUCKETS - 1, h]

    def table(d):
        t = jnp.full((KT, TQ), rb_ref[0, h] - far, F32)
        for k in range(1, REL_BUCKETS):
            t = jnp.where(d >= _THR[k], rb_ref[k, h] - far, t)
        return t

    neg = jnp.full((KT, TQ), NEG_BIG, F32)
    out_ref[0, BT_OWN] = jnp.where(i >= j, table(i - j), neg)
    out_ref[0, BT_PREV] = table(i - j + KT)
    out_ref[0, BT_UPPER] = jnp.where(j > i, 0.0, neg)
    out_ref[0, BT_MASKED] = neg


def _bias_tiles(rel_bias):
    nh = rel_bias.shape[1]
    return pl.pallas_call(
        _bias_kernel,
        grid=(nh,),
        in_specs=[pl.BlockSpec(memory_space=pltpu.SMEM)],
        out_specs=pl.BlockSpec((1, N_BIAS_TILES, KT, TQ), lambda h: (h, 0, 0, 0)),
        out_shape=jax.ShapeDtypeStruct((nh, N_BIAS_TILES, KT, TQ), F32),
        compiler_params=_params(1),
        name="bias_tiles",
    )(rel_bias)


def _inproj_kernel(x_ref, g_ref, wt_ref, wf_ref, tok_ref, fm_ref, *, chunk):
    h = _rms(x_ref[...], g_ref[...]).astype(BF16)
    n = wt_ref.shape[1]
    for c0 in range(0, n, chunk):
        c1 = min(c0 + chunk, n)
        tok_ref[:, c0:c1] = _dot(h, wt_ref[:, c0:c1]).astype(tok_ref.dtype)
    n = wf_ref.shape[0]
    for c0 in range(0, n, chunk):
        c1 = min(c0 + chunk, n)
        fm_ref[c0:c1, :] = _dot_nt(wf_ref[c0:c1, :], h).astype(fm_ref.dtype)


def _in_proj(x, g, w_tok, w_fm_t, tm):
    b, s, d = x.shape
    return pl.pallas_call(
        functools.partial(_inproj_kernel, chunk=512),
        grid=(b, s // tm),
        in_specs=[pl.BlockSpec((None, tm, d), lambda i, j: (i, j, 0)),
                  pl.BlockSpec((1, d), lambda i, j: (0, 0)),
                  pl.BlockSpec(w_tok.shape, lambda i, j: (0, 0)),
                  pl.BlockSpec(w_fm_t.shape, lambda i, j: (0, 0))],
        out_specs=[pl.BlockSpec((None, tm, TOK_W), lambda i, j: (i, j, 0)),
                   pl.BlockSpec((None, FM_W, tm), lambda i, j: (i, 0, j))],
        out_shape=[jax.ShapeDtypeStruct((b, s, TOK_W), BF16),
                   jax.ShapeDtypeStruct((b, FM_W, s), BF16)],
        compiler_params=_params(2),
        name="in_proj",
    )(x, g.reshape(1, d), w_tok, w_fm_t)


def _norm_matmul_kernel(x_ref, g_ref, w_ref, o_ref):
    o_ref[...] = _dot(_rms(x_ref[...], g_ref[...]).astype(BF16), w_ref[...]).astype(o_ref.dtype)


def _norm_matmul(x2d, g, w, tm, name):
    m, d = x2d.shape
    n = w.shape[1]
    return pl.pallas_call(
        _norm_matmul_kernel,
        grid=(m // tm,),
        in_specs=[pl.BlockSpec((tm, d), lambda i: (i, 0)),
                  pl.BlockSpec((1, d), lambda i: (0, 0)),
                  pl.BlockSpec((d, n), lambda i: (0, 0))],
        out_specs=pl.BlockSpec((tm, n), lambda i: (i, 0)),
        out_shape=jax.ShapeDtypeStruct((m, n), BF16),
        compiler_params=_params(1),
        name=name,
    )(x2d, g.reshape(1, d), w)


def _kmean_kernel(k_ref, o_ref):
    nb = o_ref.shape[0]
    for n in range(nb):
        blk = k_ref[n * MOBA_BLOCK:(n + 1) * MOBA_BLOCK, :].astype(F32)
        o_ref[n:n + 1, :] = jnp.mean(blk, axis=0, keepdims=True)


def _moba_kmean(tok):
    b, s, _ = tok.shape
    nb = s // MOBA_BLOCK
    return pl.pallas_call(
        _kmean_kernel,
        grid=(b,),
        in_specs=[pl.BlockSpec((None, s, MOBA_W), lambda i: (i, 0, COL_MK // MOBA_W))],
        out_specs=pl.BlockSpec((None, nb, MOBA_W), lambda i: (i, 0, 0)),
        out_shape=jax.ShapeDtypeStruct((b, nb, MOBA_W), F32),
        compiler_params=_params(1),
        name="moba_kmean",
    )(tok)


def _fill_v_with_ones(vt_ref, lo_ref, hi_ref, cols=1024):
    s = vt_ref.shape[1]
    ones = jnp.ones((ONES_ROWS, cols), BF16)

    def body(c, carry):
        c0 = pl.multiple_of(c * cols, cols)
        lo_ref[0:HEAD_DIM, pl.ds(c0, cols)] = vt_ref[0:HEAD_DIM, pl.ds(c0, cols)]
        lo_ref[HEAD_DIM:VROWS, pl.ds(c0, cols)] = ones
        hi_ref[0:HEAD_DIM, pl.ds(c0, cols)] = vt_ref[HEAD_DIM:2 * HEAD_DIM, pl.ds(c0, cols)]
        hi_ref[HEAD_DIM:VROWS, pl.ds(c0, cols)] = ones
        return carry

    lax.fori_loop(0, s // cols, body, 0)


def _split_heads_t(q_t):
    row = lax.broadcasted_iota(jnp.int32, q_t.shape, 0)
    zero = jnp.zeros_like(q_t)
    return jnp.where(row < HEAD_DIM, q_t, zero), jnp.where(row >= HEAD_DIM, q_t, zero)


def _pipe(s_refs, mx_ref, p_ref, m_ref, acc_ref, nheads, a=None, bc=None):
    rows = s_refs[0].shape[1]
    for h in range(nheads):
        if a is not None:
            a_slot, score_fn, bias_fn = a
            raw = score_fn(h)
            mx = None
        if bc is not None:
            b_slot, base, vt_fn = bc
            m_prev = m_ref[base + h]
            m_next = jnp.maximum(m_prev, mx_ref[b_slot, h])
            alpha = jnp.exp(m_prev - m_next)
        for i in range(rows // STRIP):
            r = slice(i * STRIP, (i + 1) * STRIP)
            if bc is not None:
                p_ref[h, r, :] = jnp.exp(s_refs[b_slot][h, r, :] - m_next).astype(BF16)
            if a is not None:
                s = raw[r, :] + bias_fn(h, i)
                s_refs[a_slot][h, r, :] = s
                c = jnp.max(s, axis=0, keepdims=True)
                mx = c if mx is None else jnp.maximum(mx, c)
        if a is not None:
            mx_ref[a_slot, h] = mx
        if bc is not None:
            acc_ref[base + h] = alpha * acc_ref[base + h] + _dot(vt_fn(h), p_ref[h])
            m_ref[base + h] = m_next


def _reset_state(m_ref, acc_ref):
    m_ref[...] = jnp.full(m_ref.shape, NEG_BIG, F32)
    acc_ref[...] = jnp.zeros(acc_ref.shape, F32)


def _pair_output(acc_lo, acc_hi):
    o_lo = acc_lo[0:HEAD_DIM, :] / acc_lo[HEAD_DIM:HEAD_DIM + 1, :]
    o_hi = acc_hi[0:HEAD_DIM, :] / acc_hi[HEAD_DIM:HEAD_DIM + 1, :]
    return jnp.transpose(jnp.concatenate([o_lo, o_hi], axis=0))


def _top_k_rows(val, n_f, k, sentinel):
    chosen = jnp.zeros(val.shape, F32)
    for _ in range(k):
        mx = jnp.max(val, axis=0, keepdims=True)
        idx = jnp.min(jnp.where(val == mx, n_f, sentinel), axis=0, keepdims=True)
        pick = n_f == idx
        chosen = jnp.where(pick, 1.0, chosen)
        val = jnp.where(pick, -jnp.inf, val)
    return chosen


def _moba_kernel(qt_ref, k_ref, vt_ref, km_ref, bt_ref, o_ref,
                 v0_ref, v1_ref, mb_ref, qs_ref, m_ref, acc_ref, s0_ref, s1_ref, mx_ref, p_ref):
    cur = pl.program_id(2)
    nb = km_ref.shape[0]
    sk = s0_ref.shape[1]
    per_block = KT // STRIP

    @pl.when(cur == 0)
    def _():
        _fill_v_with_ones(vt_ref, v0_ref, v1_ref)

    km = km_ref[...].astype(BF16)
    n_f = lax.broadcasted_iota(jnp.int32, (nb, TQ), 0).astype(F32)
    past = n_f < cur.astype(F32)
    v_refs = (v0_ref, v1_ref)
    for hh, qh in enumerate(_split_heads_t(qt_ref[...])):
        gate = jnp.where(past, _dot(km, qh), -jnp.inf)
        chosen = _top_k_rows(gate, n_f, min(MOBA_TOPK, nb), float(nb))
        mb_ref[hh] = jnp.where((chosen > 0.5) & past, 0.0, NEG_BIG)
        qs_ref[hh] = qh * jnp.asarray(HEAD_DIM ** -0.5, BF16)
    _reset_state(m_ref, acc_ref)

    def job_start(u):
        return pl.multiple_of(jnp.maximum(cur - 2 * u - 1, 0) * KT, KT)

    def score_fn(u):
        ks = job_start(u)
        return lambda hh: _dot(k_ref[pl.ds(ks, sk), :], qs_ref[hh])

    def vt_fn(u):
        ks = job_start(u)
        return lambda hh: v_refs[hh][:, pl.ds(ks, sk)]

    has_prev = cur >= 1
    kind_lo = jnp.where(has_prev, BT_PREV, BT_OWN)
    kind_hi = jnp.where(has_prev, BT_OWN, BT_MASKED)
    keep = jnp.where(has_prev, 1.0, 0.0)
    prev_row = [mb_ref[hh, pl.ds(jnp.maximum(cur - 1, 0), 1), :] * keep for hh in range(2)]

    def near_bias(hh, i):
        half, part = divmod(i, per_block)
        r = slice(part * STRIP, (part + 1) * STRIP)
        if half == 0:
            return bt_ref[hh, kind_lo, r, :] + prev_row[hh]
        return bt_ref[hh, kind_hi, r, :]

    def far_bias(u):
        b_lo = cur - 2 * u - 1
        idx = jnp.maximum(b_lo, 0)
        pen_lo = jnp.where(b_lo >= -1, 0.0, NEG_BIG)
        pen_hi = jnp.where(b_lo >= 0, 0.0, NEG_BIG)
        rows = [(mb_ref[hh, pl.ds(idx, 1), :] + pen_lo, mb_ref[hh, pl.ds(idx + 1, 1), :] + pen_hi)
                for hh in range(2)]
        return lambda hh, i: rows[hh][i // per_block]

    pipe = functools.partial(_pipe, (s0_ref, s1_ref), mx_ref, p_ref, m_ref, acc_ref, 2)
    pipe(a=(0, score_fn(0), near_bias))
    n_far = lax.shift_right_logical(cur, 1)

    def two_jobs(t, carry):
        u0 = 2 * t + 1
        pipe(a=(1, score_fn(u0), far_bias(u0)), bc=(0, 0, vt_fn(u0 - 1)))
        pipe(a=(0, score_fn(u0 + 1), far_bias(u0 + 1)), bc=(1, 0, vt_fn(u0)))
        return carry

    n_iter = lax.shift_right_logical(n_far + 1, 1)
    lax.fori_loop(0, n_iter, two_jobs, 0)
    pipe(bc=(0, 0, vt_fn(2 * n_iter)))
    o_ref[...] = _pair_output(acc_ref[0], acc_ref[1]).astype(o_ref.dtype)


def _moba_attention(tok, fm, kmean, bias_tiles):
    b, s, _ = tok.shape
    nb = s // MOBA_BLOCK
    npair = MOBA_HEADS // 2
    sk = 2 * KT
    return pl.pallas_call(
        _moba_kernel,
        grid=(b, npair, s // TQ),
        in_specs=[
            pl.BlockSpec((None, LANES, TQ), lambda i, p, c: (i, ROW_MQ // LANES + p, c)),
            pl.BlockSpec((None, s, LANES), lambda i, p, c: (i, 0, COL_MK // LANES + p)),
            pl.BlockSpec((None, LANES, s), lambda i, p, c: (i, ROW_MV // LANES + p, 0)),
            pl.BlockSpec((None, nb, LANES), lambda i, p, c: (i, 0, p)),
            pl.BlockSpec((2, N_BIAS_TILES, KT, TQ), lambda i, p, c: (p, 0, 0, 0)),
        ],
        out_specs=pl.BlockSpec((None, TQ, LANES), lambda i, p, c: (i, c, p)),
        out_shape=jax.ShapeDtypeStruct((b, s, MOBA_W), BF16),
        scratch_shapes=[
            pltpu.VMEM((VROWS, s), BF16), pltpu.VMEM((VROWS, s), BF16),
            pltpu.VMEM((2, nb, TQ), F32),
            pltpu.VMEM((2, LANES, TQ), BF16),
            pltpu.VMEM((2, 1, TQ), F32),
            pltpu.VMEM((2, VROWS, TQ), F32),
            pltpu.VMEM((2, sk, TQ), F32), pltpu.VMEM((2, sk, TQ), F32),
            pltpu.VMEM((2, 2, 1, TQ), F32),
            pltpu.VMEM((2, sk, TQ), BF16),
        ],
        compiler_params=_params(3),
        name="moba_attention",
    )(fm, tok, fm, kmean, bias_tiles)


def _compress_kernel(ck_ref, cv_ref, pk_ref, pv_ref, w1k_ref, w1v_ref, w2k_ref, w2v_ref,
                     kc_ref, vct_ref):
    nch = ck_ref.shape[1]
    half = CMP_STRIDE * HEAD_DIM

    def run(c_ref, pos_ref, w1_ref, w2_ref):
        acc = jnp.zeros((nch, LANES), F32)
        for g in range(NSA_GROUPS):
            c = c_ref[g].astype(F32)
            lo = (c + pos_ref[0:1, :]).astype(BF16)
            hi = (c + pos_ref[1:2, :]).astype(BF16)
            pre = _dot(lo, w1_ref[:half, :]) + pltpu.roll(_dot(hi, w1_ref[half:, :]), nch - 1, axis=0)
            acc = acc + _dot(jax.nn.gelu(pre).astype(BF16), w2_ref[g])
        return acc

    kc_ref[...] = run(ck_ref, pk_ref, w1k_ref, w2k_ref).astype(kc_ref.dtype)
    vct_ref[...] = jnp.transpose(run(cv_ref, pv_ref, w1v_ref, w2v_ref)).astype(vct_ref.dtype)


def _compress(ck, cv, pos_k, pos_v, w1k, w1v, w2k, w2v):
    b, g, nch, width = ck.shape
    full = lambda shape: pl.BlockSpec(shape, lambda i: (0,) * len(shape))
    return pl.pallas_call(
        _compress_kernel,
        grid=(b,),
        in_specs=[pl.BlockSpec((None, g, nch, width), lambda i: (i, 0, 0, 0)),
                  pl.BlockSpec((None, g, nch, width), lambda i: (i, 0, 0, 0)),
                  full(pos_k.shape), full(pos_v.shape), full(w1k.shape), full(w1v.shape),
                  full(w2k.shape), full(w2v.shape)],
        out_specs=[pl.BlockSpec((None, nch, LANES), lambda i: (i, 0, 0)),
                   pl.BlockSpec((None, LANES, nch), lambda i: (i, 0, 0))],
        out_shape=[jax.ShapeDtypeStruct((b, nch, LANES), BF16),
                   jax.ShapeDtypeStruct((b, LANES, nch), BF16)],
        compiler_params=_params(1),
        name="nsa_compress",
    )(ck, cv, pos_k, pos_v, w1k, w1v, w2k, w2v)


def _cmp_select_kernel(qt_ref, kc_ref, vct_ref, mimp_ref, o_ref, sel_ref, *, nsb):
    ci = pl.program_id(1)
    nch = kc_ref.shape[0]
    nsbp = mimp_ref.shape[0]
    qs = ci * TQ
    n_row = lax.broadcasted_iota(jnp.int32, (nch, TQ), 0)
    t_col = qs + lax.broadcasted_iota(jnp.int32, (nch, TQ), 1)
    visible = (n_row * CMP_STRIDE + (CMP_BLOCK - 1)) <= t_col
    any_visible = t_col[0:1, :] >= (CMP_BLOCK - 1)
    kc = kc_ref[...]
    vct = vct_ref[...]
    vrow = lax.broadcasted_iota(jnp.int32, vct.shape, 0)
    mimp = mimp_ref[...]
    scale = jnp.asarray(HEAD_DIM ** -0.5, BF16)

    j_row = lax.broadcasted_iota(jnp.int32, (nsbp, TQ), 0)
    t_sel = qs + lax.broadcasted_iota(jnp.int32, (nsbp, TQ), 1)
    cur = lax.shift_right_logical(t_sel, int(math.log2(SLC_BLOCK)))
    j_f = j_row.astype(F32)
    forced = (j_row == 0) | (j_row == cur) | (j_row == cur - 1)
    allowed = (j_row <= cur) & (j_row < nsb)

    q_heads = [_split_heads_t(qt_ref[r * LANES:(r + 1) * LANES, :] * scale) for r in range(NSA_HPG)]
    out_t = [jnp.zeros((LANES, TQ), F32) for _ in range(NSA_HPG)]
    for g in range(NSA_GROUPS):
        vin_g = (vrow < HEAD_DIM) if g == 0 else (vrow >= HEAD_DIM)
        vct_g = jnp.where(vin_g, vct, jnp.zeros_like(vct))
        p_grp = jnp.zeros((nch, TQ), F32)
        for r in range(NSA_HPG):
            s = jnp.where(visible, _dot(kc, q_heads[r][g]), NEG_BIG)
            e = jnp.exp(s - jnp.max(s, axis=0, keepdims=True))
            p = e / jnp.sum(e, axis=0, keepdims=True)
            p = jnp.where(any_visible, p, 0.0)
            p_grp = p_grp + p
            out_t[r] = out_t[r] + _dot(vct_g, p.astype(BF16))
        p_hi = p_grp.astype(BF16)
        p_lo = (p_grp - p_hi.astype(F32)).astype(BF16)
        imp = _dot(mimp, p_hi) + _dot(mimp, p_lo)
        val = jnp.where(forced & allowed, jnp.inf, jnp.where(allowed, imp, -jnp.inf))
        chosen = _top_k_rows(val, j_f, min(SLC_TOPN, nsb), float(nsbp))
        sel_ref[g] = jnp.where(allowed & (chosen > 0.5), 0.0, NEG_BIG)
    for r in range(NSA_HPG):
        o_ref[:, r * LANES:(r + 1) * LANES] = jnp.transpose(out_t[r]).astype(o_ref.dtype)


def _cmp_select(fm, kc, vct, mimp_t, nsb):
    b, _, s = fm.shape
    nch = kc.shape[1]
    nsbp = mimp_t.shape[0]
    return pl.pallas_call(
        functools.partial(_cmp_select_kernel, nsb=nsb),
        grid=(b, s // TQ),
        in_specs=[pl.BlockSpec((None, NSA_W, TQ), lambda i, c: (i, ROW_NQ // NSA_W, c)),
                  pl.BlockSpec((None, nch, LANES), lambda i, c: (i, 0, 0)),
                  pl.BlockSpec((None, LANES, nch), lambda i, c: (i, 0, 0)),
                  pl.BlockSpec((nsbp, nch), lambda i, c: (0, 0))],
        out_specs=[pl.BlockSpec((None, TQ, NSA_W), lambda i, c: (i, c, 0)),
                   pl.BlockSpec((None, NSA_GROUPS, nsbp, TQ), lambda i, c: (i, 0, 0, c))],
        out_shape=[jax.ShapeDtypeStruct((b, s, NSA_W), BF16),
                   jax.ShapeDtypeStruct((b, NSA_GROUPS, nsbp, s), F32)],
        compiler_params=_params(2),
        name="nsa_cmp_select",
    )(fm, kc, vct, mimp_t)


def _slc_win_kernel(qt_ref, ks_ref, vst_ref, kw_ref, vwt_ref, sel_ref, bt_ref,
                    oslc_ref, owin_ref,
                    vs0_ref, vs1_ref, vw0_ref, vw1_ref, qs_ref, m_ref, acc_ref,
                    s0_ref, s1_ref, mx_ref, p_ref):
    cur = pl.program_id(1)

    @pl.when(cur == 0)
    def _():
        _fill_v_with_ones(vst_ref, vs0_ref, vs1_ref)
        _fill_v_with_ones(vwt_ref, vw0_ref, vw1_ref)

    scale = jnp.asarray(HEAD_DIM ** -0.5, BF16)
    nh = NSA_HEADS
    per_tile = KT // SLC_BLOCK
    per_strip = STRIP // SLC_BLOCK
    for r in range(NSA_HPG):
        lo, hi = _split_heads_t(qt_ref[r * LANES:(r + 1) * LANES, :] * scale)
        qs_ref[r] = lo
        qs_ref[NSA_HPG + r] = hi
    _reset_state(m_ref, acc_ref)
    vs_refs = (vs0_ref, vs1_ref)
    vw_refs = (vw0_ref, vw1_ref)
    sel_base, win_base = 0, nh

    def tile_start(n):
        return pl.multiple_of(jnp.maximum(n, 0) * KT, KT)

    def score_fn(k_ref, n):
        ks = tile_start(n)
        return lambda h: _dot(k_ref[pl.ds(ks, KT), :], qs_ref[h])

    def vt_fn(v_refs, n):
        ks = tile_start(n)
        return lambda h: v_refs[h // NSA_HPG][:, pl.ds(ks, KT)]

    def sel_rows(n, penalty):
        base = jnp.maximum(n, 0) * per_tile
        out = []
        for g in range(NSA_GROUPS):
            rows = [sel_ref[g, pl.ds(base + b, 1), :] + penalty for b in range(per_tile)]
            out.append([jnp.concatenate([jnp.broadcast_to(rows[i * per_strip + b], (SLC_BLOCK, TQ))
                                         for b in range(per_strip)], axis=0)
                        for i in range(KT // STRIP)])
        return out

    def near_bias(kind, rows=None):
        def f(h, i):
            t = bt_ref[h, kind, i * STRIP:(i + 1) * STRIP, :]
            return t if rows is None else t + rows[h // NSA_HPG][i]
        return f

    def far_bias(n):
        rows = sel_rows(n, jnp.where(n >= 0, 0.0, NEG_BIG))
        return lambda h, i: rows[h // NSA_HPG][i]

    pipe = functools.partial(_pipe, (s0_ref, s1_ref), mx_ref, p_ref, m_ref, acc_ref, nh)
    kind_prev = jnp.where(cur >= 1, BT_PREV, BT_MASKED)
    kind_upper = jnp.where(cur >= 2, BT_UPPER, BT_MASKED)
    pipe(a=(0, score_fn(kw_ref, cur), near_bias(BT_OWN)))
    pipe(a=(1, score_fn(kw_ref, cur - 1), near_bias(kind_prev)), bc=(0, win_base, vt_fn(vw_refs, cur)))
    pipe(a=(0, score_fn(kw_ref, cur - 2), near_bias(kind_upper)), bc=(1, win_base, vt_fn(vw_refs, cur - 1)))
    pipe(a=(1, score_fn(ks_ref, cur), near_bias(BT_OWN, sel_rows(cur, 0.0))),
         bc=(0, win_base, vt_fn(vw_refs, cur - 2)))
    pipe(a=(0, score_fn(ks_ref, cur - 1), near_bias(kind_prev, sel_rows(cur - 1, 0.0))),
         bc=(1, sel_base, vt_fn(vs_refs, cur)))

    def two_tiles(t, carry):
        n0 = cur - 2 - 2 * t
        pipe(a=(1, score_fn(ks_ref, n0), far_bias(n0)), bc=(0, sel_base, vt_fn(vs_refs, n0 + 1)))
        pipe(a=(0, score_fn(ks_ref, n0 - 1), far_bias(n0 - 1)), bc=(1, sel_base, vt_fn(vs_refs, n0)))
        return carry

    n_iter = lax.shift_right_logical(jnp.maximum(cur - 1, 0) + 1, 1)
    lax.fori_loop(0, n_iter, two_tiles, 0)
    pipe(bc=(0, sel_base, vt_fn(vs_refs, cur - 1 - 2 * n_iter)))

    for r in range(NSA_HPG):
        cols = slice(r * LANES, (r + 1) * LANES)
        oslc_ref[:, cols] = _pair_output(acc_ref[r], acc_ref[NSA_HPG + r]).astype(oslc_ref.dtype)
        owin_ref[:, cols] = _pair_output(acc_ref[nh + r], acc_ref[nh + NSA_HPG + r]).astype(owin_ref.dtype)


def _slc_win(tok, fm, sel, bias_tiles):
    b, s, _ = tok.shape
    nsbp = sel.shape[2]
    return pl.pallas_call(
        _slc_win_kernel,
        grid=(b, s // TQ),
        in_specs=[pl.BlockSpec((None, NSA_W, TQ), lambda i, c: (i, ROW_NQ // NSA_W, c)),
                  pl.BlockSpec((None, s, LANES), lambda i, c: (i, 0, COL_NKS // LANES)),
                  pl.BlockSpec((None, LANES, s), lambda i, c: (i, ROW_NVS // LANES, 0)),
                  pl.BlockSpec((None, s, LANES), lambda i, c: (i, 0, COL_NKW // LANES)),
                  pl.BlockSpec((None, LANES, s), lambda i, c: (i, ROW_NVW // LANES, 0)),
                  pl.BlockSpec((None, NSA_GROUPS, nsbp, TQ), lambda i, c: (i, 0, 0, c)),
                  pl.BlockSpec((NSA_HEADS, N_BIAS_TILES, KT, TQ), lambda i, c: (1, 0, 0, 0))],
        out_specs=[pl.BlockSpec((None, TQ, NSA_W), lambda i, c: (i, c, 0)),
                   pl.BlockSpec((None, TQ, NSA_W), lambda i, c: (i, c, 0))],
        out_shape=[jax.ShapeDtypeStruct((b, s, NSA_W), BF16),
                   jax.ShapeDtypeStruct((b, s, NSA_W), BF16)],
        scratch_shapes=[pltpu.VMEM((VROWS, s), BF16)] * 4 + [
            pltpu.VMEM((NSA_HEADS, LANES, TQ), BF16),
            pltpu.VMEM((2 * NSA_HEADS, 1, TQ), F32),
            pltpu.VMEM((2 * NSA_HEADS, VROWS, TQ), F32),
            pltpu.VMEM((NSA_HEADS, KT, TQ), F32), pltpu.VMEM((NSA_HEADS, KT, TQ), F32),
            pltpu.VMEM((2, NSA_HEADS, 1, TQ), F32),
            pltpu.VMEM((NSA_HEADS, KT, TQ), BF16)],
        compiler_params=_params(2),
        name="nsa_slc_win",
    )(fm, tok, fm, tok, fm, sel, bias_tiles)


def _merge_kernel(x_ref, oa_ref, oc_ref, os_ref, ow_ref, ng_ref, ga_ref, gb_ref,
                  eg_ref, wa_ref, wb_ref, wo_ref, o_ref):
    sig = jax.nn.sigmoid(ng_ref[...].astype(F32))
    s_hi = sig.astype(BF16)
    s_lo = (sig - s_hi.astype(F32)).astype(BF16)
    eg = eg_ref[...]
    gexp = _dot(s_hi, eg) + _dot(s_lo, eg)
    o_b = (gexp[:, :NSA_W] * oc_ref[...].astype(F32)
           + gexp[:, NSA_W:2 * NSA_W] * os_ref[...].astype(F32)
           + gexp[:, 2 * NSA_W:] * ow_ref[...].astype(F32))
    a = _dot(oa_ref[...], wa_ref[...])
    bb = _dot(o_b.astype(BF16), wb_ref[...])
    merged = (jax.nn.sigmoid(ga_ref[...].astype(F32)) * a
              + jax.nn.sigmoid(gb_ref[...].astype(F32)) * bb)
    o_ref[...] = x_ref[...] + _dot(merged.astype(BF16), wo_ref[...])


def _merge(x2d, tok2d, o_a, o_c, o_s, o_w, egate, wa, wb, wo, tm):
    m, d = x2d.shape
    row = lambda w, col=0: pl.BlockSpec((tm, w), lambda i, col=col: (i, col))
    full = lambda a: pl.BlockSpec(a.shape, lambda i: (0, 0))
    return pl.pallas_call(
        _merge_kernel,
        grid=(m // tm,),
        in_specs=[row(d), row(MOBA_W), row(NSA_W), row(NSA_W), row(NSA_W),
                  row(GATE_PAD, COL_NGATE // GATE_PAD),
                  row(D_MODEL, COL_GA // D_MODEL), row(D_MODEL, COL_GB // D_MODEL),
                  full(egate), full(wa), full(wb), full(wo)],
        out_specs=row(d),
        out_shape=jax.ShapeDtypeStruct((m, d), F32),
        compiler_params=_params(1),
        name="merge_out",
    )(x2d, o_a, o_c, o_s, o_w, tok2d, tok2d, tok2d, egate, wa, wb, wo)


def _xattn_kernel(x_ref, g_ref, wq_ref, kv_ref, wo_ref, o_ref):
    x = x_ref[...]
    q = _dot(_rms(x, g_ref[...]).astype(BF16), wq_ref[...]).astype(BF16)
    heads = []
    for h in range(XATTN_HEADS):
        cols = slice(h * XATTN_HEAD_DIM, (h + 1) * XATTN_HEAD_DIM)
        k = kv_ref[:, cols]
        v = kv_ref[:, XATTN_W + h * XATTN_HEAD_DIM:XATTN_W + (h + 1) * XATTN_HEAD_DIM]
        s = _dot_nt(q[:, cols], k) * (XATTN_HEAD_DIM ** -0.5)
        e = jnp.exp(s - jnp.max(s, axis=1, keepdims=True))
        p = e / jnp.sum(e, axis=1, keepdims=True)
        heads.append(_dot(p.astype(BF16), v).astype(BF16))
    o_ref[...] = x + _dot(jnp.concatenate(heads, axis=1), wo_ref[...])


def _xattn(x3d, g, wq, memkv, wo, tm):
    b, s, d = x3d.shape
    mem_len = memkv.shape[1]
    return pl.pallas_call(
        _xattn_kernel,
        grid=(b, s // tm),
        in_specs=[pl.BlockSpec((None, tm, d), lambda i, j: (i, j, 0)),
                  pl.BlockSpec((1, d), lambda i, j: (0, 0)),
                  pl.BlockSpec(wq.shape, lambda i, j: (0, 0)),
                  pl.BlockSpec((None, mem_len, 2 * XATTN_W), lambda i, j: (i, 0, 0)),
                  pl.BlockSpec(wo.shape, lambda i, j: (0, 0))],
        out_specs=pl.BlockSpec((None, tm, d), lambda i, j: (i, j, 0)),
        out_shape=jax.ShapeDtypeStruct((b, s, d), F32),
        compiler_params=_params(2),
        name="xattn",
    )(x3d, g.reshape(1, d), wq, memkv, wo)


def _ffn_kernel(x_ref, halo_ref, g_ref, wg_ref, wu_ref, cw_ref, cb_ref, wd_ref, gf_ref, o_ref,
                acc_ref, *, chunk, halo_rows):
    j = pl.program_id(1)
    x = x_ref[...]
    tm = x.shape[0]
    hf = _rms(x, g_ref[...]).astype(BF16)
    keep = jnp.where(j > 0, 1.0, 0.0)
    hh = _rms(halo_ref[...], g_ref[...]).astype(BF16)
    row = lax.broadcasted_iota(jnp.int32, (tm, chunk), 0)
    acc_ref[...] = jnp.zeros_like(acc_ref)
    for c0 in range(0, D_FF, chunk):
        cols = slice(c0, c0 + chunk)
        u = _dot(hf, wg_ref[:, cols])
        uh = _dot(hh, wg_ref[:, cols]) * keep
        prev1 = uh[halo_rows - 1:halo_rows, :]
        prev2 = uh[halo_rows - 2:halo_rows - 1, :]
        u1 = jnp.where(row == 0, prev1, pltpu.roll(u, 1, axis=0))
        u2 = jnp.where(row == 0, prev2, jnp.where(row == 1, prev1, pltpu.roll(u, 2, axis=0)))
        a = cw_ref[0:1, cols] * u2 + cw_ref[1:2, cols] * u1 + cw_ref[2:3, cols] * u + cb_ref[:, cols]
        y = jax.nn.gelu(a) * _dot(hf, wu_ref[:, cols])
        acc_ref[...] += _dot(y.astype(BF16), wd_ref[cols, :])
    o_ref[...] = _rms(x + acc_ref[...], gf_ref[...])


def _ffn(x3d, g, wg, wu, cw, cb, wd, gf, tm):
    b, s, d = x3d.shape
    halo_rows = 8
    per = tm // halo_rows
    return pl.pallas_call(
        functools.partial(_ffn_kernel, chunk=256, halo_rows=halo_rows),
        grid=(b, s // tm),
        in_specs=[pl.BlockSpec((None, tm, d), lambda i, j: (i, j, 0)),
                  pl.BlockSpec((None, halo_rows, d), lambda i, j: (i, jnp.maximum(j * per - 1, 0), 0)),
                  pl.BlockSpec((1, d), lambda i, j: (0, 0)),
                  pl.BlockSpec(wg.shape, lambda i, j: (0, 0)),
                  pl.BlockSpec(wu.shape, lambda i, j: (0, 0)),
                  pl.BlockSpec(cw.shape, lambda i, j: (0, 0)),
                  pl.BlockSpec((1, D_FF), lambda i, j: (0, 0)),
                  pl.BlockSpec(wd.shape, lambda i, j: (0, 0)),
                  pl.BlockSpec((1, d), lambda i, j: (0, 0))],
        out_specs=pl.BlockSpec((None, tm, d), lambda i, j: (i, j, 0)),
        out_shape=jax.ShapeDtypeStruct((b, s, d), F32),
        scratch_shapes=[pltpu.VMEM((tm, d), F32)],
        compiler_params=_params(2),
        name="conv_ffn",
    )(x3d, x3d, g.reshape(1, d), wg, wu, cw, cb.reshape(1, D_FF), wd, gf.reshape(1, d))


def _nq_perm():
    idx = []
    for r in range(NSA_HPG):
        for g in range(NSA_GROUPS):
            h = g * NSA_HPG + r
            idx.extend(range(h * HEAD_DIM, (h + 1) * HEAD_DIM))
    return np.asarray(idx, np.int32)


def _gate_expand():
    e = np.zeros((GATE_PAD, 3 * NSA_W), np.float32)
    for br in range(3):
        for g in range(NSA_GROUPS):
            for r in range(NSA_HPG):
                c0 = br * NSA_W + r * LANES + g * HEAD_DIM
                e[br * NSA_HEADS + g * NSA_HPG + r, c0:c0 + HEAD_DIM] = 1.0
    return e


def _importance_matrix(nch, nsb, nsbp):
    ratio = SLC_BLOCK // CMP_STRIDE
    nc = nch - 1
    m = np.zeros((nsbp, nch), np.float32)
    for j in range(nsb):
        for a in range(ratio):
            for bb in range(CMP_BLOCK // CMP_STRIDE):
                c = j * ratio + a - bb
                if 0 <= c < nc:
                    m[j, c] += 1.0
    return m


def _split_w_in(w_in):
    widths = [MOBA_W] * 3 + [NSA_W] + [NSA_KV_W] * 6 + [3 * NSA_HEADS, D_MODEL, D_MODEL]
    cuts = np.cumsum(widths)[:-1]
    mq, mk, mv, nq, nkc, nvc, nks, nvs, nkw, nvw, ngate, ga, gb = jnp.split(w_in, cuts, axis=1)
    ngate = jnp.pad(ngate, ((0, 0), (0, GATE_PAD - 3 * NSA_HEADS)))
    w_tok = jnp.concatenate([ga, gb, mk, nkc, nvc, nks, nkw, ngate], axis=1)
    w_fm = jnp.concatenate([mq, mv, nq[:, _nq_perm()], nvs, nvw], axis=1)
    return w_tok.astype(BF16), jnp.transpose(w_fm).astype(BF16)


def _layer(x, mem, bias_tiles, norm_mix_g, w_in, cmp_pos_k, cmp_w1_k, cmp_w2_k, cmp_pos_v,
           cmp_w1_v, cmp_w2_v, w_branch_a, w_branch_b, w_out, norm_xattn_g, norm_mem_g, w_xq, w_xkv, w_xo,
           norm_ffn_g, w_gate, w_up, conv_w, conv_b, w_down, norm_out_g):
    b, s, d = x.shape
    assert s % 1024 == 0 and d == D_MODEL
    rows = b * s
    nch = s // CMP_STRIDE
    nsb = s // SLC_BLOCK
    nsbp = max(LANES, nsb)

    w_tok, w_fm_t = _split_w_in(w_in)
    tok, fm = _in_proj(x, norm_mix_g, w_tok, w_fm_t, tm=512)
    tok2d = tok.reshape(rows, TOK_W)

    kmean = _moba_kmean(tok)
    o_a = _moba_attention(tok, fm, kmean, bias_tiles)

    def chunks(col):
        t = tok[:, :, col:col + NSA_KV_W].reshape(b, s, NSA_GROUPS, HEAD_DIM)
        return t.transpose(0, 2, 1, 3).reshape(b, NSA_GROUPS, nch, CMP_STRIDE * HEAD_DIM)

    def w2_pad(w2):
        z = jnp.zeros_like(w2)
        return jnp.stack([jnp.concatenate([w2, z], axis=1), jnp.concatenate([z, w2], axis=1)]).astype(BF16)

    kc, vct = _compress(chunks(COL_NKC), chunks(COL_NVC),
                        cmp_pos_k.reshape(2, -1), cmp_pos_v.reshape(2, -1),
                        cmp_w1_k.astype(BF16), cmp_w1_v.astype(BF16), w2_pad(cmp_w2_k), w2_pad(cmp_w2_v))
    mimp_t = jnp.asarray(_importance_matrix(nch, nsb, nsbp), BF16)
    o_c, sel = _cmp_select(fm, kc, vct, mimp_t, nsb)
    o_s, o_w = _slc_win(tok, fm, sel, bias_tiles)

    x1 = _merge(x.reshape(rows, d), tok2d, o_a.reshape(rows, MOBA_W), o_c.reshape(rows, NSA_W),
                o_s.reshape(rows, NSA_W), o_w.reshape(rows, NSA_W), jnp.asarray(_gate_expand(), BF16),
                w_branch_a.astype(BF16), w_branch_b[_nq_perm(), :].astype(BF16), w_out.astype(BF16), tm=512)

    mem_len = mem.shape[1]
    memkv = _norm_matmul(mem.reshape(b * mem_len, d), norm_mem_g, w_xkv.astype(BF16), tm=mem_len,
                         name="mem_kv").reshape(b, mem_len, 2 * XATTN_W)
    x2 = _xattn(x1.reshape(b, s, d), norm_xattn_g, w_xq.astype(BF16), memkv, w_xo.astype(BF16), tm=512)

    return _ffn(x2, norm_ffn_g, w_gate.astype(BF16), w_up.astype(BF16), conv_w, conv_b,
                w_down.astype(BF16), norm_out_g, tm=512)


def kernel(x, mem, rel_bias, norm_mix_g, w_in, cmp_pos_k, cmp_w1_k, cmp_w2_k, cmp_pos_v, cmp_w1_v, cmp_w2_v, w_branch_a, w_branch_b, w_out, norm_xattn_g, norm_mem_g, w_xq, w_xkv, w_xo, norm_ffn_g, w_gate, w_up, conv_w, conv_b, w_down, norm_final_g):
    depth = w_in.shape[0]
    assert depth == 1, "the final norm is fused into the last layer's FFN kernel"
    bias_tiles = _bias_tiles(rel_bias)
    l = 0
    return _layer(x, mem, bias_tiles, norm_mix_g[l], w_in[l], cmp_pos_k[l], cmp_w1_k[l],
                  cmp_w2_k[l], cmp_pos_v[l], cmp_w1_v[l], cmp_w2_v[l], w_branch_a[l], w_branch_b[l], w_out[l],
                  norm_xattn_g[l], norm_mem_g[l], w_xq[l], w_xkv[l], w_xo[l],
                  norm_ffn_g[l], w_gate[l], w_up[l], conv_w[l], conv_b[l], w_down[l], norm_final_g)
```

```python
import functools
import math

import numpy as np
import jax
import jax.numpy as jnp
from jax import lax
from jax.experimental import pallas as pl
from jax.experimental.pallas import tpu as pltpu

F32 = jnp.float32
BF16 = jnp.bfloat16

D_MODEL = 1024
HEAD_DIM = 64
MOBA_HEADS = 8
MOBA_BLOCK = 256
MOBA_TOPK = 3
NSA_HEADS = 8
NSA_GROUPS = 2
NSA_HPG = NSA_HEADS // NSA_GROUPS
CMP_BLOCK = 32
CMP_STRIDE = 16
CMP_HIDDEN = 256
SLC_BLOCK = 64
SLC_TOPN = 16
WINDOW = 512
REL_BUCKETS = 32
REL_MAX_DIST = 128
XATTN_HEADS = 4
XATTN_HEAD_DIM = 128
D_FF = 2816
CONV_WIDTH = 3
Q_BLOCK = 128
RMS_EPS = 1e-6
NEG_BIG = -1e30
LOG2E = math.log2(math.e)

LANES = 128
TQ = 256
KT = 256
STRIP = 128
ONES_ROWS = 16
VROWS = HEAD_DIM + ONES_ROWS
VMEM_LIMIT = 56 * 1024 * 1024

MOBA_W = MOBA_HEADS * HEAD_DIM
NSA_W = NSA_HEADS * HEAD_DIM
NSA_KV_W = NSA_GROUPS * HEAD_DIM
XATTN_W = XATTN_HEADS * XATTN_HEAD_DIM
GATE_PAD = LANES

COL_GA = 0
COL_GB = COL_GA + D_MODEL
COL_MK = COL_GB + D_MODEL
COL_NKC = COL_MK + MOBA_W
COL_NVC = COL_NKC + NSA_KV_W
COL_NKS = COL_NVC + NSA_KV_W
COL_NKW = COL_NKS + NSA_KV_W
COL_NGATE = COL_NKW + NSA_KV_W
TOK_W = COL_NGATE + GATE_PAD
ROW_MQ = 0
ROW_MV = ROW_MQ + MOBA_W
ROW_NQ = ROW_MV + MOBA_W
ROW_NVS = ROW_NQ + NSA_W
ROW_NVW = ROW_NVS + NSA_KV_W
FM_W = ROW_NVW + NSA_KV_W

BT_OWN, BT_PREV, BT_UPPER, BT_MASKED = 0, 1, 2, 3
N_BIAS_TILES = 4


def _bucket_thresholds():
    n = np.arange(0, 4 * REL_MAX_DIST)
    exact = REL_BUCKETS // 2
    nf = np.maximum(n, 1).astype(np.float64)
    large = exact + (np.log(nf / exact) / math.log(REL_MAX_DIST / exact) * (REL_BUCKETS - exact)).astype(np.int64)
    large = np.minimum(large, REL_BUCKETS - 1)
    b = np.where(n < exact, n, large)
    return [int(np.argmax(b >= k)) for k in range(REL_BUCKETS)]


_THR = _bucket_thresholds()
assert _THR[REL_BUCKETS - 1] <= LANES


def _dot(a, b):
    return jnp.dot(a, b, preferred_element_type=F32)


def _dot_nt(a, b):
    return lax.dot_general(a, b, (((1,), (1,)), ((), ())), preferred_element_type=F32)


def _rms(x, g):
    return x * lax.rsqrt(jnp.mean(x * x, axis=-1, keepdims=True) + RMS_EPS) * g


def _params(n_axes):
    return pltpu.CompilerParams(dimension_semantics=("arbitrary",) * n_axes,
                                vmem_limit_bytes=VMEM_LIMIT)


def _bias_kernel(rb_ref, out_ref):
    h = pl.program_id(0)
    j = lax.broadcasted_iota(jnp.int32, (KT, TQ), 0)
    i = lax.broadcasted_iota(jnp.int32, (KT, TQ), 1)
    far = rb_ref[REL_BUCKETS - 1, h]

    def table(d):
        t = jnp.full((KT, TQ), (rb_ref[0, h] - far) * LOG2E, F32)
        for k in range(1, REL_BUCKETS):
            t = jnp.where(d >= _THR[k], (rb_ref[k, h] - far) * LOG2E, t)
        return t

    neg = jnp.full((KT, TQ), NEG_BIG, F32)
    out_ref[0, BT_OWN] = jnp.where(i >= j, table(i - j), neg)
    out_ref[0, BT_PREV] = table(i - j + KT)
    out_ref[0, BT_UPPER] = jnp.where(j > i, 0.0, neg)
    out_ref[0, BT_MASKED] = neg


def _bias_tiles(rel_bias):
    nh = rel_bias.shape[1]
    return pl.pallas_call(
        _bias_kernel,
        grid=(nh,),
        in_specs=[pl.BlockSpec(memory_space=pltpu.SMEM)],
        out_specs=pl.BlockSpec((1, N_BIAS_TILES, KT, TQ), lambda h: (h, 0, 0, 0)),
        out_shape=jax.ShapeDtypeStruct((nh, N_BIAS_TILES, KT, TQ), F32),
        compiler_params=_params(1),
        name="bias_tiles",
    )(rel_bias)


def _inproj_kernel(x_ref, g_ref, wt_ref, wf_ref, tok_ref, fm_ref, *, chunk):
    h = _rms(x_ref[...], g_ref[...]).astype(BF16)
    n = wt_ref.shape[1]
    for c0 in range(0, n, chunk):
        c1 = min(c0 + chunk, n)
        tok_ref[:, c0:c1] = _dot(h, wt_ref[:, c0:c1]).astype(tok_ref.dtype)
    n = wf_ref.shape[0]
    for c0 in range(0, n, chunk):
        c1 = min(c0 + chunk, n)
        fm_ref[c0:c1, :] = _dot_nt(wf_ref[c0:c1, :], h).astype(fm_ref.dtype)


def _in_proj(x, g, w_tok, w_fm_t, tm):
    b, s, d = x.shape
    return pl.pallas_call(
        functools.partial(_inproj_kernel, chunk=512),
        grid=(b, s // tm),
        in_specs=[pl.BlockSpec((None, tm, d), lambda i, j: (i, j, 0)),
                  pl.BlockSpec((1, d), lambda i, j: (0, 0)),
                  pl.BlockSpec(w_tok.shape, lambda i, j: (0, 0)),
                  pl.BlockSpec(w_fm_t.shape, lambda i, j: (0, 0))],
        out_specs=[pl.BlockSpec((None, tm, TOK_W), lambda i, j: (i, j, 0)),
                   pl.BlockSpec((None, FM_W, tm), lambda i, j: (i, 0, j))],
        out_shape=[jax.ShapeDtypeStruct((b, s, TOK_W), BF16),
                   jax.ShapeDtypeStruct((b, FM_W, s), BF16)],
        compiler_params=_params(2),
        name="in_proj",
    )(x, g.reshape(1, d), w_tok, w_fm_t)


def _norm_matmul_kernel(x_ref, g_ref, w_ref, o_ref):
    o_ref[...] = _dot(_rms(x_ref[...], g_ref[...]).astype(BF16), w_ref[...]).astype(o_ref.dtype)


def _norm_matmul(x2d, g, w, tm, name):
    m, d = x2d.shape
    n = w.shape[1]
    return pl.pallas_call(
        _norm_matmul_kernel,
        grid=(m // tm,),
        in_specs=[pl.BlockSpec((tm, d), lambda i: (i, 0)),
                  pl.BlockSpec((1, d), lambda i: (0, 0)),
                  pl.BlockSpec((d, n), lambda i: (0, 0))],
        out_specs=pl.BlockSpec((tm, n), lambda i: (i, 0)),
        out_shape=jax.ShapeDtypeStruct((m, n), BF16),
        compiler_params=_params(1),
        name=name,
    )(x2d, g.reshape(1, d), w)


def _kmean_kernel(k_ref, o_ref):
    nb = o_ref.shape[0]
    for n in range(nb):
        blk = k_ref[n * MOBA_BLOCK:(n + 1) * MOBA_BLOCK, :].astype(F32)
        o_ref[n:n + 1, :] = jnp.mean(blk, axis=0, keepdims=True)


def _moba_kmean(tok):
    b, s, _ = tok.shape
    nb = s // MOBA_BLOCK
    return pl.pallas_call(
        _kmean_kernel,
        grid=(b,),
        in_specs=[pl.BlockSpec((None, s, MOBA_W), lambda i: (i, 0, COL_MK // MOBA_W))],
        out_specs=pl.BlockSpec((None, nb, MOBA_W), lambda i: (i, 0, 0)),
        out_shape=jax.ShapeDtypeStruct((b, nb, MOBA_W), F32),
        compiler_params=_params(1),
        name="moba_kmean",
    )(tok)


def _fill_v_with_ones(vt_ref, lo_ref, hi_ref, cols=1024):
    s = vt_ref.shape[1]
    ones = jnp.ones((ONES_ROWS, cols), BF16)

    def body(c, carry):
        c0 = pl.multiple_of(c * cols, cols)
        lo_ref[0:HEAD_DIM, pl.ds(c0, cols)] = vt_ref[0:HEAD_DIM, pl.ds(c0, cols)]
        lo_ref[HEAD_DIM:VROWS, pl.ds(c0, cols)] = ones
        hi_ref[0:HEAD_DIM, pl.ds(c0, cols)] = vt_ref[HEAD_DIM:2 * HEAD_DIM, pl.ds(c0, cols)]
        hi_ref[HEAD_DIM:VROWS, pl.ds(c0, cols)] = ones
        return carry

    lax.fori_loop(0, s // cols, body, 0)


def _split_heads_t(q_t):
    row = lax.broadcasted_iota(jnp.int32, q_t.shape, 0)
    zero = jnp.zeros_like(q_t)
    return jnp.where(row < HEAD_DIM, q_t, zero), jnp.where(row >= HEAD_DIM, q_t, zero)


def _fill_k_with_block_ids(k_ref, lo_ref, hi_ref, block, nblock, rows=512):
    s = k_ref.shape[0]
    lane = lax.broadcasted_iota(jnp.int32, (rows, LANES), 1)
    shift = int(math.log2(block))

    def body(c, carry):
        r0 = pl.multiple_of(c * rows, rows)
        key = r0 + lax.broadcasted_iota(jnp.int32, (rows, LANES), 0)
        bid = lax.shift_right_logical(key, shift) & (nblock - 1)
        k = k_ref[pl.ds(r0, rows), :]
        lo_ref[pl.ds(r0, rows), :] = jnp.where(lane < HEAD_DIM, k, (lane - HEAD_DIM == bid).astype(BF16))
        hi_ref[pl.ds(r0, rows), :] = jnp.where(lane >= HEAD_DIM, k, (lane == bid).astype(BF16))
        return carry

    lax.fori_loop(0, s // rows, body, 0)


def _q_with_mask_rows(q_rows, mask_rows, low_head):
    r = lax.broadcasted_iota(jnp.int32, (ONES_ROWS, TQ), 0)
    m = jnp.zeros((ONES_ROWS, TQ), F32)
    for b, row in enumerate(mask_rows):
        m = jnp.where(r == b, row, m)
    pad = jnp.zeros((HEAD_DIM - ONES_ROWS, TQ), BF16)
    parts = [q_rows, m.astype(BF16), pad] if low_head else [m.astype(BF16), pad, q_rows]
    return jnp.concatenate(parts, axis=0)


def _pipe(bufs, m_ref, acc_ref, nheads, a=None, b=None, c=None, c_last=False):
    s_refs, p_refs, mx_ref, al_ref = bufs
    rows = s_refs[0].shape[1]

    def stage_c(h):
        c_slot, c_base, vt_fn = c
        acc_ref[c_base + h] = (al_ref[c_slot, h] * acc_ref[c_base + h]
                               + _dot(vt_fn(h), p_refs[c_slot][h]))

    for h in range(nheads):
        if c is not None and not c_last:
            stage_c(h)
        if a is not None:
            a_slot, score_fn, bias_fn = a
            raw = score_fn(h)
            mx = None
        if b is not None:
            b_slot, b_base = b
            m_prev = m_ref[b_base + h]
            m_next = jnp.maximum(m_prev, mx_ref[b_slot, h])
            al_ref[b_slot, h] = jnp.exp2(m_prev - m_next)
            m_ref[b_base + h] = m_next
        for i in range(rows // STRIP):
            r = slice(i * STRIP, (i + 1) * STRIP)
            if b is not None:
                p_refs[b_slot][h, r, :] = jnp.exp2(s_refs[b_slot][h, r, :] - m_next).astype(BF16)
            if a is not None:
                s = raw[r, :] if bias_fn is None else raw[r, :] + bias_fn(h, i)
                s_refs[a_slot][h, r, :] = s
                cmax = jnp.max(s, axis=0, keepdims=True)
                mx = cmax if mx is None else jnp.maximum(mx, cmax)
        if a is not None:
            mx_ref[a_slot, h] = mx
        if c is not None and c_last:
            stage_c(h)


def _reset_state(bufs, m_ref, acc_ref):
    _, p_refs, _, al_ref = bufs
    m_ref[...] = jnp.full(m_ref.shape, NEG_BIG, F32)
    acc_ref[...] = jnp.zeros(acc_ref.shape, F32)
    p_refs[1][...] = jnp.zeros(p_refs[1].shape, BF16)
    al_ref[...] = jnp.ones(al_ref.shape, F32)


def _pair_output(acc_lo, acc_hi):
    o_lo = acc_lo[0:HEAD_DIM, :] / acc_lo[HEAD_DIM:HEAD_DIM + 1, :]
    o_hi = acc_hi[0:HEAD_DIM, :] / acc_hi[HEAD_DIM:HEAD_DIM + 1, :]
    return jnp.transpose(jnp.concatenate([o_lo, o_hi], axis=0))


def _top_k_rows(val, n_f, k, sentinel):
    chosen = jnp.zeros(val.shape, F32)
    for _ in range(k):
        mx = jnp.max(val, axis=0, keepdims=True)
        idx = jnp.min(jnp.where(val == mx, n_f, sentinel), axis=0, keepdims=True)
        pick = n_f == idx
        chosen = jnp.where(pick, 1.0, chosen)
        val = jnp.where(pick, -jnp.inf, val)
    return chosen


def _moba_kernel(qt_ref, k_ref, vt_ref, km_ref, bt_ref, o_ref,
                 k0_ref, k1_ref, v0_ref, v1_ref, mb_ref, qs_ref, m_ref, acc_ref,
                 s0_ref, s1_ref, p0_ref, p1_ref, mx_ref, al_ref):
    cur = pl.program_id(2)
    bufs = ((s0_ref, s1_ref), (p0_ref, p1_ref), mx_ref, al_ref)
    nb = km_ref.shape[0]
    sk = s0_ref.shape[1]
    per_block = KT // STRIP

    @pl.when(cur == 0)
    def _():
        _fill_v_with_ones(vt_ref, v0_ref, v1_ref)
        _fill_k_with_block_ids(k_ref, k0_ref, k1_ref, MOBA_BLOCK, 2)

    km = km_ref[...].astype(BF16)
    n_f = lax.broadcasted_iota(jnp.int32, (nb, TQ), 0).astype(F32)
    past = n_f < cur.astype(F32)
    k_refs = (k0_ref, k1_ref)
    v_refs = (v0_ref, v1_ref)
    for hh, qh in enumerate(_split_heads_t(qt_ref[...])):
        gate = jnp.where(past, _dot(km, qh), -jnp.inf)
        chosen = _top_k_rows(gate, n_f, min(MOBA_TOPK, nb), float(nb))
        mb_ref[hh] = jnp.where((chosen > 0.5) & past, 0.0, NEG_BIG)
        qs_ref[hh] = qh
    _reset_state(bufs, m_ref, acc_ref)

    def first_block(u):
        return jnp.maximum(cur - 2 * u - 1, 0)

    def score_fn(u, first_rows, second_rows):
        idx = first_block(u)
        ks = pl.multiple_of(idx * KT, KT)
        first_even = (idx & 1) == 0

        def f(hh):
            even = jnp.where(first_even, first_rows[hh], second_rows[hh])
            odd = jnp.where(first_even, second_rows[hh], first_rows[hh])
            q_rows = qs_ref[hh, 0:HEAD_DIM, :] if hh == 0 else qs_ref[hh, HEAD_DIM:2 * HEAD_DIM, :]
            return _dot(k_refs[hh][pl.ds(ks, sk), :], _q_with_mask_rows(q_rows, (even, odd), hh == 0))
        return f

    def vt_fn(u):
        ks = pl.multiple_of(first_block(jnp.maximum(u, 0)) * KT, KT)
        return lambda hh: v_refs[hh][:, pl.ds(ks, sk)]

    has_prev = cur >= 1
    kind_lo = jnp.where(has_prev, BT_PREV, BT_OWN)
    kind_hi = jnp.where(has_prev, BT_OWN, BT_MASKED)
    keep = jnp.where(has_prev, 1.0, 0.0)
    prev_rows = [mb_ref[hh, pl.ds(jnp.maximum(cur - 1, 0), 1), :] * keep for hh in range(2)]
    no_rows = [jnp.zeros((1, TQ), F32)] * 2

    def near_bias(hh, i):
        half, part = divmod(i, per_block)
        return bt_ref[hh, kind_lo if half == 0 else kind_hi, part * STRIP:(part + 1) * STRIP, :]

    def far_scores(u):
        b_lo = cur - 2 * u - 1
        idx = jnp.maximum(b_lo, 0)
        pen_lo = jnp.where(b_lo >= -1, 0.0, NEG_BIG)
        pen_hi = jnp.where(b_lo >= 0, 0.0, NEG_BIG)
        return score_fn(u, [mb_ref[hh, pl.ds(idx, 1), :] + pen_lo for hh in range(2)],
                        [mb_ref[hh, pl.ds(idx + 1, 1), :] + pen_hi for hh in range(2)])

    pipe = functools.partial(_pipe, bufs, m_ref, acc_ref, 2)
    pipe(a=(0, score_fn(0, prev_rows, no_rows), near_bias))
    n_far = lax.shift_right_logical(cur, 1)

    def two_jobs(t, carry):
        u0 = 2 * t + 1
        pipe(a=(1, far_scores(u0), None), b=(0, 0), c=(1, 0, vt_fn(u0 - 2)))
        pipe(a=(0, far_scores(u0 + 1), None), b=(1, 0), c=(0, 0, vt_fn(u0 - 1)))
        return carry

    n_iter = lax.shift_right_logical(n_far + 1, 1)
    lax.fori_loop(0, n_iter, two_jobs, 0)
    pipe(b=(0, 0), c=(1, 0, vt_fn(2 * n_iter - 1)))
    pipe(c=(0, 0, vt_fn(2 * n_iter)))
    o_ref[...] = _pair_output(acc_ref[0], acc_ref[1]).astype(o_ref.dtype)


def _moba_attention(tok, fm, kmean, bias_tiles):
    b, s, _ = tok.shape
    nb = s // MOBA_BLOCK
    npair = MOBA_HEADS // 2
    sk = 2 * KT
    return pl.pallas_call(
        _moba_kernel,
        grid=(b, npair, s // TQ),
        in_specs=[
            pl.BlockSpec((None, LANES, TQ), lambda i, p, c: (i, ROW_MQ // LANES + p, c)),
            pl.BlockSpec((None, s, LANES), lambda i, p, c: (i, 0, COL_MK // LANES + p)),
            pl.BlockSpec((None, LANES, s), lambda i, p, c: (i, ROW_MV // LANES + p, 0)),
            pl.BlockSpec((None, nb, LANES), lambda i, p, c: (i, 0, p)),
            pl.BlockSpec((2, N_BIAS_TILES, KT, TQ), lambda i, p, c: (p, 0, 0, 0)),
        ],
        out_specs=pl.BlockSpec((None, TQ, LANES), lambda i, p, c: (i, c, p)),
        out_shape=jax.ShapeDtypeStruct((b, s, MOBA_W), BF16),
        scratch_shapes=[
            pltpu.VMEM((s, LANES), BF16), pltpu.VMEM((s, LANES), BF16),
            pltpu.VMEM((VROWS, s), BF16), pltpu.VMEM((VROWS, s), BF16),
            pltpu.VMEM((2, nb, TQ), F32),
            pltpu.VMEM((2, LANES, TQ), BF16),
            pltpu.VMEM((2, 1, TQ), F32),
            pltpu.VMEM((2, VROWS, TQ), F32),
            pltpu.VMEM((2, sk, TQ), F32), pltpu.VMEM((2, sk, TQ), F32),
            pltpu.VMEM((2, sk, TQ), BF16), pltpu.VMEM((2, sk, TQ), BF16),
            pltpu.VMEM((2, 2, 1, TQ), F32), pltpu.VMEM((2, 2, 1, TQ), F32),
        ],
        compiler_params=_params(3),
        name="moba_attention",
    )(fm, tok, fm, kmean, bias_tiles)


def _compress_kernel(ck_ref, cv_ref, pk_ref, pv_ref, w1k_ref, w1v_ref, w2k_ref, w2v_ref,
                     kc_ref, vct_ref):
    nch = ck_ref.shape[1]
    half = CMP_STRIDE * HEAD_DIM

    def run(c_ref, pos_ref, w1_ref, w2_ref):
        acc = jnp.zeros((nch, LANES), F32)
        for g in range(NSA_GROUPS):
            c = c_ref[g].astype(F32)
            lo = (c + pos_ref[0:1, :]).astype(BF16)
            hi = (c + pos_ref[1:2, :]).astype(BF16)
            pre = _dot(lo, w1_ref[:half, :]) + pltpu.roll(_dot(hi, w1_ref[half:, :]), nch - 1, axis=0)
            acc = acc + _dot(jax.nn.gelu(pre).astype(BF16), w2_ref[g])
        return acc

    kc_ref[...] = run(ck_ref, pk_ref, w1k_ref, w2k_ref).astype(kc_ref.dtype)
    vct_ref[...] = jnp.transpose(run(cv_ref, pv_ref, w1v_ref, w2v_ref)).astype(vct_ref.dtype)


def _compress(ck, cv, pos_k, pos_v, w1k, w1v, w2k, w2v):
    b, g, nch, width = ck.shape
    full = lambda shape: pl.BlockSpec(shape, lambda i: (0,) * len(shape))
    return pl.pallas_call(
        _compress_kernel,
        grid=(b,),
        in_specs=[pl.BlockSpec((None, g, nch, width), lambda i: (i, 0, 0, 0)),
                  pl.BlockSpec((None, g, nch, width), lambda i: (i, 0, 0, 0)),
                  full(pos_k.shape), full(pos_v.shape), full(w1k.shape), full(w1v.shape),
                  full(w2k.shape), full(w2v.shape)],
        out_specs=[pl.BlockSpec((None, nch, LANES), lambda i: (i, 0, 0)),
                   pl.BlockSpec((None, LANES, nch), lambda i: (i, 0, 0))],
        out_shape=[jax.ShapeDtypeStruct((b, nch, LANES), BF16),
                   jax.ShapeDtypeStruct((b, LANES, nch), BF16)],
        compiler_params=_params(1),
        name="nsa_compress",
    )(ck, cv, pos_k, pos_v, w1k, w1v, w2k, w2v)


def _cmp_select_kernel(qt_ref, kc_ref, vct_ref, mimp_ref, o_ref, sel_ref, *, nsb):
    ci = pl.program_id(1)
    nch = kc_ref.shape[0]
    nsbp = mimp_ref.shape[0]
    qs = ci * TQ
    n_row = lax.broadcasted_iota(jnp.int32, (nch, TQ), 0)
    t_col = qs + lax.broadcasted_iota(jnp.int32, (nch, TQ), 1)
    visible = (n_row * CMP_STRIDE + (CMP_BLOCK - 1)) <= t_col
    any_visible = t_col[0:1, :] >= (CMP_BLOCK - 1)
    kc = kc_ref[...]
    vct = vct_ref[...]
    vrow = lax.broadcasted_iota(jnp.int32, vct.shape, 0)
    mimp = mimp_ref[...]

    j_row = lax.broadcasted_iota(jnp.int32, (nsbp, TQ), 0)
    t_sel = qs + lax.broadcasted_iota(jnp.int32, (nsbp, TQ), 1)
    cur = lax.shift_right_logical(t_sel, int(math.log2(SLC_BLOCK)))
    j_f = j_row.astype(F32)
    forced = (j_row == 0) | (j_row == cur) | (j_row == cur - 1)
    allowed = (j_row <= cur) & (j_row < nsb)

    q_heads = [_split_heads_t(qt_ref[r * LANES:(r + 1) * LANES, :]) for r in range(NSA_HPG)]
    out_t = [jnp.zeros((LANES, TQ), F32) for _ in range(NSA_HPG)]
    for g in range(NSA_GROUPS):
        vin_g = (vrow < HEAD_DIM) if g == 0 else (vrow >= HEAD_DIM)
        vct_g = jnp.where(vin_g, vct, jnp.zeros_like(vct))
        p_grp = jnp.zeros((nch, TQ), F32)
        for r in range(NSA_HPG):
            s = jnp.where(visible, _dot(kc, q_heads[r][g]), NEG_BIG)
            e = jnp.exp2(s - jnp.max(s, axis=0, keepdims=True))
            p = e / jnp.sum(e, axis=0, keepdims=True)
            p = jnp.where(any_visible, p, 0.0)
            p_grp = p_grp + p
            out_t[r] = out_t[r] + _dot(vct_g, p.astype(BF16))
        p_hi = p_grp.astype(BF16)
        p_lo = (p_grp - p_hi.astype(F32)).astype(BF16)
        imp = _dot(mimp, p_hi) + _dot(mimp, p_lo)
        val = jnp.where(forced & allowed, jnp.inf, jnp.where(allowed, imp, -jnp.inf))
        chosen = _top_k_rows(val, j_f, min(SLC_TOPN, nsb), float(nsbp))
        sel_ref[g] = jnp.where(allowed & (chosen > 0.5), 0.0, NEG_BIG)
    for r in range(NSA_HPG):
        o_ref[:, r * LANES:(r + 1) * LANES] = jnp.transpose(out_t[r]).astype(o_ref.dtype)


def _cmp_select(fm, kc, vct, mimp_t, nsb):
    b, _, s = fm.shape
    nch = kc.shape[1]
    nsbp = mimp_t.shape[0]
    return pl.pallas_call(
        functools.partial(_cmp_select_kernel, nsb=nsb),
        grid=(b, s // TQ),
        in_specs=[pl.BlockSpec((None, NSA_W, TQ), lambda i, c: (i, ROW_NQ // NSA_W, c)),
                  pl.BlockSpec((None, nch, LANES), lambda i, c: (i, 0, 0)),
                  pl.BlockSpec((None, LANES, nch), lambda i, c: (i, 0, 0)),
                  pl.BlockSpec((nsbp, nch), lambda i, c: (0, 0))],
        out_specs=[pl.BlockSpec((None, TQ, NSA_W), lambda i, c: (i, c, 0)),
                   pl.BlockSpec((None, NSA_GROUPS, nsbp, TQ), lambda i, c: (i, 0, 0, c))],
        out_shape=[jax.ShapeDtypeStruct((b, s, NSA_W), BF16),
                   jax.ShapeDtypeStruct((b, NSA_GROUPS, nsbp, s), F32)],
        compiler_params=_params(2),
        name="nsa_cmp_select",
    )(fm, kc, vct, mimp_t)


def _slc_win_kernel(qt_ref, ks_ref, vst_ref, kw_ref, vwt_ref, sel_ref, bt_ref,
                    oslc_ref, owin_ref,
                    ks0_ref, ks1_ref, vs0_ref, vs1_ref, vw0_ref, vw1_ref, qs_ref, m_ref, acc_ref,
                    s0_ref, s1_ref, p0_ref, p1_ref, mx_ref, al_ref):
    cur = pl.program_id(1)
    bufs = ((s0_ref, s1_ref), (p0_ref, p1_ref), mx_ref, al_ref)
    per_tile = KT // SLC_BLOCK

    @pl.when(cur == 0)
    def _():
        _fill_v_with_ones(vst_ref, vs0_ref, vs1_ref)
        _fill_v_with_ones(vwt_ref, vw0_ref, vw1_ref)
        _fill_k_with_block_ids(ks_ref, ks0_ref, ks1_ref, SLC_BLOCK, per_tile)

    nh = NSA_HEADS
    for r in range(NSA_HPG):
        lo, hi = _split_heads_t(qt_ref[r * LANES:(r + 1) * LANES, :])
        qs_ref[r] = lo
        qs_ref[NSA_HPG + r] = hi
    _reset_state(bufs, m_ref, acc_ref)
    ks_refs = (ks0_ref, ks1_ref)
    vs_refs = (vs0_ref, vs1_ref)
    vw_refs = (vw0_ref, vw1_ref)
    sel_base, win_base = 0, nh

    def tile_start(n):
        return pl.multiple_of(jnp.maximum(n, 0) * KT, KT)

    def win_scores(n):
        ks = tile_start(n)
        return lambda h: _dot(kw_ref[pl.ds(ks, KT), :], qs_ref[h])

    def sel_scores(n, penalty):
        ks = tile_start(n)
        base = jnp.maximum(n, 0) * per_tile
        rows = [[sel_ref[g, pl.ds(base + b, 1), :] + penalty for b in range(per_tile)]
                for g in range(NSA_GROUPS)]

        def f(h):
            g = h // NSA_HPG
            q_rows = qs_ref[h, 0:HEAD_DIM, :] if g == 0 else qs_ref[h, HEAD_DIM:2 * HEAD_DIM, :]
            return _dot(ks_refs[g][pl.ds(ks, KT), :], _q_with_mask_rows(q_rows, rows[g], g == 0))
        return f

    def vt_fn(v_refs, n):
        ks = tile_start(n)
        return lambda h: v_refs[h // NSA_HPG][:, pl.ds(ks, KT)]

    def near_bias(kind):
        return lambda h, i: bt_ref[h, kind, i * STRIP:(i + 1) * STRIP, :]

    pipe = functools.partial(_pipe, bufs, m_ref, acc_ref, nh, c_last=True)
    kind_prev = jnp.where(cur >= 1, BT_PREV, BT_MASKED)
    kind_upper = jnp.where(cur >= 2, BT_UPPER, BT_MASKED)
    pipe(a=(0, win_scores(cur), near_bias(BT_OWN)))
    pipe(a=(1, win_scores(cur - 1), near_bias(kind_prev)), b=(0, win_base))
    pipe(a=(0, win_scores(cur - 2), near_bias(kind_upper)), b=(1, win_base),
         c=(0, win_base, vt_fn(vw_refs, cur)))
    pipe(a=(1, sel_scores(cur, 0.0), near_bias(BT_OWN)), b=(0, win_base),
         c=(1, win_base, vt_fn(vw_refs, cur - 1)))
    pipe(a=(0, sel_scores(cur - 1, 0.0), near_bias(kind_prev)), b=(1, sel_base),
         c=(0, win_base, vt_fn(vw_refs, cur - 2)))

    def far_scores(n):
        return sel_scores(n, jnp.where(n >= 0, 0.0, NEG_BIG))

    def two_tiles(t, carry):
        n0 = cur - 2 - 2 * t
        pipe(a=(1, far_scores(n0), None), b=(0, sel_base), c=(1, sel_base, vt_fn(vs_refs, n0 + 2)))
        pipe(a=(0, far_scores(n0 - 1), None), b=(1, sel_base), c=(0, sel_base, vt_fn(vs_refs, n0 + 1)))
        return carry

    n_iter = lax.shift_right_logical(jnp.maximum(cur - 1, 0) + 1, 1)
    lax.fori_loop(0, n_iter, two_tiles, 0)
    pipe(b=(0, sel_base), c=(1, sel_base, vt_fn(vs_refs, cur - 2 * n_iter)))
    pipe(c=(0, sel_base, vt_fn(vs_refs, cur - 1 - 2 * n_iter)))

    for r in range(NSA_HPG):
        cols = slice(r * LANES, (r + 1) * LANES)
        oslc_ref[:, cols] = _pair_output(acc_ref[r], acc_ref[NSA_HPG + r]).astype(oslc_ref.dtype)
        owin_ref[:, cols] = _pair_output(acc_ref[nh + r], acc_ref[nh + NSA_HPG + r]).astype(owin_ref.dtype)


def _slc_win(tok, fm, sel, bias_tiles):
    b, s, _ = tok.shape
    nsbp = sel.shape[2]
    return pl.pallas_call(
        _slc_win_kernel,
        grid=(b, s // TQ),
        in_specs=[pl.BlockSpec((None, NSA_W, TQ), lambda i, c: (i, ROW_NQ // NSA_W, c)),
                  pl.BlockSpec((None, s, LANES), lambda i, c: (i, 0, COL_NKS // LANES)),
                  pl.BlockSpec((None, LANES, s), lambda i, c: (i, ROW_NVS // LANES, 0)),
                  pl.BlockSpec((None, s, LANES), lambda i, c: (i, 0, COL_NKW // LANES)),
                  pl.BlockSpec((None, LANES, s), lambda i, c: (i, ROW_NVW // LANES, 0)),
                  pl.BlockSpec((None, NSA_GROUPS, nsbp, TQ), lambda i, c: (i, 0, 0, c)),
                  pl.BlockSpec((NSA_HEADS, N_BIAS_TILES, KT, TQ), lambda i, c: (1, 0, 0, 0))],
        out_specs=[pl.BlockSpec((None, TQ, NSA_W), lambda i, c: (i, c, 0)),
                   pl.BlockSpec((None, TQ, NSA_W), lambda i, c: (i, c, 0))],
        out_shape=[jax.ShapeDtypeStruct((b, s, NSA_W), BF16),
                   jax.ShapeDtypeStruct((b, s, NSA_W), BF16)],
        scratch_shapes=[pltpu.VMEM((s, LANES), BF16)] * 2 + [pltpu.VMEM((VROWS, s), BF16)] * 4 + [
            pltpu.VMEM((NSA_HEADS, LANES, TQ), BF16),
            pltpu.VMEM((2 * NSA_HEADS, 1, TQ), F32),
            pltpu.VMEM((2 * NSA_HEADS, VROWS, TQ), F32),
            pltpu.VMEM((NSA_HEADS, KT, TQ), F32), pltpu.VMEM((NSA_HEADS, KT, TQ), F32),
            pltpu.VMEM((NSA_HEADS, KT, TQ), BF16), pltpu.VMEM((NSA_HEADS, KT, TQ), BF16),
            pltpu.VMEM((2, NSA_HEADS, 1, TQ), F32), pltpu.VMEM((2, NSA_HEADS, 1, TQ), F32)],
        compiler_params=_params(2),
        name="nsa_slc_win",
    )(fm, tok, fm, tok, fm, sel, bias_tiles)


def _merge_kernel(x_ref, oa_ref, oc_ref, os_ref, ow_ref, ng_ref, ga_ref, gb_ref,
                  eg_ref, wa_ref, wb_ref, wo_ref, o_ref):
    sig = jax.nn.sigmoid(ng_ref[...].astype(F32))
    s_hi = sig.astype(BF16)
    s_lo = (sig - s_hi.astype(F32)).astype(BF16)
    eg = eg_ref[...]
    gexp = _dot(s_hi, eg) + _dot(s_lo, eg)
    o_b = (gexp[:, :NSA_W] * oc_ref[...].astype(F32)
           + gexp[:, NSA_W:2 * NSA_W] * os_ref[...].astype(F32)
           + gexp[:, 2 * NSA_W:] * ow_ref[...].astype(F32))
    a = _dot(oa_ref[...], wa_ref[...])
    bb = _dot(o_b.astype(BF16), wb_ref[...])
    merged = (jax.nn.sigmoid(ga_ref[...].astype(F32)) * a
              + jax.nn.sigmoid(gb_ref[...].astype(F32)) * bb)
    o_ref[...] = x_ref[...] + _dot(merged.astype(BF16), wo_ref[...])


def _merge(x2d, tok2d, o_a, o_c, o_s, o_w, egate, wa, wb, wo, tm):
    m, d = x2d.shape
    row = lambda w, col=0: pl.BlockSpec((tm, w), lambda i, col=col: (i, col))
    full = lambda a: pl.BlockSpec(a.shape, lambda i: (0, 0))
    return pl.pallas_call(
        _merge_kernel,
        grid=(m // tm,),
        in_specs=[row(d), row(MOBA_W), row(NSA_W), row(NSA_W), row(NSA_W),
                  row(GATE_PAD, COL_NGATE // GATE_PAD),
                  row(D_MODEL, COL_GA // D_MODEL), row(D_MODEL, COL_GB // D_MODEL),
                  full(egate), full(wa), full(wb), full(wo)],
        out_specs=row(d),
        out_shape=jax.ShapeDtypeStruct((m, d), F32),
        compiler_params=_params(1),
        name="merge_out",
    )(x2d, o_a, o_c, o_s, o_w, tok2d, tok2d, tok2d, egate, wa, wb, wo)


def _xattn_kernel(x_ref, g_ref, wq_ref, kv_ref, wo_ref, o_ref):
    x = x_ref[...]
    q = _dot(_rms(x, g_ref[...]).astype(BF16), wq_ref[...]).astype(BF16)
    heads = []
    for h in range(XATTN_HEADS):
        cols = slice(h * XATTN_HEAD_DIM, (h + 1) * XATTN_HEAD_DIM)
        k = kv_ref[:, cols]
        v = kv_ref[:, XATTN_W + h * XATTN_HEAD_DIM:XATTN_W + (h + 1) * XATTN_HEAD_DIM]
        s = _dot_nt(q[:, cols], k) * (XATTN_HEAD_DIM ** -0.5)
        e = jnp.exp(s - jnp.max(s, axis=1, keepdims=True))
        p = e / jnp.sum(e, axis=1, keepdims=True)
        heads.append(_dot(p.astype(BF16), v).astype(BF16))
    o_ref[...] = x + _dot(jnp.concatenate(heads, axis=1), wo_ref[...])


def _xattn(x3d, g, wq, memkv, wo, tm):
    b, s, d = x3d.shape
    mem_len = memkv.shape[1]
    return pl.pallas_call(
        _xattn_kernel,
        grid=(b, s // tm),
        in_specs=[pl.BlockSpec((None, tm, d), lambda i, j: (i, j, 0)),
                  pl.BlockSpec((1, d), lambda i, j: (0, 0)),
                  pl.BlockSpec(wq.shape, lambda i, j: (0, 0)),
                  pl.BlockSpec((None, mem_len, 2 * XATTN_W), lambda i, j: (i, 0, 0)),
                  pl.BlockSpec(wo.shape, lambda i, j: (0, 0))],
        out_specs=pl.BlockSpec((None, tm, d), lambda i, j: (i, j, 0)),
        out_shape=jax.ShapeDtypeStruct((b, s, d), F32),
        compiler_params=_params(2),
        name="xattn",
    )(x3d, g.reshape(1, d), wq, memkv, wo)


def _ffn_kernel(x_ref, halo_ref, g_ref, wg_ref, wu_ref, cw_ref, cb_ref, wd_ref, gf_ref, o_ref,
                acc_ref, *, chunk, halo_rows):
    j = pl.program_id(1)
    x = x_ref[...]
    tm = x.shape[0]
    hf = _rms(x, g_ref[...]).astype(BF16)
    keep = jnp.where(j > 0, 1.0, 0.0)
    hh = _rms(halo_ref[...], g_ref[...]).astype(BF16)
    row = lax.broadcasted_iota(jnp.int32, (tm, chunk), 0)
    acc_ref[...] = jnp.zeros_like(acc_ref)
    for c0 in range(0, D_FF, chunk):
        cols = slice(c0, c0 + chunk)
        u = _dot(hf, wg_ref[:, cols])
        uh = _dot(hh, wg_ref[:, cols]) * keep
        prev1 = uh[halo_rows - 1:halo_rows, :]
        prev2 = uh[halo_rows - 2:halo_rows - 1, :]
        u1 = jnp.where(row == 0, prev1, pltpu.roll(u, 1, axis=0))
        u2 = jnp.where(row == 0, prev2, jnp.where(row == 1, prev1, pltpu.roll(u, 2, axis=0)))
        a = cw_ref[0:1, cols] * u2 + cw_ref[1:2, cols] * u1 + cw_ref[2:3, cols] * u + cb_ref[:, cols]
        y = jax.nn.gelu(a) * _dot(hf, wu_ref[:, cols])
        acc_ref[...] += _dot(y.astype(BF16), wd_ref[cols, :])
    o_ref[...] = _rms(x + acc_ref[...], gf_ref[...])


def _ffn(x3d, g, wg, wu, cw, cb, wd, gf, tm):
    b, s, d = x3d.shape
    halo_rows = 8
    per = tm // halo_rows
    return pl.pallas_call(
        functools.partial(_ffn_kernel, chunk=256, halo_rows=halo_rows),
        grid=(b, s // tm),
        in_specs=[pl.BlockSpec((None, tm, d), lambda i, j: (i, j, 0)),
                  pl.BlockSpec((None, halo_rows, d), lambda i, j: (i, jnp.maximum(j * per - 1, 0), 0)),
                  pl.BlockSpec((1, d), lambda i, j: (0, 0)),
                  pl.BlockSpec(wg.shape, lambda i, j: (0, 0)),
                  pl.BlockSpec(wu.shape, lambda i, j: (0, 0)),
                  pl.BlockSpec(cw.shape, lambda i, j: (0, 0)),
                  pl.BlockSpec((1, D_FF), lambda i, j: (0, 0)),
                  pl.BlockSpec(wd.shape, lambda i, j: (0, 0)),
                  pl.BlockSpec((1, d), lambda i, j: (0, 0))],
        out_specs=pl.BlockSpec((None, tm, d), lambda i, j: (i, j, 0)),
        out_shape=jax.ShapeDtypeStruct((b, s, d), F32),
        scratch_shapes=[pltpu.VMEM((tm, d), F32)],
        compiler_params=_params(2),
        name="conv_ffn",
    )(x3d, x3d, g.reshape(1, d), wg, wu, cw, cb.reshape(1, D_FF), wd, gf.reshape(1, d))


def _nq_perm():
    idx = []
    for r in range(NSA_HPG):
        for g in range(NSA_GROUPS):
            h = g * NSA_HPG + r
            idx.extend(range(h * HEAD_DIM, (h + 1) * HEAD_DIM))
    return np.asarray(idx, np.int32)


def _gate_expand():
    e = np.zeros((GATE_PAD, 3 * NSA_W), np.float32)
    for br in range(3):
        for g in range(NSA_GROUPS):
            for r in range(NSA_HPG):
                c0 = br * NSA_W + r * LANES + g * HEAD_DIM
                e[br * NSA_HEADS + g * NSA_HPG + r, c0:c0 + HEAD_DIM] = 1.0
    return e


def _importance_matrix(nch, nsb, nsbp):
    ratio = SLC_BLOCK // CMP_STRIDE
    nc = nch - 1
    m = np.zeros((nsbp, nch), np.float32)
    for j in range(nsb):
        for a in range(ratio):
            for bb in range(CMP_BLOCK // CMP_STRIDE):
                c = j * ratio + a - bb
                if 0 <= c < nc:
                    m[j, c] += 1.0
    return m


def _split_w_in(w_in):
    widths = [MOBA_W] * 3 + [NSA_W] + [NSA_KV_W] * 6 + [3 * NSA_HEADS, D_MODEL, D_MODEL]
    cuts = np.cumsum(widths)[:-1]
    mq, mk, mv, nq, nkc, nvc, nks, nvs, nkw, nvw, ngate, ga, gb = jnp.split(w_in, cuts, axis=1)
    ngate = jnp.pad(ngate, ((0, 0), (0, GATE_PAD - 3 * NSA_HEADS)))
    q_scale = HEAD_DIM ** -0.5 * LOG2E
    mq, nq = mq * q_scale, nq * q_scale
    w_tok = jnp.concatenate([ga, gb, mk, nkc, nvc, nks, nkw, ngate], axis=1)
    w_fm = jnp.concatenate([mq, mv, nq[:, _nq_perm()], nvs, nvw], axis=1)
    return w_tok.astype(BF16), jnp.transpose(w_fm).astype(BF16)


def _layer(x, mem, bias_tiles, norm_mix_g, w_in, cmp_pos_k, cmp_w1_k, cmp_w2_k, cmp_pos_v,
           cmp_w1_v, cmp_w2_v, w_branch_a, w_branch_b, w_out, norm_xattn_g, norm_mem_g, w_xq, w_xkv, w_xo,
           norm_ffn_g, w_gate, w_up, conv_w, conv_b, w_down, norm_out_g):
    b, s, d = x.shape
    assert s % 1024 == 0 and d == D_MODEL
    rows = b * s
    nch = s // CMP_STRIDE
    nsb = s // SLC_BLOCK
    nsbp = max(LANES, nsb)

    w_tok, w_fm_t = _split_w_in(w_in)
    tok, fm = _in_proj(x, norm_mix_g, w_tok, w_fm_t, tm=512)
    tok2d = tok.reshape(rows, TOK_W)

    kmean = _moba_kmean(tok)
    o_a = _moba_attention(tok, fm, kmean, bias_tiles)

    def chunks(col):
        t = tok[:, :, col:col + NSA_KV_W].reshape(b, s, NSA_GROUPS, HEAD_DIM)
        return t.transpose(0, 2, 1, 3).reshape(b, NSA_GROUPS, nch, CMP_STRIDE * HEAD_DIM)

    def w2_pad(w2):
        z = jnp.zeros_like(w2)
        return jnp.stack([jnp.concatenate([w2, z], axis=1), jnp.concatenate([z, w2], axis=1)]).astype(BF16)

    kc, vct = _compress(chunks(COL_NKC), chunks(COL_NVC),
                        cmp_pos_k.reshape(2, -1), cmp_pos_v.reshape(2, -1),
                        cmp_w1_k.astype(BF16), cmp_w1_v.astype(BF16), w2_pad(cmp_w2_k), w2_pad(cmp_w2_v))
    mimp_t = jnp.asarray(_importance_matrix(nch, nsb, nsbp), BF16)
    o_c, sel = _cmp_select(fm, kc, vct, mimp_t, nsb)
    o_s, o_w = _slc_win(tok, fm, sel, bias_tiles)

    x1 = _merge(x.reshape(rows, d), tok2d, o_a.reshape(rows, MOBA_W), o_c.reshape(rows, NSA_W),
                o_s.reshape(rows, NSA_W), o_w.reshape(rows, NSA_W), jnp.asarray(_gate_expand(), BF16),
                w_branch_a.astype(BF16), w_branch_b[_nq_perm(), :].astype(BF16), w_out.astype(BF16), tm=512)

    mem_len = mem.shape[1]
    memkv = _norm_matmul(mem.reshape(b * mem_len, d), norm_mem_g, w_xkv.astype(BF16), tm=mem_len,
                         name="mem_kv").reshape(b, mem_len, 2 * XATTN_W)
    x2 = _xattn(x1.reshape(b, s, d), norm_xattn_g, w_xq.astype(BF16), memkv, w_xo.astype(BF16), tm=512)

    return _ffn(x2, norm_ffn_g, w_gate.astype(BF16), w_up.astype(BF16), conv_w, conv_b,
                w_down.astype(BF16), norm_out_g, tm=512)


def kernel(x, mem, rel_bias, norm_mix_g, w_in, cmp_pos_k, cmp_w1_k, cmp_w2_k, cmp_pos_v, cmp_w1_v, cmp_w2_v, w_branch_a, w_branch_b, w_out, norm_xattn_g, norm_mem_g, w_xq, w_xkv, w_xo, norm_ffn_g, w_gate, w_up, conv_w, conv_b, w_down, norm_final_g):
    depth = w_in.shape[0]
    assert depth == 1, "the final norm is fused into the last layer's FFN kernel"
    bias_tiles = _bias_tiles(rel_bias)
    l = 0
    return _layer(x, mem, bias_tiles, norm_mix_g[l], w_in[l], cmp_pos_k[l], cmp_w1_k[l],
                  cmp_w2_k[l], cmp_pos_v[l], cmp_w1_v[l], cmp_w2_v[l], w_branch_a[l], w_branch_b[l], w_out[l],
                  norm_xattn_g[l], norm_mem_g[l], w_xq[l], w_xkv[l], w_xo[l],
                  norm_ffn_g[l], w_gate[l], w_up[l], conv_w[l], conv_b[l], w_down[l], norm_final_g)
```

```python
import functools
import math

import numpy as np
import jax
import jax.numpy as jnp
from jax import lax
from jax.experimental import pallas as pl
from jax.experimental.pallas import tpu as pltpu

F32 = jnp.float32
BF16 = jnp.bfloat16

D_MODEL = 1024
HEAD_DIM = 64
MOBA_HEADS = 8
MOBA_BLOCK = 256
MOBA_TOPK = 3
NSA_HEADS = 8
NSA_GROUPS = 2
NSA_HPG = NSA_HEADS // NSA_GROUPS
CMP_BLOCK = 32
CMP_STRIDE = 16
CMP_HIDDEN = 256
SLC_BLOCK = 64
SLC_TOPN = 16
WINDOW = 512
REL_BUCKETS = 32
REL_MAX_DIST = 128
XATTN_HEADS = 4
XATTN_HEAD_DIM = 128
D_FF = 2816
CONV_WIDTH = 3
Q_BLOCK = 128
RMS_EPS = 1e-6
NEG_BIG = -1e30
LOG2E = math.log2(math.e)

LANES = 128
TQ = 256
KT = 256
STRIP = 128
ONES_ROWS = 16
VROWS = HEAD_DIM + ONES_ROWS
VMEM_LIMIT = 56 * 1024 * 1024

MOBA_W = MOBA_HEADS * HEAD_DIM
NSA_W = NSA_HEADS * HEAD_DIM
NSA_KV_W = NSA_GROUPS * HEAD_DIM
XATTN_W = XATTN_HEADS * XATTN_HEAD_DIM
GATE_PAD = LANES

COL_GA = 0
COL_GB = COL_GA + D_MODEL
COL_MK = COL_GB + D_MODEL
COL_NKC = COL_MK + MOBA_W
COL_NVC = COL_NKC + NSA_KV_W
COL_NKS = COL_NVC + NSA_KV_W
COL_NKW = COL_NKS + NSA_KV_W
COL_NGATE = COL_NKW + NSA_KV_W
TOK_W = COL_NGATE + GATE_PAD
ROW_MQ = 0
ROW_MV = ROW_MQ + MOBA_W
ROW_NQ = ROW_MV + MOBA_W
ROW_NVS = ROW_NQ + NSA_W
ROW_NVW = ROW_NVS + NSA_KV_W
FM_W = ROW_NVW + NSA_KV_W

BT_OWN, BT_PREV, BT_UPPER, BT_MASKED = 0, 1, 2, 3
N_BIAS_TILES = 4


def _bucket_thresholds():
    n = np.arange(0, 4 * REL_MAX_DIST)
    exact = REL_BUCKETS // 2
    nf = np.maximum(n, 1).astype(np.float64)
    large = exact + (np.log(nf / exact) / math.log(REL_MAX_DIST / exact) * (REL_BUCKETS - exact)).astype(np.int64)
    large = np.minimum(large, REL_BUCKETS - 1)
    b = np.where(n < exact, n, large)
    return [int(np.argmax(b >= k)) for k in range(REL_BUCKETS)]


_THR = _bucket_thresholds()
assert _THR[REL_BUCKETS - 1] <= LANES


def _dot(a, b):
    return jnp.dot(a, b, preferred_element_type=F32)


def _dot_nt(a, b):
    return lax.dot_general(a, b, (((1,), (1,)), ((), ())), preferred_element_type=F32)


def _rms(x, g):
    return x * lax.rsqrt(jnp.mean(x * x, axis=-1, keepdims=True) + RMS_EPS) * g


def _params(n_axes):
    return pltpu.CompilerParams(dimension_semantics=("arbitrary",) * n_axes,
                                vmem_limit_bytes=VMEM_LIMIT)


def _bias_kernel(rb_ref, out_ref):
    h = pl.program_id(0)
    j = lax.broadcasted_iota(jnp.int32, (KT, TQ), 0)
    i = lax.broadcasted_iota(jnp.int32, (KT, TQ), 1)
    far = rb_ref[REL_BUCKETS - 1, h]

    def table(d):
        t = jnp.full((KT, TQ), (rb_ref[0, h] - far) * LOG2E, F32)
        for k in range(1, REL_BUCKETS):
            t = jnp.where(d >= _THR[k], (rb_ref[k, h] - far) * LOG2E, t)
        return t

    neg = jnp.full((KT, TQ), NEG_BIG, F32)
    out_ref[0, BT_OWN] = jnp.where(i >= j, table(i - j), neg)
    out_ref[0, BT_PREV] = table(i - j + KT)
    out_ref[0, BT_UPPER] = jnp.where(j > i, 0.0, neg)
    out_ref[0, BT_MASKED] = neg


def _bias_tiles(rel_bias):
    nh = rel_bias.shape[1]
    return pl.pallas_call(
        _bias_kernel,
        grid=(nh,),
        in_specs=[pl.BlockSpec(memory_space=pltpu.SMEM)],
        out_specs=pl.BlockSpec((1, N_BIAS_TILES, KT, TQ), lambda h: (h, 0, 0, 0)),
        out_shape=jax.ShapeDtypeStruct((nh, N_BIAS_TILES, KT, TQ), F32),
        compiler_params=_params(1),
        name="bias_tiles",
    )(rel_bias)


def _inproj_kernel(x_ref, g_ref, wt_ref, wf_ref, tok_ref, fm_ref, *, chunk):
    h = _rms(x_ref[...], g_ref[...]).astype(BF16)
    n = wt_ref.shape[1]
    for c0 in range(0, n, chunk):
        c1 = min(c0 + chunk, n)
        tok_ref[:, c0:c1] = _dot(h, wt_ref[:, c0:c1]).astype(tok_ref.dtype)
    n = wf_ref.shape[0]
    for c0 in range(0, n, chunk):
        c1 = min(c0 + chunk, n)
        fm_ref[c0:c1, :] = _dot_nt(wf_ref[c0:c1, :], h).astype(fm_ref.dtype)


def _in_proj(x, g, w_tok, w_fm_t, tm):
    b, s, d = x.shape
    return pl.pallas_call(
        functools.partial(_inproj_kernel, chunk=512),
        grid=(b, s // tm),
        in_specs=[pl.BlockSpec((None, tm, d), lambda i, j: (i, j, 0)),
                  pl.BlockSpec((1, d), lambda i, j: (0, 0)),
                  pl.BlockSpec(w_tok.shape, lambda i, j: (0, 0)),
                  pl.BlockSpec(w_fm_t.shape, lambda i, j: (0, 0))],
        out_specs=[pl.BlockSpec((None, tm, TOK_W), lambda i, j: (i, j, 0)),
                   pl.BlockSpec((None, FM_W, tm), lambda i, j: (i, 0, j))],
        out_shape=[jax.ShapeDtypeStruct((b, s, TOK_W), BF16),
                   jax.ShapeDtypeStruct((b, FM_W, s), BF16)],
        compiler_params=_params(2),
        name="in_proj",
    )(x, g.reshape(1, d), w_tok, w_fm_t)


def _norm_matmul_kernel(x_ref, g_ref, w_ref, o_ref):
    o_ref[...] = _dot(_rms(x_ref[...], g_ref[...]).astype(BF16), w_ref[...]).astype(o_ref.dtype)


def _norm_matmul(x2d, g, w, tm, name):
    m, d = x2d.shape
    n = w.shape[1]
    return pl.pallas_call(
        _norm_matmul_kernel,
        grid=(m // tm,),
        in_specs=[pl.BlockSpec((tm, d), lambda i: (i, 0)),
                  pl.BlockSpec((1, d), lambda i: (0, 0)),
                  pl.BlockSpec((d, n), lambda i: (0, 0))],
        out_specs=pl.BlockSpec((tm, n), lambda i: (i, 0)),
        out_shape=jax.ShapeDtypeStruct((m, n), BF16),
        compiler_params=_params(1),
        name=name,
    )(x2d, g.reshape(1, d), w)


def _kmean_kernel(k_ref, o_ref):
    nb = o_ref.shape[0]
    for n in range(nb):
        blk = k_ref[n * MOBA_BLOCK:(n + 1) * MOBA_BLOCK, :].astype(F32)
        o_ref[n:n + 1, :] = jnp.mean(blk, axis=0, keepdims=True)


def _moba_kmean(tok):
    b, s, _ = tok.shape
    nb = s // MOBA_BLOCK
    return pl.pallas_call(
        _kmean_kernel,
        grid=(b,),
        in_specs=[pl.BlockSpec((None, s, MOBA_W), lambda i: (i, 0, COL_MK // MOBA_W))],
        out_specs=pl.BlockSpec((None, nb, MOBA_W), lambda i: (i, 0, 0)),
        out_shape=jax.ShapeDtypeStruct((b, nb, MOBA_W), F32),
        compiler_params=_params(1),
        name="moba_kmean",
    )(tok)


def _fill_v_with_ones(vt_ref, lo_ref, hi_ref, cols=1024):
    s = vt_ref.shape[1]
    ones = jnp.ones((ONES_ROWS, cols), BF16)

    def body(c, carry):
        c0 = pl.multiple_of(c * cols, cols)
        lo_ref[0:HEAD_DIM, pl.ds(c0, cols)] = vt_ref[0:HEAD_DIM, pl.ds(c0, cols)]
        lo_ref[HEAD_DIM:VROWS, pl.ds(c0, cols)] = ones
        hi_ref[0:HEAD_DIM, pl.ds(c0, cols)] = vt_ref[HEAD_DIM:2 * HEAD_DIM, pl.ds(c0, cols)]
        hi_ref[HEAD_DIM:VROWS, pl.ds(c0, cols)] = ones
        return carry

    lax.fori_loop(0, s // cols, body, 0)


def _split_heads_t(q_t):
    row = lax.broadcasted_iota(jnp.int32, q_t.shape, 0)
    zero = jnp.zeros_like(q_t)
    return jnp.where(row < HEAD_DIM, q_t, zero), jnp.where(row >= HEAD_DIM, q_t, zero)


def _fill_k_with_block_ids(k_ref, lo_ref, hi_ref, block, nblock, rows=512):
    s = k_ref.shape[0]
    lane = lax.broadcasted_iota(jnp.int32, (rows, LANES), 1)
    shift = int(math.log2(block))

    def body(c, carry):
        r0 = pl.multiple_of(c * rows, rows)
        key = r0 + lax.broadcasted_iota(jnp.int32, (rows, LANES), 0)
        bid = lax.shift_right_logical(key, shift) & (nblock - 1)
        k = k_ref[pl.ds(r0, rows), :]
        lo_ref[pl.ds(r0, rows), :] = jnp.where(lane < HEAD_DIM, k, (lane - HEAD_DIM == bid).astype(BF16))
        hi_ref[pl.ds(r0, rows), :] = jnp.where(lane >= HEAD_DIM, k, (lane == bid).astype(BF16))
        return carry

    lax.fori_loop(0, s // rows, body, 0)


def _q_with_mask_rows(q_rows, mask_rows, low_head):
    r = lax.broadcasted_iota(jnp.int32, (ONES_ROWS, TQ), 0)
    m = jnp.zeros((ONES_ROWS, TQ), F32)
    for b, row in enumerate(mask_rows):
        m = jnp.where(r == b, row, m)
    pad = jnp.zeros((HEAD_DIM - ONES_ROWS, TQ), BF16)
    parts = [q_rows, m.astype(BF16), pad] if low_head else [m.astype(BF16), pad, q_rows]
    return jnp.concatenate(parts, axis=0)


def _pipe(bufs, m_ref, acc_ref, nheads, a=None, b=None, c=None, c_last=False):
    s_refs, p_refs, mx_ref, al_ref = bufs
    rows = s_refs[0].shape[1]

    def stage_c(h):
        c_slot, c_base, vt_fn = c
        acc_ref[c_base + h] = (al_ref[c_slot, h] * acc_ref[c_base + h]
                               + _dot(vt_fn(h), p_refs[c_slot][h]))

    for h in range(nheads):
        if c is not None and not c_last:
            stage_c(h)
        if a is not None:
            a_slot, score_fn, bias_fn = a
            raw = score_fn(h)
            mx = None
        if b is not None:
            b_slot, b_base = b
            m_prev = m_ref[b_base + h]
            m_next = jnp.maximum(m_prev, mx_ref[b_slot, h])
            al_ref[b_slot, h] = jnp.exp2(m_prev - m_next)
            m_ref[b_base + h] = m_next
        for i in range(rows // STRIP):
            r = slice(i * STRIP, (i + 1) * STRIP)
            if b is not None:
                p_refs[b_slot][h, r, :] = jnp.exp2(s_refs[b_slot][h, r, :] - m_next).astype(BF16)
            if a is not None:
                s = raw[r, :] if bias_fn is None else raw[r, :] + bias_fn(h, i)
                s_refs[a_slot][h, r, :] = s
                cmax = jnp.max(s, axis=0, keepdims=True)
                mx = cmax if mx is None else jnp.maximum(mx, cmax)
        if a is not None:
            mx_ref[a_slot, h] = mx
        if c is not None and c_last:
            stage_c(h)


def _reset_state(bufs, m_ref, acc_ref):
    _, p_refs, _, al_ref = bufs
    m_ref[...] = jnp.full(m_ref.shape, NEG_BIG, F32)
    acc_ref[...] = jnp.zeros(acc_ref.shape, F32)
    p_refs[1][...] = jnp.zeros(p_refs[1].shape, BF16)
    al_ref[...] = jnp.ones(al_ref.shape, F32)


def _pair_output(acc_lo, acc_hi):
    o_lo = acc_lo[0:HEAD_DIM, :] / acc_lo[HEAD_DIM:HEAD_DIM + 1, :]
    o_hi = acc_hi[0:HEAD_DIM, :] / acc_hi[HEAD_DIM:HEAD_DIM + 1, :]
    return jnp.transpose(jnp.concatenate([o_lo, o_hi], axis=0))


def _top_k_rows(val, n_f, k, sentinel):
    chosen = jnp.zeros(val.shape, F32)
    for _ in range(k):
        mx = jnp.max(val, axis=0, keepdims=True)
        idx = jnp.min(jnp.where(val == mx, n_f, sentinel), axis=0, keepdims=True)
        pick = n_f == idx
        chosen = jnp.where(pick, 1.0, chosen)
        val = jnp.where(pick, -jnp.inf, val)
    return chosen


def _moba_kernel(qt_ref, k_ref, vt_ref, km_ref, bt_ref, o_ref,
                 k0_ref, k1_ref, v0_ref, v1_ref, mb_ref, qs_ref, m_ref, acc_ref,
                 s0_ref, s1_ref, p0_ref, p1_ref, mx_ref, al_ref):
    cur = pl.program_id(2)
    bufs = ((s0_ref, s1_ref), (p0_ref, p1_ref), mx_ref, al_ref)
    nb = km_ref.shape[0]
    sk = s0_ref.shape[1]
    per_block = KT // STRIP

    @pl.when(cur == 0)
    def _():
        _fill_v_with_ones(vt_ref, v0_ref, v1_ref)
        _fill_k_with_block_ids(k_ref, k0_ref, k1_ref, MOBA_BLOCK, 2)

    km = km_ref[...].astype(BF16)
    n_f = lax.broadcasted_iota(jnp.int32, (nb, TQ), 0).astype(F32)
    past = n_f < cur.astype(F32)
    k_refs = (k0_ref, k1_ref)
    v_refs = (v0_ref, v1_ref)
    for hh, qh in enumerate(_split_heads_t(qt_ref[...])):
        gate = jnp.where(past, _dot(km, qh), -jnp.inf)
        chosen = _top_k_rows(gate, n_f, min(MOBA_TOPK, nb), float(nb))
        mb_ref[hh] = jnp.where((chosen > 0.5) & past, 0.0, NEG_BIG)
        qs_ref[hh] = qh
    _reset_state(bufs, m_ref, acc_ref)

    def first_block(u):
        return jnp.maximum(cur - 2 * u - 1, 0)

    def score_fn(u, first_rows, second_rows):
        idx = first_block(u)
        ks = pl.multiple_of(idx * KT, KT)
        first_even = (idx & 1) == 0

        def f(hh):
            even = jnp.where(first_even, first_rows[hh], second_rows[hh])
            odd = jnp.where(first_even, second_rows[hh], first_rows[hh])
            q_rows = qs_ref[hh, 0:HEAD_DIM, :] if hh == 0 else qs_ref[hh, HEAD_DIM:2 * HEAD_DIM, :]
            return _dot(k_refs[hh][pl.ds(ks, sk), :], _q_with_mask_rows(q_rows, (even, odd), hh == 0))
        return f

    def vt_fn(u):
        ks = pl.multiple_of(first_block(jnp.maximum(u, 0)) * KT, KT)
        return lambda hh: v_refs[hh][:, pl.ds(ks, sk)]

    has_prev = cur >= 1
    kind_lo = jnp.where(has_prev, BT_PREV, BT_OWN)
    kind_hi = jnp.where(has_prev, BT_OWN, BT_MASKED)
    keep = jnp.where(has_prev, 1.0, 0.0)
    prev_rows = [mb_ref[hh, pl.ds(jnp.maximum(cur - 1, 0), 1), :] * keep for hh in range(2)]
    no_rows = [jnp.zeros((1, TQ), F32)] * 2

    def near_bias(hh, i):
        half, part = divmod(i, per_block)
        return bt_ref[hh, kind_lo if half == 0 else kind_hi, part * STRIP:(part + 1) * STRIP, :]

    def far_scores(u):
        b_lo = cur - 2 * u - 1
        idx = jnp.maximum(b_lo, 0)
        pen_lo = jnp.where(b_lo >= -1, 0.0, NEG_BIG)
        pen_hi = jnp.where(b_lo >= 0, 0.0, NEG_BIG)
        return score_fn(u, [mb_ref[hh, pl.ds(idx, 1), :] + pen_lo for hh in range(2)],
                        [mb_ref[hh, pl.ds(idx + 1, 1), :] + pen_hi for hh in range(2)])

    pipe = functools.partial(_pipe, bufs, m_ref, acc_ref, 2)
    pipe(a=(0, score_fn(0, prev_rows, no_rows), near_bias))
    n_far = lax.shift_right_logical(cur, 1)

    def two_jobs(t, carry):
        u0 = 2 * t + 1
        pipe(a=(1, far_scores(u0), None), b=(0, 0), c=(1, 0, vt_fn(u0 - 2)))
        pipe(a=(0, far_scores(u0 + 1), None), b=(1, 0), c=(0, 0, vt_fn(u0 - 1)))
        return carry

    n_iter = lax.shift_right_logical(n_far + 1, 1)
    lax.fori_loop(0, n_iter, two_jobs, 0)
    pipe(b=(0, 0), c=(1, 0, vt_fn(2 * n_iter - 1)))
    pipe(c=(0, 0, vt_fn(2 * n_iter)))
    o_ref[...] = _pair_output(acc_ref[0], acc_ref[1]).astype(o_ref.dtype)


def _moba_attention(tok, fm, kmean, bias_tiles):
    b, s, _ = tok.shape
    nb = s // MOBA_BLOCK
    npair = MOBA_HEADS // 2
    sk = 2 * KT
    return pl.pallas_call(
        _moba_kernel,
        grid=(b, npair, s // TQ),
        in_specs=[
            pl.BlockSpec((None, LANES, TQ), lambda i, p, c: (i, ROW_MQ // LANES + p, c)),
            pl.BlockSpec((None, s, LANES), lambda i, p, c: (i, 0, COL_MK // LANES + p)),
            pl.BlockSpec((None, LANES, s), lambda i, p, c: (i, ROW_MV // LANES + p, 0)),
            pl.BlockSpec((None, nb, LANES), lambda i, p, c: (i, 0, p)),
            pl.BlockSpec((2, N_BIAS_TILES, KT, TQ), lambda i, p, c: (p, 0, 0, 0)),
        ],
        out_specs=pl.BlockSpec((None, TQ, LANES), lambda i, p, c: (i, c, p)),
        out_shape=jax.ShapeDtypeStruct((b, s, MOBA_W), BF16),
        scratch_shapes=[
            pltpu.VMEM((s, LANES), BF16), pltpu.VMEM((s, LANES), BF16),
            pltpu.VMEM((VROWS, s), BF16), pltpu.VMEM((VROWS, s), BF16),
            pltpu.VMEM((2, nb, TQ), F32),
            pltpu.VMEM((2, LANES, TQ), BF16),
            pltpu.VMEM((2, 1, TQ), F32),
            pltpu.VMEM((2, VROWS, TQ), F32),
            pltpu.VMEM((2, sk, TQ), F32), pltpu.VMEM((2, sk, TQ), F32),
            pltpu.VMEM((2, sk, TQ), BF16), pltpu.VMEM((2, sk, TQ), BF16),
            pltpu.VMEM((2, 2, 1, TQ), F32), pltpu.VMEM((2, 2, 1, TQ), F32),
        ],
        compiler_params=_params(3),
        name="moba_attention",
    )(fm, tok, fm, kmean, bias_tiles)


def _compress_kernel(ck_ref, cv_ref, pk_ref, pv_ref, w1k_ref, w1v_ref, w2k_ref, w2v_ref,
                     kc_ref, vct_ref):
    nch = ck_ref.shape[1]
    half = CMP_STRIDE * HEAD_DIM

    def run(c_ref, pos_ref, w1_ref, w2_ref):
        acc = jnp.zeros((nch, LANES), F32)
        for g in range(NSA_GROUPS):
            c = c_ref[g].astype(F32)
            lo = (c + pos_ref[0:1, :]).astype(BF16)
            hi = (c + pos_ref[1:2, :]).astype(BF16)
            pre = _dot(lo, w1_ref[:half, :]) + pltpu.roll(_dot(hi, w1_ref[half:, :]), nch - 1, axis=0)
            acc = acc + _dot(jax.nn.gelu(pre).astype(BF16), w2_ref[g])
        return acc

    kc_ref[...] = run(ck_ref, pk_ref, w1k_ref, w2k_ref).astype(kc_ref.dtype)
    vct_ref[...] = jnp.transpose(run(cv_ref, pv_ref, w1v_ref, w2v_ref)).astype(vct_ref.dtype)


def _compress(ck, cv, pos_k, pos_v, w1k, w1v, w2k, w2v):
    b, g, nch, width = ck.shape
    full = lambda shape: pl.BlockSpec(shape, lambda i: (0,) * len(shape))
    return pl.pallas_call(
        _compress_kernel,
        grid=(b,),
        in_specs=[pl.BlockSpec((None, g, nch, width), lambda i: (i, 0, 0, 0)),
                  pl.BlockSpec((None, g, nch, width), lambda i: (i, 0, 0, 0)),
                  full(pos_k.shape), full(pos_v.shape), full(w1k.shape), full(w1v.shape),
                  full(w2k.shape), full(w2v.shape)],
        out_specs=[pl.BlockSpec((None, nch, LANES), lambda i: (i, 0, 0)),
                   pl.BlockSpec((None, LANES, nch), lambda i: (i, 0, 0))],
        out_shape=[jax.ShapeDtypeStruct((b, nch, LANES), BF16),
                   jax.ShapeDtypeStruct((b, LANES, nch), BF16)],
        compiler_params=_params(1),
        name="nsa_compress",
    )(ck, cv, pos_k, pos_v, w1k, w1v, w2k, w2v)


def _cmp_select_kernel(qt_ref, kc_ref, vct_ref, mimp_ref, o_ref, sel_ref, *, nsb, nq):
    ci = pl.program_id(1)
    nch = kc_ref.shape[0]
    nsbp = mimp_ref.shape[0]
    parts = 4 if (nch // 4) % LANES == 0 and (nsbp // 4) % 8 == 0 else 1
    for v in range(1, parts + 1):
        @pl.when((ci * parts >= (v - 1) * nq) & (ci * parts < v * nq))
        def _(v=v):
            _cmp_select_body(qt_ref, kc_ref, vct_ref, mimp_ref, o_ref, sel_ref,
                             nsb=nsb, nch=nch * v // parts, nrow=nsbp * v // parts)


def _cmp_select_body(qt_ref, kc_ref, vct_ref, mimp_ref, o_ref, sel_ref, *, nsb, nch, nrow):
    qs = pl.program_id(1) * TQ
    n_row = lax.broadcasted_iota(jnp.int32, (nch, TQ), 0)
    t_col = qs + lax.broadcasted_iota(jnp.int32, (nch, TQ), 1)
    hidden = jnp.where((n_row * CMP_STRIDE + (CMP_BLOCK - 1)) <= t_col, 0.0, NEG_BIG)
    any_visible = t_col[0:1, :] >= (CMP_BLOCK - 1)
    kc = kc_ref[0:nch, :]
    vct = vct_ref[:, 0:nch]
    vrow = lax.broadcasted_iota(jnp.int32, vct.shape, 0)
    mimp = mimp_ref[0:nrow, 0:nch]

    j_row = lax.broadcasted_iota(jnp.int32, (nrow, TQ), 0)
    t_sel = qs + lax.broadcasted_iota(jnp.int32, (nrow, TQ), 1)
    cur = lax.shift_right_logical(t_sel, int(math.log2(SLC_BLOCK)))
    j_f = j_row.astype(F32)
    forced = (j_row == 0) | (j_row == cur) | (j_row == cur - 1)
    allowed = (j_row <= cur) & (j_row < nsb)

    q_heads = [_split_heads_t(qt_ref[r * LANES:(r + 1) * LANES, :]) for r in range(NSA_HPG)]
    out_t = [jnp.zeros((LANES, TQ), F32) for _ in range(NSA_HPG)]
    for g in range(NSA_GROUPS):
        vin_g = (vrow < HEAD_DIM) if g == 0 else (vrow >= HEAD_DIM)
        vct_g = jnp.where(vin_g, vct, jnp.zeros_like(vct))
        p_grp = jnp.zeros((nch, TQ), F32)
        for r in range(NSA_HPG):
            s = _dot(kc, q_heads[r][g]) + hidden
            e = jnp.exp2(s - jnp.max(s, axis=0, keepdims=True))
            p = e * jnp.where(any_visible, 1.0 / jnp.sum(e, axis=0, keepdims=True), 0.0)
            p_grp = p_grp + p
            out_t[r] = out_t[r] + _dot(vct_g, p.astype(BF16))
        p_hi = p_grp.astype(BF16)
        p_lo = (p_grp - p_hi.astype(F32)).astype(BF16)
        imp = _dot(mimp, p_hi) + _dot(mimp, p_lo)
        val = jnp.where(forced & allowed, jnp.inf, jnp.where(allowed, imp, -jnp.inf))
        chosen = _top_k_rows(val, j_f, min(SLC_TOPN, nsb), float(nrow))
        sel_ref[g, 0:nrow, :] = jnp.where(allowed & (chosen > 0.5), 0.0, NEG_BIG)
        if nrow < sel_ref.shape[1]:
            sel_ref[g, nrow:, :] = jnp.full((sel_ref.shape[1] - nrow, TQ), NEG_BIG, F32)
    for r in range(NSA_HPG):
        o_ref[:, r * LANES:(r + 1) * LANES] = jnp.transpose(out_t[r]).astype(o_ref.dtype)


def _cmp_select(fm, kc, vct, mimp_t, nsb):
    b, _, s = fm.shape
    nch = kc.shape[1]
    nsbp = mimp_t.shape[0]
    return pl.pallas_call(
        functools.partial(_cmp_select_kernel, nsb=nsb, nq=s // TQ),
        grid=(b, s // TQ),
        in_specs=[pl.BlockSpec((None, NSA_W, TQ), lambda i, c: (i, ROW_NQ // NSA_W, c)),
                  pl.BlockSpec((None, nch, LANES), lambda i, c: (i, 0, 0)),
                  pl.BlockSpec((None, LANES, nch), lambda i, c: (i, 0, 0)),
                  pl.BlockSpec((nsbp, nch), lambda i, c: (0, 0))],
        out_specs=[pl.BlockSpec((None, TQ, NSA_W), lambda i, c: (i, c, 0)),
                   pl.BlockSpec((None, NSA_GROUPS, nsbp, TQ), lambda i, c: (i, 0, 0, c))],
        out_shape=[jax.ShapeDtypeStruct((b, s, NSA_W), BF16),
                   jax.ShapeDtypeStruct((b, NSA_GROUPS, nsbp, s), F32)],
        compiler_params=_params(2),
        name="nsa_cmp_select",
    )(fm, kc, vct, mimp_t)


def _slc_win_kernel(qt_ref, ks_ref, vst_ref, kw_ref, vwt_ref, sel_ref, bt_ref,
                    oslc_ref, owin_ref,
                    ks0_ref, ks1_ref, vs0_ref, vs1_ref, vw0_ref, vw1_ref, qs_ref, m_ref, acc_ref,
                    s0_ref, s1_ref, p0_ref, p1_ref, mx_ref, al_ref):
    cur = pl.program_id(1)
    bufs = ((s0_ref, s1_ref), (p0_ref, p1_ref), mx_ref, al_ref)
    per_tile = KT // SLC_BLOCK

    @pl.when(cur == 0)
    def _():
        _fill_v_with_ones(vst_ref, vs0_ref, vs1_ref)
        _fill_v_with_ones(vwt_ref, vw0_ref, vw1_ref)
        _fill_k_with_block_ids(ks_ref, ks0_ref, ks1_ref, SLC_BLOCK, per_tile)

    nh = NSA_HEADS
    for r in range(NSA_HPG):
        lo, hi = _split_heads_t(qt_ref[r * LANES:(r + 1) * LANES, :])
        qs_ref[r] = lo
        qs_ref[NSA_HPG + r] = hi
    _reset_state(bufs, m_ref, acc_ref)
    ks_refs = (ks0_ref, ks1_ref)
    vs_refs = (vs0_ref, vs1_ref)
    vw_refs = (vw0_ref, vw1_ref)
    sel_base, win_base = 0, nh

    def tile_start(n):
        return pl.multiple_of(jnp.maximum(n, 0) * KT, KT)

    def win_scores(n):
        ks = tile_start(n)
        return lambda h: _dot(kw_ref[pl.ds(ks, KT), :], qs_ref[h])

    def sel_scores(n, penalty):
        ks = tile_start(n)
        base = jnp.maximum(n, 0) * per_tile
        rows = [[sel_ref[g, pl.ds(base + b, 1), :] + penalty for b in range(per_tile)]
                for g in range(NSA_GROUPS)]

        def f(h):
            g = h // NSA_HPG
            q_rows = qs_ref[h, 0:HEAD_DIM, :] if g == 0 else qs_ref[h, HEAD_DIM:2 * HEAD_DIM, :]
            return _dot(ks_refs[g][pl.ds(ks, KT), :], _q_with_mask_rows(q_rows, rows[g], g == 0))
        return f

    def vt_fn(v_refs, n):
        ks = tile_start(n)
        return lambda h: v_refs[h // NSA_HPG][:, pl.ds(ks, KT)]

    def near_bias(kind):
        return lambda h, i: bt_ref[h, kind, i * STRIP:(i + 1) * STRIP, :]

    pipe = functools.partial(_pipe, bufs, m_ref, acc_ref, nh, c_last=True)
    kind_prev = jnp.where(cur >= 1, BT_PREV, BT_MASKED)
    kind_upper = jnp.where(cur >= 2, BT_UPPER, BT_MASKED)
    pipe(a=(0, win_scores(cur), near_bias(BT_OWN)))
    pipe(a=(1, win_scores(cur - 1), near_bias(kind_prev)), b=(0, win_base))
    pipe(a=(0, win_scores(cur - 2), near_bias(kind_upper)), b=(1, win_base),
         c=(0, win_base, vt_fn(vw_refs, cur)))
    pipe(a=(1, sel_scores(cur, 0.0), near_bias(BT_OWN)), b=(0, win_base),
         c=(1, win_base, vt_fn(vw_refs, cur - 1)))
    pipe(a=(0, sel_scores(cur - 1, 0.0), near_bias(kind_prev)), b=(1, sel_base),
         c=(0, win_base, vt_fn(vw_refs, cur - 2)))

    def far_scores(n):
        return sel_scores(n, jnp.where(n >= 0, 0.0, NEG_BIG))

    def two_tiles(t, carry):
        n0 = cur - 2 - 2 * t
        pipe(a=(1, far_scores(n0), None), b=(0, sel_base), c=(1, sel_base, vt_fn(vs_refs, n0 + 2)))
        pipe(a=(0, far_scores(n0 - 1), None), b=(1, sel_base), c=(0, sel_base, vt_fn(vs_refs, n0 + 1)))
        return carry

    n_iter = lax.shift_right_logical(jnp.maximum(cur - 1, 0) + 1, 1)
    lax.fori_loop(0, n_iter, two_tiles, 0)
    pipe(b=(0, sel_base), c=(1, sel_base, vt_fn(vs_refs, cur - 2 * n_iter)))
    pipe(c=(0, sel_base, vt_fn(vs_refs, cur - 1 - 2 * n_iter)))

    for r in range(NSA_HPG):
        cols = slice(r * LANES, (r + 1) * LANES)
        oslc_ref[:, cols] = _pair_output(acc_ref[r], acc_ref[NSA_HPG + r]).astype(oslc_ref.dtype)
        owin_ref[:, cols] = _pair_output(acc_ref[nh + r], acc_ref[nh + NSA_HPG + r]).astype(owin_ref.dtype)


def _slc_win(tok, fm, sel, bias_tiles):
    b, s, _ = tok.shape
    nsbp = sel.shape[2]
    return pl.pallas_call(
        _slc_win_kernel,
        grid=(b, s // TQ),
        in_specs=[pl.BlockSpec((None, NSA_W, TQ), lambda i, c: (i, ROW_NQ // NSA_W, c)),
                  pl.BlockSpec((None, s, LANES), lambda i, c: (i, 0, COL_NKS // LANES)),
                  pl.BlockSpec((None, LANES, s), lambda i, c: (i, ROW_NVS // LANES, 0)),
                  pl.BlockSpec((None, s, LANES), lambda i, c: (i, 0, COL_NKW // LANES)),
                  pl.BlockSpec((None, LANES, s), lambda i, c: (i, ROW_NVW // LANES, 0)),
                  pl.BlockSpec((None, NSA_GROUPS, nsbp, TQ), lambda i, c: (i, 0, 0, c)),
                  pl.BlockSpec((NSA_HEADS, N_BIAS_TILES, KT, TQ), lambda i, c: (1, 0, 0, 0))],
        out_specs=[pl.BlockSpec((None, TQ, NSA_W), lambda i, c: (i, c, 0)),
                   pl.BlockSpec((None, TQ, NSA_W), lambda i, c: (i, c, 0))],
        out_shape=[jax.ShapeDtypeStruct((b, s, NSA_W), BF16),
                   jax.ShapeDtypeStruct((b, s, NSA_W), BF16)],
        scratch_shapes=[pltpu.VMEM((s, LANES), BF16)] * 2 + [pltpu.VMEM((VROWS, s), BF16)] * 4 + [
            pltpu.VMEM((NSA_HEADS, LANES, TQ), BF16),
            pltpu.VMEM((2 * NSA_HEADS, 1, TQ), F32),
            pltpu.VMEM((2 * NSA_HEADS, VROWS, TQ), F32),
            pltpu.VMEM((NSA_HEADS, KT, TQ), F32), pltpu.VMEM((NSA_HEADS, KT, TQ), F32),
            pltpu.VMEM((NSA_HEADS, KT, TQ), BF16), pltpu.VMEM((NSA_HEADS, KT, TQ), BF16),
            pltpu.VMEM((2, NSA_HEADS, 1, TQ), F32), pltpu.VMEM((2, NSA_HEADS, 1, TQ), F32)],
        compiler_params=_params(2),
        name="nsa_slc_win",
    )(fm, tok, fm, tok, fm, sel, bias_tiles)


def _merge_kernel(x_ref, oa_ref, oc_ref, os_ref, ow_ref, ng_ref, ga_ref, gb_ref,
                  eg_ref, wa_ref, wb_ref, wo_ref, o_ref):
    sig = jax.nn.sigmoid(ng_ref[...].astype(F32))
    s_hi = sig.astype(BF16)
    s_lo = (sig - s_hi.astype(F32)).astype(BF16)
    eg = eg_ref[...]
    gexp = _dot(s_hi, eg) + _dot(s_lo, eg)
    o_b = (gexp[:, :NSA_W] * oc_ref[...].astype(F32)
           + gexp[:, NSA_W:2 * NSA_W] * os_ref[...].astype(F32)
           + gexp[:, 2 * NSA_W:] * ow_ref[...].astype(F32))
    a = _dot(oa_ref[...], wa_ref[...])
    bb = _dot(o_b.astype(BF16), wb_ref[...])
    merged = (jax.nn.sigmoid(ga_ref[...].astype(F32)) * a
              + jax.nn.sigmoid(gb_ref[...].astype(F32)) * bb)
    o_ref[...] = x_ref[...] + _dot(merged.astype(BF16), wo_ref[...])


def _merge(x2d, tok2d, o_a, o_c, o_s, o_w, egate, wa, wb, wo, tm):
    m, d = x2d.shape
    row = lambda w, col=0: pl.BlockSpec((tm, w), lambda i, col=col: (i, col))
    full = lambda a: pl.BlockSpec(a.shape, lambda i: (0, 0))
    return pl.pallas_call(
        _merge_kernel,
        grid=(m // tm,),
        in_specs=[row(d), row(MOBA_W), row(NSA_W), row(NSA_W), row(NSA_W),
                  row(GATE_PAD, COL_NGATE // GATE_PAD),
                  row(D_MODEL, COL_GA // D_MODEL), row(D_MODEL, COL_GB // D_MODEL),
                  full(egate), full(wa), full(wb), full(wo)],
        out_specs=row(d),
        out_shape=jax.ShapeDtypeStruct((m, d), F32),
        compiler_params=_params(1),
        name="merge_out",
    )(x2d, o_a, o_c, o_s, o_w, tok2d, tok2d, tok2d, egate, wa, wb, wo)


def _xattn_kernel(x_ref, g_ref, wq_ref, kv_ref, wo_ref, o_ref):
    x = x_ref[...]
    q = _dot(_rms(x, g_ref[...]).astype(BF16), wq_ref[...]).astype(BF16)
    heads = []
    for h in range(XATTN_HEADS):
        cols = slice(h * XATTN_HEAD_DIM, (h + 1) * XATTN_HEAD_DIM)
        k = kv_ref[:, cols]
        v = kv_ref[:, XATTN_W + h * XATTN_HEAD_DIM:XATTN_W + (h + 1) * XATTN_HEAD_DIM]
        s = _dot_nt(q[:, cols], k) * (XATTN_HEAD_DIM ** -0.5)
        e = jnp.exp(s - jnp.max(s, axis=1, keepdims=True))
        p = e / jnp.sum(e, axis=1, keepdims=True)
        heads.append(_dot(p.astype(BF16), v).astype(BF16))
    o_ref[...] = x + _dot(jnp.concatenate(heads, axis=1), wo_ref[...])


def _xattn(x3d, g, wq, memkv, wo, tm):
    b, s, d = x3d.shape
    mem_len = memkv.shape[1]
    return pl.pallas_call(
        _xattn_kernel,
        grid=(b, s // tm),
        in_specs=[pl.BlockSpec((None, tm, d), lambda i, j: (i, j, 0)),
                  pl.BlockSpec((1, d), lambda i, j: (0, 0)),
                  pl.BlockSpec(wq.shape, lambda i, j: (0, 0)),
                  pl.BlockSpec((None, mem_len, 2 * XATTN_W), lambda i, j: (i, 0, 0)),
                  pl.BlockSpec(wo.shape, lambda i, j: (0, 0))],
        out_specs=pl.BlockSpec((None, tm, d), lambda i, j: (i, j, 0)),
        out_shape=jax.ShapeDtypeStruct((b, s, d), F32),
        compiler_params=_params(2),
        name="xattn",
    )(x3d, g.reshape(1, d), wq, memkv, wo)


def _ffn_kernel(x_ref, halo_ref, g_ref, wg_ref, wu_ref, cw_ref, cb_ref, wd_ref, gf_ref, o_ref,
                y_ref, *, chunk, halo_rows):
    j = pl.program_id(1)
    x = x_ref[...]
    tm = x.shape[0]
    g = g_ref[...]
    hf = _rms(x, g).astype(BF16)
    keep = jnp.where(j > 0, 1.0, 0.0)
    halo = (_rms(halo_ref[...], g) * keep).astype(BF16)
    ext = jnp.concatenate([halo, hf], axis=0)
    row = lax.broadcasted_iota(jnp.int32, (tm, chunk), 0)
    starts = list(range(0, D_FF, chunk))

    def gate_up(c0):
        cols = slice(c0, c0 + chunk)
        return _dot(ext, wg_ref[:, cols]), _dot(hf, wu_ref[:, cols])

    nxt = gate_up(starts[0])
    for k, c0 in enumerate(starts):
        cols = slice(c0, c0 + chunk)
        ue, up = nxt
        if k + 1 < len(starts):
            nxt = gate_up(starts[k + 1])
        u = ue[halo_rows:, :]
        prev1 = ue[halo_rows - 1:halo_rows, :]
        prev2 = ue[halo_rows - 2:halo_rows - 1, :]
        u1 = jnp.where(row == 0, prev1, pltpu.roll(u, 1, axis=0))
        u2 = jnp.where(row == 0, prev2, jnp.where(row == 1, prev1, pltpu.roll(u, 2, axis=0)))
        a = cw_ref[0:1, cols] * u2 + cw_ref[1:2, cols] * u1 + cw_ref[2:3, cols] * u + cb_ref[:, cols]
        y_ref[:, cols] = (jax.nn.gelu(a) * up).astype(BF16)
    o_ref[...] = _rms(x + _dot(y_ref[...], wd_ref[...]), gf_ref[...])


def _ffn(x3d, g, wg, wu, cw, cb, wd, gf, tm):
    b, s, d = x3d.shape
    halo_rows = 16
    per = tm // halo_rows
    return pl.pallas_call(
        functools.partial(_ffn_kernel, chunk=256, halo_rows=halo_rows),
        grid=(b, s // tm),
        in_specs=[pl.BlockSpec((None, tm, d), lambda i, j: (i, j, 0)),
                  pl.BlockSpec((None, halo_rows, d), lambda i, j: (i, jnp.maximum(j * per - 1, 0), 0)),
                  pl.BlockSpec((1, d), lambda i, j: (0, 0)),
                  pl.BlockSpec(wg.shape, lambda i, j: (0, 0)),
                  pl.BlockSpec(wu.shape, lambda i, j: (0, 0)),
                  pl.BlockSpec(cw.shape, lambda i, j: (0, 0)),
                  pl.BlockSpec((1, D_FF), lambda i, j: (0, 0)),
                  pl.BlockSpec(wd.shape, lambda i, j: (0, 0)),
                  pl.BlockSpec((1, d), lambda i, j: (0, 0))],
        out_specs=pl.BlockSpec((None, tm, d), lambda i, j: (i, j, 0)),
        out_shape=jax.ShapeDtypeStruct((b, s, d), F32),
        scratch_shapes=[pltpu.VMEM((tm, D_FF), BF16)],
        compiler_params=_params(2),
        name="conv_ffn",
    )(x3d, x3d, g.reshape(1, d), wg, wu, cw, cb.reshape(1, D_FF), wd, gf.reshape(1, d))


def _nq_perm():
    idx = []
    for r in range(NSA_HPG):
        for g in range(NSA_GROUPS):
            h = g * NSA_HPG + r
            idx.extend(range(h * HEAD_DIM, (h + 1) * HEAD_DIM))
    return np.asarray(idx, np.int32)


def _gate_expand():
    e = np.zeros((GATE_PAD, 3 * NSA_W), np.float32)
    for br in range(3):
        for g in range(NSA_GROUPS):
            for r in range(NSA_HPG):
                c0 = br * NSA_W + r * LANES + g * HEAD_DIM
                e[br * NSA_HEADS + g * NSA_HPG + r, c0:c0 + HEAD_DIM] = 1.0
    return e


def _importance_matrix(nch, nsb, nsbp):
    ratio = SLC_BLOCK // CMP_STRIDE
    nc = nch - 1
    m = np.zeros((nsbp, nch), np.float32)
    for j in range(nsb):
        for a in range(ratio):
            for bb in range(CMP_BLOCK // CMP_STRIDE):
                c = j * ratio + a - bb
                if 0 <= c < nc:
                    m[j, c] += 1.0
    return m


def _split_w_in(w_in):
    widths = [MOBA_W] * 3 + [NSA_W] + [NSA_KV_W] * 6 + [3 * NSA_HEADS, D_MODEL, D_MODEL]
    cuts = np.cumsum(widths)[:-1]
    mq, mk, mv, nq, nkc, nvc, nks, nvs, nkw, nvw, ngate, ga, gb = jnp.split(w_in, cuts, axis=1)
    ngate = jnp.pad(ngate, ((0, 0), (0, GATE_PAD - 3 * NSA_HEADS)))
    q_scale = HEAD_DIM ** -0.5 * LOG2E
    mq, nq = mq * q_scale, nq * q_scale
    w_tok = jnp.concatenate([ga, gb, mk, nkc, nvc, nks, nkw, ngate], axis=1)
    w_fm = jnp.concatenate([mq, mv, nq[:, _nq_perm()], nvs, nvw], axis=1)
    return w_tok.astype(BF16), jnp.transpose(w_fm).astype(BF16)


def _layer(x, mem, bias_tiles, norm_mix_g, w_in, cmp_pos_k, cmp_w1_k, cmp_w2_k, cmp_pos_v,
           cmp_w1_v, cmp_w2_v, w_branch_a, w_branch_b, w_out, norm_xattn_g, norm_mem_g, w_xq, w_xkv, w_xo,
           norm_ffn_g, w_gate, w_up, conv_w, conv_b, w_down, norm_out_g):
    b, s, d = x.shape
    assert s % 1024 == 0 and d == D_MODEL
    rows = b * s
    nch = s // CMP_STRIDE
    nsb = s // SLC_BLOCK
    nsbp = max(LANES, nsb)

    w_tok, w_fm_t = _split_w_in(w_in)
    tok, fm = _in_proj(x, norm_mix_g, w_tok, w_fm_t, tm=512)
    tok2d = tok.reshape(rows, TOK_W)

    kmean = _moba_kmean(tok)
    o_a = _moba_attention(tok, fm, kmean, bias_tiles)

    def chunks(col):
        t = tok[:, :, col:col + NSA_KV_W].reshape(b, s, NSA_GROUPS, HEAD_DIM)
        return t.transpose(0, 2, 1, 3).reshape(b, NSA_GROUPS, nch, CMP_STRIDE * HEAD_DIM)

    def w2_pad(w2):
        z = jnp.zeros_like(w2)
        return jnp.stack([jnp.concatenate([w2, z], axis=1), jnp.concatenate([z, w2], axis=1)]).astype(BF16)

    kc, vct = _compress(chunks(COL_NKC), chunks(COL_NVC),
                        cmp_pos_k.reshape(2, -1), cmp_pos_v.reshape(2, -1),
                        cmp_w1_k.astype(BF16), cmp_w1_v.astype(BF16), w2_pad(cmp_w2_k), w2_pad(cmp_w2_v))
    mimp_t = jnp.asarray(_importance_matrix(nch, nsb, nsbp), BF16)
    o_c, sel = _cmp_select(fm, kc, vct, mimp_t, nsb)
    o_s, o_w = _slc_win(tok, fm, sel, bias_tiles)

    x1 = _merge(x.reshape(rows, d), tok2d, o_a.reshape(rows, MOBA_W), o_c.reshape(rows, NSA_W),
                o_s.reshape(rows, NSA_W), o_w.reshape(rows, NSA_W), jnp.asarray(_gate_expand(), BF16),
                w_branch_a.astype(BF16), w_branch_b[_nq_perm(), :].astype(BF16), w_out.astype(BF16), tm=512)

    mem_len = mem.shape[1]
    memkv = _norm_matmul(mem.reshape(b * mem_len, d), norm_mem_g, w_xkv.astype(BF16), tm=mem_len,
                         name="mem_kv").reshape(b, mem_len, 2 * XATTN_W)
    x2 = _xattn(x1.reshape(b, s, d), norm_xattn_g, w_xq.astype(BF16), memkv, w_xo.astype(BF16), tm=512)

    return _ffn(x2, norm_ffn_g, w_gate.astype(BF16), w_up.astype(BF16), conv_w, conv_b,
                w_down.astype(BF16), norm_out_g, tm=512)


def kernel(x, mem, rel_bias, norm_mix_g, w_in, cmp_pos_k, cmp_w1_k, cmp_w2_k, cmp_pos_v, cmp_w1_v, cmp_w2_v, w_branch_a, w_branch_b, w_out, norm_xattn_g, norm_mem_g, w_xq, w_xkv, w_xo, norm_ffn_g, w_gate, w_up, conv_w, conv_b, w_down, norm_final_g):
    depth = w_in.shape[0]
    assert depth == 1, "the final norm is fused into the last layer's FFN kernel"
    bias_tiles = _bias_tiles(rel_bias)
    l = 0
    return _layer(x, mem, bias_tiles, norm_mix_g[l], w_in[l], cmp_pos_k[l], cmp_w1_k[l],
                  cmp_w2_k[l], cmp_pos_v[l], cmp_w1_v[l], cmp_w2_v[l], w_branch_a[l], w_branch_b[l], w_out[l],
                  norm_xattn_g[l], norm_mem_g[l], w_xq[l], w_xkv[l], w_xo[l],
                  norm_ffn_g[l], w_gate[l], w_up[l], conv_w[l], conv_b[l], w_down[l], norm_final_g)
```

```python
import functools
import math

import numpy as np
import jax
import jax.numpy as jnp
from jax import lax
from jax.experimental import pallas as pl
from jax.experimental.pallas import tpu as pltpu

F32 = jnp.float32
BF16 = jnp.bfloat16

D_MODEL = 1024
HEAD_DIM = 64
MOBA_HEADS = 8
MOBA_BLOCK = 256
MOBA_TOPK = 3
NSA_HEADS = 8
NSA_GROUPS = 2
NSA_HPG = NSA_HEADS // NSA_GROUPS
CMP_BLOCK = 32
CMP_STRIDE = 16
CMP_HIDDEN = 256
SLC_BLOCK = 64
SLC_TOPN = 16
WINDOW = 512
REL_BUCKETS = 32
REL_MAX_DIST = 128
XATTN_HEADS = 4
XATTN_HEAD_DIM = 128
D_FF = 2816
CONV_WIDTH = 3
Q_BLOCK = 128
RMS_EPS = 1e-6
NEG_BIG = -1e30
LOG2E = math.log2(math.e)

LANES = 128
TQ = 256
KT = 256
STRIP = 128
ONES_ROWS = 16
VROWS = HEAD_DIM + ONES_ROWS
VMEM_LIMIT = 56 * 1024 * 1024

MOBA_W = MOBA_HEADS * HEAD_DIM
NSA_W = NSA_HEADS * HEAD_DIM
NSA_KV_W = NSA_GROUPS * HEAD_DIM
XATTN_W = XATTN_HEADS * XATTN_HEAD_DIM
GATE_PAD = LANES

COL_GA = 0
COL_GB = COL_GA + D_MODEL
COL_MK = COL_GB + D_MODEL
COL_NKS = COL_MK + MOBA_W
COL_NKW = COL_NKS + NSA_KV_W
COL_NGATE = COL_NKW + NSA_KV_W
TOK_W = COL_NGATE + GATE_PAD
CMP_W = 2 * NSA_KV_W
ROW_MQ = 0
ROW_MV = ROW_MQ + MOBA_W
ROW_NQ = ROW_MV + MOBA_W
ROW_NVS = ROW_NQ + NSA_W
ROW_NVW = ROW_NVS + NSA_KV_W
FM_W = ROW_NVW + NSA_KV_W

BT_OWN, BT_PREV, BT_UPPER, BT_MASKED = 0, 1, 2, 3
N_BIAS_TILES = 4


def _bucket_thresholds():
    n = np.arange(0, 4 * REL_MAX_DIST)
    exact = REL_BUCKETS // 2
    nf = np.maximum(n, 1).astype(np.float64)
    large = exact + (np.log(nf / exact) / math.log(REL_MAX_DIST / exact) * (REL_BUCKETS - exact)).astype(np.int64)
    large = np.minimum(large, REL_BUCKETS - 1)
    b = np.where(n < exact, n, large)
    return [int(np.argmax(b >= k)) for k in range(REL_BUCKETS)]


_THR = _bucket_thresholds()
assert _THR[REL_BUCKETS - 1] <= LANES


def _dot(a, b):
    return jnp.dot(a, b, preferred_element_type=F32)


def _dot_nt(a, b):
    return lax.dot_general(a, b, (((1,), (1,)), ((), ())), preferred_element_type=F32)


def _rms(x, g):
    return x * lax.rsqrt(jnp.mean(x * x, axis=-1, keepdims=True) + RMS_EPS) * g


def _params(n_axes):
    return pltpu.CompilerParams(dimension_semantics=("arbitrary",) * n_axes,
                                vmem_limit_bytes=VMEM_LIMIT)


def _bias_kernel(rb_ref, out_ref):
    h = pl.program_id(0)
    j = lax.broadcasted_iota(jnp.int32, (KT, TQ), 0)
    i = lax.broadcasted_iota(jnp.int32, (KT, TQ), 1)
    far = rb_ref[REL_BUCKETS - 1, h]

    def table(d):
        t = jnp.full((KT, TQ), (rb_ref[0, h] - far) * LOG2E, F32)
        for k in range(1, REL_BUCKETS):
            t = jnp.where(d >= _THR[k], (rb_ref[k, h] - far) * LOG2E, t)
        return t

    neg = jnp.full((KT, TQ), NEG_BIG, F32)
    out_ref[0, BT_OWN] = jnp.where(i >= j, table(i - j), neg)
    out_ref[0, BT_PREV] = table(i - j + KT)
    out_ref[0, BT_UPPER] = jnp.where(j > i, 0.0, neg)
    out_ref[0, BT_MASKED] = neg


def _bias_tiles(rel_bias):
    nh = rel_bias.shape[1]
    return pl.pallas_call(
        _bias_kernel,
        grid=(nh,),
        in_specs=[pl.BlockSpec(memory_space=pltpu.SMEM)],
        out_specs=pl.BlockSpec((1, N_BIAS_TILES, KT, TQ), lambda h: (h, 0, 0, 0)),
        out_shape=jax.ShapeDtypeStruct((nh, N_BIAS_TILES, KT, TQ), F32),
        compiler_params=_params(1),
        name="bias_tiles",
    )(rel_bias)


def _inproj_kernel(x_ref, g_ref, wt_ref, wf_ref, wc_ref, tok_ref, fm_ref, cmp_ref, *, chunk):
    h = _rms(x_ref[...], g_ref[...]).astype(BF16)
    c = _dot(h, wc_ref[...]).astype(cmp_ref.dtype)
    for i in range(cmp_ref.shape[0]):
        cmp_ref[i] = c[:, i * HEAD_DIM:(i + 1) * HEAD_DIM]
    n = wt_ref.shape[1]
    for c0 in range(0, n, chunk):
        c1 = min(c0 + chunk, n)
        tok_ref[:, c0:c1] = _dot(h, wt_ref[:, c0:c1]).astype(tok_ref.dtype)
    n = wf_ref.shape[0]
    for c0 in range(0, n, chunk):
        c1 = min(c0 + chunk, n)
        fm_ref[c0:c1, :] = _dot_nt(wf_ref[c0:c1, :], h).astype(fm_ref.dtype)


def _in_proj(x, g, w_tok, w_fm_t, w_cmp, tm):
    b, s, d = x.shape
    ncmp = CMP_W // HEAD_DIM
    return pl.pallas_call(
        functools.partial(_inproj_kernel, chunk=512),
        grid=(b, s // tm),
        in_specs=[pl.BlockSpec((None, tm, d), lambda i, j: (i, j, 0)),
                  pl.BlockSpec((1, d), lambda i, j: (0, 0)),
                  pl.BlockSpec(w_tok.shape, lambda i, j: (0, 0)),
                  pl.BlockSpec(w_fm_t.shape, lambda i, j: (0, 0)),
                  pl.BlockSpec(w_cmp.shape, lambda i, j: (0, 0))],
        out_specs=[pl.BlockSpec((None, tm, TOK_W), lambda i, j: (i, j, 0)),
                   pl.BlockSpec((None, FM_W, tm), lambda i, j: (i, 0, j)),
                   pl.BlockSpec((None, ncmp, tm, HEAD_DIM), lambda i, j: (i, 0, j, 0))],
        out_shape=[jax.ShapeDtypeStruct((b, s, TOK_W), BF16),
                   jax.ShapeDtypeStruct((b, FM_W, s), BF16),
                   jax.ShapeDtypeStruct((b, ncmp, s, HEAD_DIM), BF16)],
        compiler_params=_params(2),
        name="in_proj",
    )(x, g.reshape(1, d), w_tok, w_fm_t, w_cmp)


def _norm_matmul_kernel(x_ref, g_ref, w_ref, o_ref):
    o_ref[...] = _dot(_rms(x_ref[...], g_ref[...]).astype(BF16), w_ref[...]).astype(o_ref.dtype)


def _norm_matmul(x2d, g, w, tm, name):
    m, d = x2d.shape
    n = w.shape[1]
    return pl.pallas_call(
        _norm_matmul_kernel,
        grid=(m // tm,),
        in_specs=[pl.BlockSpec((tm, d), lambda i: (i, 0)),
                  pl.BlockSpec((1, d), lambda i: (0, 0)),
                  pl.BlockSpec((d, n), lambda i: (0, 0))],
        out_specs=pl.BlockSpec((tm, n), lambda i: (i, 0)),
        out_shape=jax.ShapeDtypeStruct((m, n), BF16),
        compiler_params=_params(1),
        name=name,
    )(x2d, g.reshape(1, d), w)


def _kmean_kernel(k_ref, o_ref):
    nb = o_ref.shape[0]
    for n in range(nb):
        blk = k_ref[n * MOBA_BLOCK:(n + 1) * MOBA_BLOCK, :].astype(F32)
        o_ref[n:n + 1, :] = jnp.mean(blk, axis=0, keepdims=True)


def _moba_kmean(tok):
    b, s, _ = tok.shape
    nb = s // MOBA_BLOCK
    return pl.pallas_call(
        _kmean_kernel,
        grid=(b,),
        in_specs=[pl.BlockSpec((None, s, MOBA_W), lambda i: (i, 0, COL_MK // MOBA_W))],
        out_specs=pl.BlockSpec((None, nb, MOBA_W), lambda i: (i, 0, 0)),
        out_shape=jax.ShapeDtypeStruct((b, nb, MOBA_W), F32),
        compiler_params=_params(1),
        name="moba_kmean",
    )(tok)


def _fill_v_with_ones(vt_ref, lo_ref, hi_ref, row0=0, cols=1024):
    s = vt_ref.shape[1]
    ones = jnp.ones((ONES_ROWS, cols), BF16)

    def body(c, carry):
        c0 = pl.multiple_of(c * cols, cols)
        lo_ref[0:HEAD_DIM, pl.ds(c0, cols)] = vt_ref[row0:row0 + HEAD_DIM, pl.ds(c0, cols)]
        lo_ref[HEAD_DIM:VROWS, pl.ds(c0, cols)] = ones
        hi_ref[0:HEAD_DIM, pl.ds(c0, cols)] = vt_ref[row0 + HEAD_DIM:row0 + 2 * HEAD_DIM, pl.ds(c0, cols)]
        hi_ref[HEAD_DIM:VROWS, pl.ds(c0, cols)] = ones
        return carry

    lax.fori_loop(0, s // cols, body, 0)


def _split_heads_t(q_t):
    row = lax.broadcasted_iota(jnp.int32, q_t.shape, 0)
    zero = jnp.zeros_like(q_t)
    return jnp.where(row < HEAD_DIM, q_t, zero), jnp.where(row >= HEAD_DIM, q_t, zero)


def _fill_k_with_block_ids(k_ref, lo_ref, hi_ref, block, nblock, lane0=0, rows=512):
    s = k_ref.shape[0]
    lane = lax.broadcasted_iota(jnp.int32, (rows, LANES), 1)
    shift = int(math.log2(block))

    def body(c, carry):
        r0 = pl.multiple_of(c * rows, rows)
        key = r0 + lax.broadcasted_iota(jnp.int32, (rows, LANES), 0)
        bid = lax.shift_right_logical(key, shift) & (nblock - 1)
        k = k_ref[pl.ds(r0, rows), lane0:lane0 + LANES]
        lo_ref[pl.ds(r0, rows), :] = jnp.where(lane < HEAD_DIM, k, (lane - HEAD_DIM == bid).astype(BF16))
        hi_ref[pl.ds(r0, rows), :] = jnp.where(lane >= HEAD_DIM, k, (lane == bid).astype(BF16))
        return carry

    lax.fori_loop(0, s // rows, body, 0)


def _q_with_mask_rows(q_rows, mask_rows, low_head):
    r = lax.broadcasted_iota(jnp.int32, (ONES_ROWS, TQ), 0)
    m = jnp.zeros((ONES_ROWS, TQ), F32)
    for b, row in enumerate(mask_rows):
        m = jnp.where(r == b, row, m)
    pad = jnp.zeros((HEAD_DIM - ONES_ROWS, TQ), BF16)
    parts = [q_rows, m.astype(BF16), pad] if low_head else [m.astype(BF16), pad, q_rows]
    return jnp.concatenate(parts, axis=0)


def _pipe(bufs, m_ref, acc_ref, nheads, a=None, b=None, c=None, c_last=False):
    s_refs, p_refs, mx_ref, al_ref = bufs
    rows = s_refs[0].shape[1]

    def stage_c(h):
        c_slot, c_base, vt_fn = c
        acc_ref[c_base + h] = (al_ref[c_slot, h] * acc_ref[c_base + h]
                               + _dot(vt_fn(h), p_refs[c_slot][h]))

    for h in range(nheads):
        if c is not None and not c_last:
            stage_c(h)
        if a is not None:
            a_slot, score_fn, bias_fn = a
            raw = score_fn(h)
            mx = None
        if b is not None:
            b_slot, b_base = b
            m_prev = m_ref[b_base + h]
            m_next = jnp.maximum(m_prev, mx_ref[b_slot, h])
            al_ref[b_slot, h] = jnp.exp2(m_prev - m_next)
            m_ref[b_base + h] = m_next
        for i in range(rows // STRIP):
            r = slice(i * STRIP, (i + 1) * STRIP)
            if b is not None:
                p_refs[b_slot][h, r, :] = jnp.exp2(s_refs[b_slot][h, r, :] - m_next).astype(BF16)
            if a is not None:
                s = raw[r, :] if bias_fn is None else raw[r, :] + bias_fn(h, i)
                s_refs[a_slot][h, r, :] = s
                cmax = jnp.max(s, axis=0, keepdims=True)
                mx = cmax if mx is None else jnp.maximum(mx, cmax)
        if a is not None:
            mx_ref[a_slot, h] = mx
        if c is not None and c_last:
            stage_c(h)


def _reset_state(bufs, m_ref, acc_ref):
    _, p_refs, _, al_ref = bufs
    m_ref[...] = jnp.full(m_ref.shape, NEG_BIG, F32)
    acc_ref[...] = jnp.zeros(acc_ref.shape, F32)
    p_refs[1][...] = jnp.zeros(p_refs[1].shape, BF16)
    al_ref[...] = jnp.ones(al_ref.shape, F32)


def _pair_output(acc_lo, acc_hi):
    o_lo = acc_lo[0:HEAD_DIM, :] / acc_lo[HEAD_DIM:HEAD_DIM + 1, :]
    o_hi = acc_hi[0:HEAD_DIM, :] / acc_hi[HEAD_DIM:HEAD_DIM + 1, :]
    return jnp.transpose(jnp.concatenate([o_lo, o_hi], axis=0))


def _top_k_rows(val, n_f, k, sentinel):
    chosen = jnp.zeros(val.shape, F32)
    for _ in range(k):
        mx = jnp.max(val, axis=0, keepdims=True)
        idx = jnp.min(jnp.where(val == mx, n_f, sentinel), axis=0, keepdims=True)
        pick = n_f == idx
        chosen = jnp.where(pick, 1.0, chosen)
        val = jnp.where(pick, -jnp.inf, val)
    return chosen


def _moba_kernel(qt_ref, k_ref, vt_ref, km_ref, bt_ref, o_ref, *scratch):
    npair = qt_ref.shape[0] // LANES
    nh = 2 * npair
    k_refs, v_refs = scratch[0:nh], scratch[nh:2 * nh]
    mb_ref, qs_ref, m_ref, acc_ref, s0_ref, s1_ref, p0_ref, p1_ref, mx_ref, al_ref = scratch[2 * nh:]
    cur = pl.program_id(2)
    bufs = ((s0_ref, s1_ref), (p0_ref, p1_ref), mx_ref, al_ref)
    nb = km_ref.shape[0]
    sk = s0_ref.shape[1]
    per_block = KT // STRIP

    @pl.when(cur == 0)
    def _():
        for pp in range(npair):
            _fill_v_with_ones(vt_ref, v_refs[2 * pp], v_refs[2 * pp + 1], row0=pp * LANES)
            _fill_k_with_block_ids(k_ref, k_refs[2 * pp], k_refs[2 * pp + 1], MOBA_BLOCK, 2, lane0=pp * LANES)

    n_f = lax.broadcasted_iota(jnp.int32, (nb, TQ), 0).astype(F32)
    past = n_f < cur.astype(F32)
    for pp in range(npair):
        km = km_ref[:, pp * LANES:(pp + 1) * LANES].astype(BF16)
        for hh, qh in enumerate(_split_heads_t(qt_ref[pp * LANES:(pp + 1) * LANES, :])):
            gate = jnp.where(past, _dot(km, qh), -jnp.inf)
            chosen = _top_k_rows(gate, n_f, min(MOBA_TOPK, nb), float(nb))
            mb_ref[2 * pp + hh] = jnp.where((chosen > 0.5) & past, 0.0, NEG_BIG)
            qs_ref[2 * pp + hh] = qh
    _reset_state(bufs, m_ref, acc_ref)

    def first_block(u):
        return jnp.maximum(cur - 2 * u - 1, 0)

    def score_fn(u, first_rows, second_rows):
        idx = first_block(u)
        ks = pl.multiple_of(idx * KT, KT)
        first_even = (idx & 1) == 0

        def f(h):
            low = h % 2 == 0
            even = jnp.where(first_even, first_rows[h], second_rows[h])
            odd = jnp.where(first_even, second_rows[h], first_rows[h])
            q_rows = qs_ref[h, 0:HEAD_DIM, :] if low else qs_ref[h, HEAD_DIM:2 * HEAD_DIM, :]
            return _dot(k_refs[h][pl.ds(ks, sk), :], _q_with_mask_rows(q_rows, (even, odd), low))
        return f

    def vt_fn(u):
        ks = pl.multiple_of(first_block(jnp.maximum(u, 0)) * KT, KT)
        return lambda h: v_refs[h][:, pl.ds(ks, sk)]

    has_prev = cur >= 1
    kind_lo = jnp.where(has_prev, BT_PREV, BT_OWN)
    kind_hi = jnp.where(has_prev, BT_OWN, BT_MASKED)
    keep = jnp.where(has_prev, 1.0, 0.0)
    prev_rows = [mb_ref[h, pl.ds(jnp.maximum(cur - 1, 0), 1), :] * keep for h in range(nh)]
    no_rows = [jnp.zeros((1, TQ), F32)] * nh

    def near_bias(h, i):
        half, part = divmod(i, per_block)
        return bt_ref[h, kind_lo if half == 0 else kind_hi, part * STRIP:(part + 1) * STRIP, :]

    def far_scores(u):
        b_lo = cur - 2 * u - 1
        idx = jnp.maximum(b_lo, 0)
        pen_lo = jnp.where(b_lo >= -1, 0.0, NEG_BIG)
        pen_hi = jnp.where(b_lo >= 0, 0.0, NEG_BIG)
        return score_fn(u, [mb_ref[h, pl.ds(idx, 1), :] + pen_lo for h in range(nh)],
                        [mb_ref[h, pl.ds(idx + 1, 1), :] + pen_hi for h in range(nh)])

    pipe = functools.partial(_pipe, bufs, m_ref, acc_ref, nh)
    pipe(a=(0, score_fn(0, prev_rows, no_rows), near_bias))
    n_far = lax.shift_right_logical(cur, 1)

    def two_jobs(t, carry):
        u0 = 2 * t + 1
        pipe(a=(1, far_scores(u0), None), b=(0, 0), c=(1, 0, vt_fn(u0 - 2)))
        pipe(a=(0, far_scores(u0 + 1), None), b=(1, 0), c=(0, 0, vt_fn(u0 - 1)))
        return carry

    n_iter = lax.shift_right_logical(n_far + 1, 1)
    lax.fori_loop(0, n_iter, two_jobs, 0)
    pipe(b=(0, 0), c=(1, 0, vt_fn(2 * n_iter - 1)))
    pipe(c=(0, 0, vt_fn(2 * n_iter)))
    for pp in range(npair):
        o_ref[:, pp * LANES:(pp + 1) * LANES] = _pair_output(
            acc_ref[2 * pp], acc_ref[2 * pp + 1]).astype(o_ref.dtype)


def _moba_attention(tok, fm, kmean, bias_tiles, pairs_per_step=2):
    b, s, _ = tok.shape
    nb = s // MOBA_BLOCK
    nh = 2 * pairs_per_step
    w = pairs_per_step * LANES
    sk = 2 * KT
    return pl.pallas_call(
        _moba_kernel,
        grid=(b, MOBA_W // w, s // TQ),
        in_specs=[
            pl.BlockSpec((None, w, TQ), lambda i, p, c: (i, ROW_MQ // w + p, c)),
            pl.BlockSpec((None, s, w), lambda i, p, c: (i, 0, COL_MK // w + p)),
            pl.BlockSpec((None, w, s), lambda i, p, c: (i, ROW_MV // w + p, 0)),
            pl.BlockSpec((None, nb, w), lambda i, p, c: (i, 0, p)),
            pl.BlockSpec((nh, N_BIAS_TILES, KT, TQ), lambda i, p, c: (p, 0, 0, 0)),
        ],
        out_specs=pl.BlockSpec((None, TQ, w), lambda i, p, c: (i, c, p)),
        out_shape=jax.ShapeDtypeStruct((b, s, MOBA_W), BF16),
        scratch_shapes=[pltpu.VMEM((s, LANES), BF16)] * nh + [pltpu.VMEM((VROWS, s), BF16)] * nh + [
            pltpu.VMEM((nh, nb, TQ), F32),
            pltpu.VMEM((nh, LANES, TQ), BF16),
            pltpu.VMEM((nh, 1, TQ), F32),
            pltpu.VMEM((nh, VROWS, TQ), F32),
            pltpu.VMEM((nh, sk, TQ), F32), pltpu.VMEM((nh, sk, TQ), F32),
            pltpu.VMEM((nh, sk, TQ), BF16), pltpu.VMEM((nh, sk, TQ), BF16),
            pltpu.VMEM((2, nh, 1, TQ), F32), pltpu.VMEM((2, nh, 1, TQ), F32),
        ],
        compiler_params=_params(3),
        name="moba_attention",
    )(fm, tok, fm, kmean, bias_tiles)


def _compress_kernel(ck_ref, cv_ref, pk_ref, pv_ref, w1k_ref, w1v_ref, w2k_ref, w2v_ref,
                     kc_ref, vct_ref):
    nch = ck_ref.shape[1]
    half = CMP_STRIDE * HEAD_DIM

    def run(c_ref, pos_ref, w1_ref, w2_ref):
        acc = jnp.zeros((nch, LANES), F32)
        for g in range(NSA_GROUPS):
            c = c_ref[g].astype(F32)
            lo = (c + pos_ref[0:1, :]).astype(BF16)
            hi = (c + pos_ref[1:2, :]).astype(BF16)
            pre = _dot(lo, w1_ref[:half, :]) + pltpu.roll(_dot(hi, w1_ref[half:, :]), nch - 1, axis=0)
            acc = acc + _dot(jax.nn.gelu(pre).astype(BF16), w2_ref[g])
        return acc

    kc_ref[...] = run(ck_ref, pk_ref, w1k_ref, w2k_ref).astype(kc_ref.dtype)
    vct_ref[...] = jnp.transpose(run(cv_ref, pv_ref, w1v_ref, w2v_ref)).astype(vct_ref.dtype)


def _compress(chunks, pos_k, pos_v, w1k, w1v, w2k, w2v):
    b, _, nch, width = chunks.shape
    g = NSA_GROUPS
    full = lambda shape: pl.BlockSpec(shape, lambda i: (0,) * len(shape))
    return pl.pallas_call(
        _compress_kernel,
        grid=(b,),
        in_specs=[pl.BlockSpec((None, g, nch, width), lambda i: (i, 0, 0, 0)),
                  pl.BlockSpec((None, g, nch, width), lambda i: (i, 1, 0, 0)),
                  full(pos_k.shape), full(pos_v.shape), full(w1k.shape), full(w1v.shape),
                  full(w2k.shape), full(w2v.shape)],
        out_specs=[pl.BlockSpec((None, nch, LANES), lambda i: (i, 0, 0)),
                   pl.BlockSpec((None, LANES, nch), lambda i: (i, 0, 0))],
        out_shape=[jax.ShapeDtypeStruct((b, nch, LANES), BF16),
                   jax.ShapeDtypeStruct((b, LANES, nch), BF16)],
        compiler_params=_params(1),
        name="nsa_compress",
    )(chunks, chunks, pos_k, pos_v, w1k, w1v, w2k, w2v)


def _cmp_select_kernel(qt_ref, kc_ref, vct_ref, mimp_ref, o_ref, sel_ref, *, nsb, nq):
    ci = pl.program_id(1)
    nch = kc_ref.shape[0]
    nsbp = mimp_ref.shape[0]
    parts = 4 if (nch // 4) % LANES == 0 and (nsbp // 4) % 8 == 0 else 1
    for v in range(1, parts + 1):
        @pl.when((ci * parts >= (v - 1) * nq) & (ci * parts < v * nq))
        def _(v=v):
            _cmp_select_body(qt_ref, kc_ref, vct_ref, mimp_ref, o_ref, sel_ref,
                             nsb=nsb, nch=nch * v // parts, nrow=nsbp * v // parts)


def _cmp_select_body(qt_ref, kc_ref, vct_ref, mimp_ref, o_ref, sel_ref, *, nsb, nch, nrow):
    qs = pl.program_id(1) * TQ
    n_row = lax.broadcasted_iota(jnp.int32, (nch, TQ), 0)
    t_col = qs + lax.broadcasted_iota(jnp.int32, (nch, TQ), 1)
    hidden = jnp.where((n_row * CMP_STRIDE + (CMP_BLOCK - 1)) <= t_col, 0.0, NEG_BIG)
    any_visible = t_col[0:1, :] >= (CMP_BLOCK - 1)
    kc = kc_ref[0:nch, :]
    vct = vct_ref[:, 0:nch]
    vrow = lax.broadcasted_iota(jnp.int32, vct.shape, 0)
    mimp = mimp_ref[0:nrow, 0:nch]

    j_row = lax.broadcasted_iota(jnp.int32, (nrow, TQ), 0)
    t_sel = qs + lax.broadcasted_iota(jnp.int32, (nrow, TQ), 1)
    cur = lax.shift_right_logical(t_sel, int(math.log2(SLC_BLOCK)))
    j_f = j_row.astype(F32)
    forced = (j_row == 0) | (j_row == cur) | (j_row == cur - 1)
    allowed = (j_row <= cur) & (j_row < nsb)

    q_heads = [_split_heads_t(qt_ref[r * LANES:(r + 1) * LANES, :]) for r in range(NSA_HPG)]
    out_t = [jnp.zeros((LANES, TQ), F32) for _ in range(NSA_HPG)]
    for g in range(NSA_GROUPS):
        vin_g = (vrow < HEAD_DIM) if g == 0 else (vrow >= HEAD_DIM)
        vct_g = jnp.where(vin_g, vct, jnp.zeros_like(vct))
        p_grp = jnp.zeros((nch, TQ), F32)
        for r in range(NSA_HPG):
            s = _dot(kc, q_heads[r][g]) + hidden
            e = jnp.exp2(s - jnp.max(s, axis=0, keepdims=True))
            p = e * jnp.where(any_visible, 1.0 / jnp.sum(e, axis=0, keepdims=True), 0.0)
            p_grp = p_grp + p
            out_t[r] = out_t[r] + _dot(vct_g, p.astype(BF16))
        p_hi = p_grp.astype(BF16)
        p_lo = (p_grp - p_hi.astype(F32)).astype(BF16)
        imp = _dot(mimp, p_hi) + _dot(mimp, p_lo)
        val = jnp.where(forced & allowed, jnp.inf, jnp.where(allowed, imp, -jnp.inf))
        chosen = _top_k_rows(val, j_f, min(SLC_TOPN, nsb), float(nrow))
        sel_ref[g, 0:nrow, :] = jnp.where(allowed & (chosen > 0.5), 0.0, NEG_BIG)
        if nrow < sel_ref.shape[1]:
            sel_ref[g, nrow:, :] = jnp.full((sel_ref.shape[1] - nrow, TQ), NEG_BIG, F32)
    for r in range(NSA_HPG):
        o_ref[:, r * LANES:(r + 1) * LANES] = jnp.transpose(out_t[r]).astype(o_ref.dtype)


def _cmp_select(fm, kc, vct, mimp_t, nsb):
    b, _, s = fm.shape
    nch = kc.shape[1]
    nsbp = mimp_t.shape[0]
    return pl.pallas_call(
        functools.partial(_cmp_select_kernel, nsb=nsb, nq=s // TQ),
        grid=(b, s // TQ),
        in_specs=[pl.BlockSpec((None, NSA_W, TQ), lambda i, c: (i, ROW_NQ // NSA_W, c)),
                  pl.BlockSpec((None, nch, LANES), lambda i, c: (i, 0, 0)),
                  pl.BlockSpec((None, LANES, nch), lambda i, c: (i, 0, 0)),
                  pl.BlockSpec((nsbp, nch), lambda i, c: (0, 0))],
        out_specs=[pl.BlockSpec((None, TQ, NSA_W), lambda i, c: (i, c, 0)),
                   pl.BlockSpec((None, NSA_GROUPS, nsbp, TQ), lambda i, c: (i, 0, 0, c))],
        out_shape=[jax.ShapeDtypeStruct((b, s, NSA_W), BF16),
                   jax.ShapeDtypeStruct((b, NSA_GROUPS, nsbp, s), F32)],
        compiler_params=_params(2),
        name="nsa_cmp_select",
    )(fm, kc, vct, mimp_t)


def _slc_win_kernel(qt_ref, ks_ref, vst_ref, kw_ref, vwt_ref, sel_ref, bt_ref,
                    oslc_ref, owin_ref,
                    ks0_ref, ks1_ref, vs0_ref, vs1_ref, vw0_ref, vw1_ref, qs_ref, m_ref, acc_ref,
                    s0_ref, s1_ref, p0_ref, p1_ref, mx_ref, al_ref):
    cur = pl.program_id(1)
    bufs = ((s0_ref, s1_ref), (p0_ref, p1_ref), mx_ref, al_ref)
    per_tile = KT // SLC_BLOCK

    @pl.when(cur == 0)
    def _():
        _fill_v_with_ones(vst_ref, vs0_ref, vs1_ref)
        _fill_v_with_ones(vwt_ref, vw0_ref, vw1_ref)
        _fill_k_with_block_ids(ks_ref, ks0_ref, ks1_ref, SLC_BLOCK, per_tile)

    nh = NSA_HEADS
    for r in range(NSA_HPG):
        lo, hi = _split_heads_t(qt_ref[r * LANES:(r + 1) * LANES, :])
        qs_ref[r] = lo
        qs_ref[NSA_HPG + r] = hi
    _reset_state(bufs, m_ref, acc_ref)
    ks_refs = (ks0_ref, ks1_ref)
    vs_refs = (vs0_ref, vs1_ref)
    vw_refs = (vw0_ref, vw1_ref)
    sel_base, win_base = 0, nh

    def tile_start(n):
        return pl.multiple_of(jnp.maximum(n, 0) * KT, KT)

    def win_scores(n):
        ks = tile_start(n)
        return lambda h: _dot(kw_ref[pl.ds(ks, KT), :], qs_ref[h])

    def sel_scores(n, penalty):
        ks = tile_start(n)
        base = jnp.maximum(n, 0) * per_tile
        rows = [[sel_ref[g, pl.ds(base + b, 1), :] + penalty for b in range(per_tile)]
                for g in range(NSA_GROUPS)]

        def f(h):
            g = h // NSA_HPG
            q_rows = qs_ref[h, 0:HEAD_DIM, :] if g == 0 else qs_ref[h, HEAD_DIM:2 * HEAD_DIM, :]
            return _dot(ks_refs[g][pl.ds(ks, KT), :], _q_with_mask_rows(q_rows, rows[g], g == 0))
        return f

    def vt_fn(v_refs, n):
        ks = tile_start(n)
        return lambda h: v_refs[h // NSA_HPG][:, pl.ds(ks, KT)]

    def near_bias(kind):
        return lambda h, i: bt_ref[h, kind, i * STRIP:(i + 1) * STRIP, :]

    pipe = functools.partial(_pipe, bufs, m_ref, acc_ref, nh, c_last=True)
    kind_prev = jnp.where(cur >= 1, BT_PREV, BT_MASKED)
    kind_upper = jnp.where(cur >= 2, BT_UPPER, BT_MASKED)
    pipe(a=(0, win_scores(cur), near_bias(BT_OWN)))
    pipe(a=(1, win_scores(cur - 1), near_bias(kind_prev)), b=(0, win_base))
    pipe(a=(0, win_scores(cur - 2), near_bias(kind_upper)), b=(1, win_base),
         c=(0, win_base, vt_fn(vw_refs, cur)))
    pipe(a=(1, sel_scores(cur, 0.0), near_bias(BT_OWN)), b=(0, win_base),
         c=(1, win_base, vt_fn(vw_refs, cur - 1)))
    pipe(a=(0, sel_scores(cur - 1, 0.0), near_bias(kind_prev)), b=(1, sel_base),
         c=(0, win_base, vt_fn(vw_refs, cur - 2)))

    def far_scores(n):
        return sel_scores(n, jnp.where(n >= 0, 0.0, NEG_BIG))

    def two_tiles(t, carry):
        n0 = cur - 2 - 2 * t
        pipe(a=(1, far_scores(n0), None), b=(0, sel_base), c=(1, sel_base, vt_fn(vs_refs, n0 + 2)))
        pipe(a=(0, far_scores(n0 - 1), None), b=(1, sel_base), c=(0, sel_base, vt_fn(vs_refs, n0 + 1)))
        return carry

    n_iter = lax.shift_right_logical(jnp.maximum(cur - 1, 0) + 1, 1)
    lax.fori_loop(0, n_iter, two_tiles, 0)
    pipe(b=(0, sel_base), c=(1, sel_base, vt_fn(vs_refs, cur - 2 * n_iter)))
    pipe(c=(0, sel_base, vt_fn(vs_refs, cur - 1 - 2 * n_iter)))

    for r in range(NSA_HPG):
        cols = slice(r * LANES, (r + 1) * LANES)
        oslc_ref[:, cols] = _pair_output(acc_ref[r], acc_ref[NSA_HPG + r]).astype(oslc_ref.dtype)
        owin_ref[:, cols] = _pair_output(acc_ref[nh + r], acc_ref[nh + NSA_HPG + r]).astype(owin_ref.dtype)


def _slc_win(tok, fm, sel, bias_tiles):
    b, s, _ = tok.shape
    nsbp = sel.shape[2]
    return pl.pallas_call(
        _slc_win_kernel,
        grid=(b, s // TQ),
        in_specs=[pl.BlockSpec((None, NSA_W, TQ), lambda i, c: (i, ROW_NQ // NSA_W, c)),
                  pl.BlockSpec((None, s, LANES), lambda i, c: (i, 0, COL_NKS // LANES)),
                  pl.BlockSpec((None, LANES, s), lambda i, c: (i, ROW_NVS // LANES, 0)),
                  pl.BlockSpec((None, s, LANES), lambda i, c: (i, 0, COL_NKW // LANES)),
                  pl.BlockSpec((None, LANES, s), lambda i, c: (i, ROW_NVW // LANES, 0)),
                  pl.BlockSpec((None, NSA_GROUPS, nsbp, TQ), lambda i, c: (i, 0, 0, c)),
                  pl.BlockSpec((NSA_HEADS, N_BIAS_TILES, KT, TQ), lambda i, c: (1, 0, 0, 0))],
        out_specs=[pl.BlockSpec((None, TQ, NSA_W), lambda i, c: (i, c, 0)),
                   pl.BlockSpec((None, TQ, NSA_W), lambda i, c: (i, c, 0))],
        out_shape=[jax.ShapeDtypeStruct((b, s, NSA_W), BF16),
                   jax.ShapeDtypeStruct((b, s, NSA_W), BF16)],
        scratch_shapes=[pltpu.VMEM((s, LANES), BF16)] * 2 + [pltpu.VMEM((VROWS, s), BF16)] * 4 + [
            pltpu.VMEM((NSA_HEADS, LANES, TQ), BF16),
            pltpu.VMEM((2 * NSA_HEADS, 1, TQ), F32),
            pltpu.VMEM((2 * NSA_HEADS, VROWS, TQ), F32),
            pltpu.VMEM((NSA_HEADS, KT, TQ), F32), pltpu.VMEM((NSA_HEADS, KT, TQ), F32),
            pltpu.VMEM((NSA_HEADS, KT, TQ), BF16), pltpu.VMEM((NSA_HEADS, KT, TQ), BF16),
            pltpu.VMEM((2, NSA_HEADS, 1, TQ), F32), pltpu.VMEM((2, NSA_HEADS, 1, TQ), F32)],
        compiler_params=_params(2),
        name="nsa_slc_win",
    )(fm, tok, fm, tok, fm, sel, bias_tiles)


def _merge_kernel(x_ref, oa_ref, oc_ref, os_ref, ow_ref, ng_ref, ga_ref, gb_ref,
                  eg_ref, wa_ref, wb_ref, wo_ref, o_ref):
    sig = jax.nn.sigmoid(ng_ref[...].astype(F32))
    s_hi = sig.astype(BF16)
    s_lo = (sig - s_hi.astype(F32)).astype(BF16)
    eg = eg_ref[...]
    gexp = _dot(s_hi, eg) + _dot(s_lo, eg)
    o_b = (gexp[:, :NSA_W] * oc_ref[...].astype(F32)
           + gexp[:, NSA_W:2 * NSA_W] * os_ref[...].astype(F32)
           + gexp[:, 2 * NSA_W:] * ow_ref[...].astype(F32))
    a = _dot(oa_ref[...], wa_ref[...])
    bb = _dot(o_b.astype(BF16), wb_ref[...])
    merged = (jax.nn.sigmoid(ga_ref[...].astype(F32)) * a
              + jax.nn.sigmoid(gb_ref[...].astype(F32)) * bb)
    o_ref[...] = x_ref[...] + _dot(merged.astype(BF16), wo_ref[...])


def _merge(x2d, tok2d, o_a, o_c, o_s, o_w, egate, wa, wb, wo, tm):
    m, d = x2d.shape
    row = lambda w, col=0: pl.BlockSpec((tm, w), lambda i, col=col: (i, col))
    full = lambda a: pl.BlockSpec(a.shape, lambda i: (0, 0))
    return pl.pallas_call(
        _merge_kernel,
        grid=(m // tm,),
        in_specs=[row(d), row(MOBA_W), row(NSA_W), row(NSA_W), row(NSA_W),
                  row(GATE_PAD, COL_NGATE // GATE_PAD),
                  row(D_MODEL, COL_GA // D_MODEL), row(D_MODEL, COL_GB // D_MODEL),
                  full(egate), full(wa), full(wb), full(wo)],
        out_specs=row(d),
        out_shape=jax.ShapeDtypeStruct((m, d), F32),
        compiler_params=_params(1),
        name="merge_out",
    )(x2d, o_a, o_c, o_s, o_w, tok2d, tok2d, tok2d, egate, wa, wb, wo)


def _xattn_kernel(x_ref, g_ref, wq_ref, kv_ref, wo_ref, o_ref):
    x = x_ref[...]
    q = _dot(_rms(x, g_ref[...]).astype(BF16), wq_ref[...]).astype(BF16)
    heads = []
    for h in range(XATTN_HEADS):
        cols = slice(h * XATTN_HEAD_DIM, (h + 1) * XATTN_HEAD_DIM)
        k = kv_ref[:, cols]
        v = kv_ref[:, XATTN_W + h * XATTN_HEAD_DIM:XATTN_W + (h + 1) * XATTN_HEAD_DIM]
        s = _dot_nt(q[:, cols], k) * (XATTN_HEAD_DIM ** -0.5)
        e = jnp.exp(s - jnp.max(s, axis=1, keepdims=True))
        p = e / jnp.sum(e, axis=1, keepdims=True)
        heads.append(_dot(p.astype(BF16), v).astype(BF16))
    o_ref[...] = x + _dot(jnp.concatenate(heads, axis=1), wo_ref[...])


def _xattn(x3d, g, wq, memkv, wo, tm):
    b, s, d = x3d.shape
    mem_len = memkv.shape[1]
    return pl.pallas_call(
        _xattn_kernel,
        grid=(b, s // tm),
        in_specs=[pl.BlockSpec((None, tm, d), lambda i, j: (i, j, 0)),
                  pl.BlockSpec((1, d), lambda i, j: (0, 0)),
                  pl.BlockSpec(wq.shape, lambda i, j: (0, 0)),
                  pl.BlockSpec((None, mem_len, 2 * XATTN_W), lambda i, j: (i, 0, 0)),
                  pl.BlockSpec(wo.shape, lambda i, j: (0, 0))],
        out_specs=pl.BlockSpec((None, tm, d), lambda i, j: (i, j, 0)),
        out_shape=jax.ShapeDtypeStruct((b, s, d), F32),
        compiler_params=_params(2),
        name="xattn",
    )(x3d, g.reshape(1, d), wq, memkv, wo)


def _ffn_kernel(x_ref, halo_ref, g_ref, wg_ref, wu_ref, cw_ref, cb_ref, wd_ref, gf_ref, o_ref,
                y_ref, *, chunk, halo_rows):
    j = pl.program_id(1)
    x = x_ref[...]
    tm = x.shape[0]
    g = g_ref[...]
    hf = _rms(x, g).astype(BF16)
    keep = jnp.where(j > 0, 1.0, 0.0)
    halo = (_rms(halo_ref[...], g) * keep).astype(BF16)
    ext = jnp.concatenate([halo, hf], axis=0)
    row = lax.broadcasted_iota(jnp.int32, (tm, chunk), 0)
    starts = list(range(0, D_FF, chunk))

    def gate_up(c0):
        cols = slice(c0, c0 + chunk)
        return _dot(ext, wg_ref[:, cols]), _dot(hf, wu_ref[:, cols])

    nxt = gate_up(starts[0])
    for k, c0 in enumerate(starts):
        cols = slice(c0, c0 + chunk)
        ue, up = nxt
        if k + 1 < len(starts):
            nxt = gate_up(starts[k + 1])
        u = ue[halo_rows:, :]
        prev1 = ue[halo_rows - 1:halo_rows, :]
        prev2 = ue[halo_rows - 2:halo_rows - 1, :]
        u1 = jnp.where(row == 0, prev1, pltpu.roll(u, 1, axis=0))
        u2 = jnp.where(row == 0, prev2, jnp.where(row == 1, prev1, pltpu.roll(u, 2, axis=0)))
        a = cw_ref[0:1, cols] * u2 + cw_ref[1:2, cols] * u1 + cw_ref[2:3, cols] * u + cb_ref[:, cols]
        y_ref[:, cols] = (jax.nn.gelu(a) * up).astype(BF16)
    o_ref[...] = _rms(x + _dot(y_ref[...], wd_ref[...]), gf_ref[...])


def _ffn(x3d, g, wg, wu, cw, cb, wd, gf, tm):
    b, s, d = x3d.shape
    halo_rows = 16
    per = tm // halo_rows
    return pl.pallas_call(
        functools.partial(_ffn_kernel, chunk=256, halo_rows=halo_rows),
        grid=(b, s // tm),
        in_specs=[pl.BlockSpec((None, tm, d), lambda i, j: (i, j, 0)),
                  pl.BlockSpec((None, halo_rows, d), lambda i, j: (i, jnp.maximum(j * per - 1, 0), 0)),
                  pl.BlockSpec((1, d), lambda i, j: (0, 0)),
                  pl.BlockSpec(wg.shape, lambda i, j: (0, 0)),
                  pl.BlockSpec(wu.shape, lambda i, j: (0, 0)),
                  pl.BlockSpec(cw.shape, lambda i, j: (0, 0)),
                  pl.BlockSpec((1, D_FF), lambda i, j: (0, 0)),
                  pl.BlockSpec(wd.shape, lambda i, j: (0, 0)),
                  pl.BlockSpec((1, d), lambda i, j: (0, 0))],
        out_specs=pl.BlockSpec((None, tm, d), lambda i, j: (i, j, 0)),
        out_shape=jax.ShapeDtypeStruct((b, s, d), F32),
        scratch_shapes=[pltpu.VMEM((tm, D_FF), BF16)],
        compiler_params=_params(2),
        name="conv_ffn",
    )(x3d, x3d, g.reshape(1, d), wg, wu, cw, cb.reshape(1, D_FF), wd, gf.reshape(1, d))


def _nq_perm():
    idx = []
    for r in range(NSA_HPG):
        for g in range(NSA_GROUPS):
            h = g * NSA_HPG + r
            idx.extend(range(h * HEAD_DIM, (h + 1) * HEAD_DIM))
    return np.asarray(idx, np.int32)


def _gate_expand():
    e = np.zeros((GATE_PAD, 3 * NSA_W), np.float32)
    for br in range(3):
        for g in range(NSA_GROUPS):
            for r in range(NSA_HPG):
                c0 = br * NSA_W + r * LANES + g * HEAD_DIM
                e[br * NSA_HEADS + g * NSA_HPG + r, c0:c0 + HEAD_DIM] = 1.0
    return e


def _importance_matrix(nch, nsb, nsbp):
    ratio = SLC_BLOCK // CMP_STRIDE
    nc = nch - 1
    m = np.zeros((nsbp, nch), np.float32)
    for j in range(nsb):
        for a in range(ratio):
            for bb in range(CMP_BLOCK // CMP_STRIDE):
                c = j * ratio + a - bb
                if 0 <= c < nc:
                    m[j, c] += 1.0
    return m


def _split_w_in(w_in):
    widths = [MOBA_W] * 3 + [NSA_W] + [NSA_KV_W] * 6 + [3 * NSA_HEADS, D_MODEL, D_MODEL]
    cuts = np.cumsum(widths)[:-1]
    mq, mk, mv, nq, nkc, nvc, nks, nvs, nkw, nvw, ngate, ga, gb = jnp.split(w_in, cuts, axis=1)
    ngate = jnp.pad(ngate, ((0, 0), (0, GATE_PAD - 3 * NSA_HEADS)))
    q_scale = HEAD_DIM ** -0.5 * LOG2E
    mq, nq = mq * q_scale, nq * q_scale
    w_tok = jnp.concatenate([ga, gb, mk, nks, nkw, ngate], axis=1)
    w_fm = jnp.concatenate([mq, mv, nq[:, _nq_perm()], nvs, nvw], axis=1)
    w_cmp = jnp.concatenate([nkc, nvc], axis=1)
    return w_tok.astype(BF16), jnp.transpose(w_fm).astype(BF16), w_cmp.astype(BF16)


def _layer(x, mem, bias_tiles, norm_mix_g, w_in, cmp_pos_k, cmp_w1_k, cmp_w2_k, cmp_pos_v,
           cmp_w1_v, cmp_w2_v, w_branch_a, w_branch_b, w_out, norm_xattn_g, norm_mem_g, w_xq, w_xkv, w_xo,
           norm_ffn_g, w_gate, w_up, conv_w, conv_b, w_down, norm_out_g):
    b, s, d = x.shape
    assert s % 1024 == 0 and d == D_MODEL
    rows = b * s
    nch = s // CMP_STRIDE
    nsb = s // SLC_BLOCK
    nsbp = max(LANES, nsb)

    w_tok, w_fm_t, w_cmp = _split_w_in(w_in)
    tok, fm, cmp_kv = _in_proj(x, norm_mix_g, w_tok, w_fm_t, w_cmp, tm=512)
    tok2d = tok.reshape(rows, TOK_W)

    kmean = _moba_kmean(tok)
    o_a = _moba_attention(tok, fm, kmean, bias_tiles)

    def w2_pad(w2):
        z = jnp.zeros_like(w2)
        return jnp.stack([jnp.concatenate([w2, z], axis=1), jnp.concatenate([z, w2], axis=1)]).astype(BF16)

    kc, vct = _compress(cmp_kv.reshape(b, CMP_W // HEAD_DIM, nch, CMP_STRIDE * HEAD_DIM),
                        cmp_pos_k.reshape(2, -1), cmp_pos_v.reshape(2, -1),
                        cmp_w1_k.astype(BF16), cmp_w1_v.astype(BF16), w2_pad(cmp_w2_k), w2_pad(cmp_w2_v))
    mimp_t = jnp.asarray(_importance_matrix(nch, nsb, nsbp), BF16)
    o_c, sel = _cmp_select(fm, kc, vct, mimp_t, nsb)
    o_s, o_w = _slc_win(tok, fm, sel, bias_tiles)

    x1 = _merge(x.reshape(rows, d), tok2d, o_a.reshape(rows, MOBA_W), o_c.reshape(rows, NSA_W),
                o_s.reshape(rows, NSA_W), o_w.reshape(rows, NSA_W), jnp.asarray(_gate_expand(), BF16),
                w_branch_a.astype(BF16), w_branch_b[_nq_perm(), :].astype(BF16), w_out.astype(BF16), tm=512)

    mem_len = mem.shape[1]
    memkv = _norm_matmul(mem.reshape(b * mem_len, d), norm_mem_g, w_xkv.astype(BF16), tm=mem_len,
                         name="mem_kv").reshape(b, mem_len, 2 * XATTN_W)
    x2 = _xattn(x1.reshape(b, s, d), norm_xattn_g, w_xq.astype(BF16), memkv, w_xo.astype(BF16), tm=512)

    return _ffn(x2, norm_ffn_g, w_gate.astype(BF16), w_up.astype(BF16), conv_w, conv_b,
                w_down.astype(BF16), norm_out_g, tm=512)


def kernel(x, mem, rel_bias, norm_mix_g, w_in, cmp_pos_k, cmp_w1_k, cmp_w2_k, cmp_pos_v, cmp_w1_v, cmp_w2_v, w_branch_a, w_branch_b, w_out, norm_xattn_g, norm_mem_g, w_xq, w_xkv, w_xo, norm_ffn_g, w_gate, w_up, conv_w, conv_b, w_down, norm_final_g):
    depth = w_in.shape[0]
    assert depth == 1, "the final norm is fused into the last layer's FFN kernel"
    bias_tiles = _bias_tiles(rel_bias)
    l = 0
    return _layer(x, mem, bias_tiles, norm_mix_g[l], w_in[l], cmp_pos_k[l], cmp_w1_k[l],
                  cmp_w2_k[l], cmp_pos_v[l], cmp_w1_v[l], cmp_w2_v[l], w_branch_a[l], w_branch_b[l], w_out[l],
                  norm_xattn_g[l], norm_mem_g[l], w_xq[l], w_xkv[l], w_xo[l],
                  norm_ffn_g[l], w_gate[l], w_up[l], conv_w[l], conv_b[l], w_down[l], norm_final_g)
```

```python
import functools
import math

import numpy as np
import jax
import jax.numpy as jnp
from jax import lax
from jax.experimental import pallas as pl
from jax.experimental.pallas import tpu as pltpu

F32 = jnp.float32
BF16 = jnp.bfloat16

D_MODEL = 1024
HEAD_DIM = 64
MOBA_HEADS = 8
MOBA_BLOCK = 256
MOBA_TOPK = 3
NSA_HEADS = 8
NSA_GROUPS = 2
NSA_HPG = NSA_HEADS // NSA_GROUPS
CMP_BLOCK = 32
CMP_STRIDE = 16
CMP_HIDDEN = 256
SLC_BLOCK = 64
SLC_TOPN = 16
WINDOW = 512
REL_BUCKETS = 32
REL_MAX_DIST = 128
XATTN_HEADS = 4
XATTN_HEAD_DIM = 128
D_FF = 2816
CONV_WIDTH = 3
Q_BLOCK = 128
RMS_EPS = 1e-6
NEG_BIG = -1e30
LOG2E = math.log2(math.e)

LANES = 128
TQ = 256
KT = 256
STRIP = 128
ONES_ROWS = 16
VROWS = HEAD_DIM + ONES_ROWS
VMEM_LIMIT = 56 * 1024 * 1024

MOBA_W = MOBA_HEADS * HEAD_DIM
NSA_W = NSA_HEADS * HEAD_DIM
NSA_KV_W = NSA_GROUPS * HEAD_DIM
XATTN_W = XATTN_HEADS * XATTN_HEAD_DIM
GATE_PAD = LANES

COL_GA = 0
COL_GB = COL_GA + D_MODEL
COL_MK = COL_GB + D_MODEL
COL_NKS = COL_MK + MOBA_W
COL_NKW = COL_NKS + NSA_KV_W
COL_NGATE = COL_NKW + NSA_KV_W
TOK_W = COL_NGATE + GATE_PAD
CMP_W = 2 * NSA_KV_W
ROW_MQ = 0
ROW_MV = ROW_MQ + MOBA_W
ROW_NQ = ROW_MV + MOBA_W
ROW_NVS = ROW_NQ + NSA_W
ROW_NVW = ROW_NVS + NSA_KV_W
FM_W = ROW_NVW + NSA_KV_W

BT_OWN, BT_PREV, BT_UPPER, BT_MASKED = 0, 1, 2, 3
N_BIAS_TILES = 4


def _bucket_thresholds():
    n = np.arange(0, 4 * REL_MAX_DIST)
    exact = REL_BUCKETS // 2
    nf = np.maximum(n, 1).astype(np.float64)
    large = exact + (np.log(nf / exact) / math.log(REL_MAX_DIST / exact) * (REL_BUCKETS - exact)).astype(np.int64)
    large = np.minimum(large, REL_BUCKETS - 1)
    b = np.where(n < exact, n, large)
    return [int(np.argmax(b >= k)) for k in range(REL_BUCKETS)]


_THR = _bucket_thresholds()
assert _THR[REL_BUCKETS - 1] <= LANES


def _dot(a, b):
    return jnp.dot(a, b, preferred_element_type=F32)


def _dot_nt(a, b):
    return lax.dot_general(a, b, (((1,), (1,)), ((), ())), preferred_element_type=F32)


def _rms(x, g):
    return x * lax.rsqrt(jnp.mean(x * x, axis=-1, keepdims=True) + RMS_EPS) * g


def _params(n_axes):
    return pltpu.CompilerParams(dimension_semantics=("arbitrary",) * n_axes,
                                vmem_limit_bytes=VMEM_LIMIT)


def _bias_kernel(rb_ref, out_ref):
    h = pl.program_id(0)
    j = lax.broadcasted_iota(jnp.int32, (KT, TQ), 0)
    i = lax.broadcasted_iota(jnp.int32, (KT, TQ), 1)
    far = rb_ref[REL_BUCKETS - 1, h]

    def table(d):
        t = jnp.full((KT, TQ), (rb_ref[0, h] - far) * LOG2E, F32)
        for k in range(1, REL_BUCKETS):
            t = jnp.where(d >= _THR[k], (rb_ref[k, h] - far) * LOG2E, t)
        return t

    neg = jnp.full((KT, TQ), NEG_BIG, F32)
    out_ref[0, BT_OWN] = jnp.where(i >= j, table(i - j), neg)
    out_ref[0, BT_PREV] = table(i - j + KT)
    out_ref[0, BT_UPPER] = jnp.where(j > i, 0.0, neg)
    out_ref[0, BT_MASKED] = neg


def _bias_tiles(rel_bias):
    nh = rel_bias.shape[1]
    return pl.pallas_call(
        _bias_kernel,
        grid=(nh,),
        in_specs=[pl.BlockSpec(memory_space=pltpu.SMEM)],
        out_specs=pl.BlockSpec((1, N_BIAS_TILES, KT, TQ), lambda h: (h, 0, 0, 0)),
        out_shape=jax.ShapeDtypeStruct((nh, N_BIAS_TILES, KT, TQ), F32),
        compiler_params=_params(1),
        name="bias_tiles",
    )(rel_bias)


def _inproj_kernel(x_ref, g_ref, wt_ref, wf_ref, wc_ref, tok_ref, fm_ref, cmp_ref, *, chunk):
    h = _rms(x_ref[...], g_ref[...]).astype(BF16)
    c = _dot(h, wc_ref[...]).astype(cmp_ref.dtype)
    for i in range(cmp_ref.shape[0]):
        cmp_ref[i] = c[:, i * HEAD_DIM:(i + 1) * HEAD_DIM]
    n = wt_ref.shape[1]
    for c0 in range(0, n, chunk):
        c1 = min(c0 + chunk, n)
        tok_ref[:, c0:c1] = _dot(h, wt_ref[:, c0:c1]).astype(tok_ref.dtype)
    n = wf_ref.shape[0]
    for c0 in range(0, n, chunk):
        c1 = min(c0 + chunk, n)
        fm_ref[c0:c1, :] = _dot_nt(wf_ref[c0:c1, :], h).astype(fm_ref.dtype)


def _in_proj(x, g, w_tok, w_fm_t, w_cmp, tm):
    b, s, d = x.shape
    ncmp = CMP_W // HEAD_DIM
    return pl.pallas_call(
        functools.partial(_inproj_kernel, chunk=512),
        grid=(b, s // tm),
        in_specs=[pl.BlockSpec((None, tm, d), lambda i, j: (i, j, 0)),
                  pl.BlockSpec((1, d), lambda i, j: (0, 0)),
                  pl.BlockSpec(w_tok.shape, lambda i, j: (0, 0)),
                  pl.BlockSpec(w_fm_t.shape, lambda i, j: (0, 0)),
                  pl.BlockSpec(w_cmp.shape, lambda i, j: (0, 0))],
        out_specs=[pl.BlockSpec((None, tm, TOK_W), lambda i, j: (i, j, 0)),
                   pl.BlockSpec((None, FM_W, tm), lambda i, j: (i, 0, j)),
                   pl.BlockSpec((None, ncmp, tm, HEAD_DIM), lambda i, j: (i, 0, j, 0))],
        out_shape=[jax.ShapeDtypeStruct((b, s, TOK_W), BF16),
                   jax.ShapeDtypeStruct((b, FM_W, s), BF16),
                   jax.ShapeDtypeStruct((b, ncmp, s, HEAD_DIM), BF16)],
        compiler_params=_params(2),
        name="in_proj",
    )(x, g.reshape(1, d), w_tok, w_fm_t, w_cmp)


def _norm_matmul_kernel(x_ref, g_ref, w_ref, o_ref):
    o_ref[...] = _dot(_rms(x_ref[...], g_ref[...]).astype(BF16), w_ref[...]).astype(o_ref.dtype)


def _norm_matmul(x2d, g, w, tm, name):
    m, d = x2d.shape
    n = w.shape[1]
    return pl.pallas_call(
        _norm_matmul_kernel,
        grid=(m // tm,),
        in_specs=[pl.BlockSpec((tm, d), lambda i: (i, 0)),
                  pl.BlockSpec((1, d), lambda i: (0, 0)),
                  pl.BlockSpec((d, n), lambda i: (0, 0))],
        out_specs=pl.BlockSpec((tm, n), lambda i: (i, 0)),
        out_shape=jax.ShapeDtypeStruct((m, n), BF16),
        compiler_params=_params(1),
        name=name,
    )(x2d, g.reshape(1, d), w)


def _kmean_kernel(k_ref, o_ref):
    nb = o_ref.shape[0]
    for n in range(nb):
        blk = k_ref[n * MOBA_BLOCK:(n + 1) * MOBA_BLOCK, :].astype(F32)
        o_ref[n:n + 1, :] = jnp.mean(blk, axis=0, keepdims=True)


def _moba_kmean(tok):
    b, s, _ = tok.shape
    nb = s // MOBA_BLOCK
    return pl.pallas_call(
        _kmean_kernel,
        grid=(b,),
        in_specs=[pl.BlockSpec((None, s, MOBA_W), lambda i: (i, 0, COL_MK // MOBA_W))],
        out_specs=pl.BlockSpec((None, nb, MOBA_W), lambda i: (i, 0, 0)),
        out_shape=jax.ShapeDtypeStruct((b, nb, MOBA_W), F32),
        compiler_params=_params(1),
        name="moba_kmean",
    )(tok)


def _fill_v_with_ones(vt_ref, lo_ref, hi_ref, row0=0, cols=1024):
    s = vt_ref.shape[1]
    ones = jnp.ones((ONES_ROWS, cols), BF16)

    def body(c, carry):
        c0 = pl.multiple_of(c * cols, cols)
        lo_ref[0:HEAD_DIM, pl.ds(c0, cols)] = vt_ref[row0:row0 + HEAD_DIM, pl.ds(c0, cols)]
        lo_ref[HEAD_DIM:VROWS, pl.ds(c0, cols)] = ones
        hi_ref[0:HEAD_DIM, pl.ds(c0, cols)] = vt_ref[row0 + HEAD_DIM:row0 + 2 * HEAD_DIM, pl.ds(c0, cols)]
        hi_ref[HEAD_DIM:VROWS, pl.ds(c0, cols)] = ones
        return carry

    lax.fori_loop(0, s // cols, body, 0)


def _split_heads_t(q_t):
    row = lax.broadcasted_iota(jnp.int32, q_t.shape, 0)
    zero = jnp.zeros_like(q_t)
    return jnp.where(row < HEAD_DIM, q_t, zero), jnp.where(row >= HEAD_DIM, q_t, zero)


def _fill_k_with_block_ids(k_ref, lo_ref, hi_ref, block, nblock, lane0=0, rows=512):
    s = k_ref.shape[0]
    lane = lax.broadcasted_iota(jnp.int32, (rows, LANES), 1)
    shift = int(math.log2(block))

    def body(c, carry):
        r0 = pl.multiple_of(c * rows, rows)
        key = r0 + lax.broadcasted_iota(jnp.int32, (rows, LANES), 0)
        bid = lax.shift_right_logical(key, shift) & (nblock - 1)
        k = k_ref[pl.ds(r0, rows), lane0:lane0 + LANES]
        lo_ref[pl.ds(r0, rows), :] = jnp.where(lane < HEAD_DIM, k, (lane - HEAD_DIM == bid).astype(BF16))
        hi_ref[pl.ds(r0, rows), :] = jnp.where(lane >= HEAD_DIM, k, (lane == bid).astype(BF16))
        return carry

    lax.fori_loop(0, s // rows, body, 0)


def _q_with_mask_rows(q_rows, mask_rows, low_head):
    r = lax.broadcasted_iota(jnp.int32, (ONES_ROWS, TQ), 0)
    m = jnp.zeros((ONES_ROWS, TQ), F32)
    for b, row in enumerate(mask_rows):
        m = jnp.where(r == b, row, m)
    pad = jnp.zeros((HEAD_DIM - ONES_ROWS, TQ), BF16)
    parts = [q_rows, m.astype(BF16), pad] if low_head else [m.astype(BF16), pad, q_rows]
    return jnp.concatenate(parts, axis=0)


def _pipe(bufs, m_ref, acc_ref, nheads, a=None, b=None, c=None, c_last=False):
    s_refs, p_refs, mx_ref, al_ref = bufs
    rows = s_refs[0].shape[1]

    def stage_c(h):
        c_slot, c_base, vt_fn = c
        acc_ref[c_base + h] = (al_ref[c_slot, h] * acc_ref[c_base + h]
                               + _dot(vt_fn(h), p_refs[c_slot][h]))

    for h in range(nheads):
        if c is not None and not c_last:
            stage_c(h)
        if a is not None:
            a_slot, score_fn, bias_fn = a
            raw = score_fn(h)
            mx = None
        if b is not None:
            b_slot, b_base = b
            m_prev = m_ref[b_base + h]
            m_next = jnp.maximum(m_prev, mx_ref[b_slot, h])
            al_ref[b_slot, h] = jnp.exp2(m_prev - m_next)
            m_ref[b_base + h] = m_next
        for i in range(rows // STRIP):
            r = slice(i * STRIP, (i + 1) * STRIP)
            if b is not None:
                p_refs[b_slot][h, r, :] = jnp.exp2(s_refs[b_slot][h, r, :] - m_next).astype(BF16)
            if a is not None:
                s = raw[r, :] if bias_fn is None else raw[r, :] + bias_fn(h, i)
                s_refs[a_slot][h, r, :] = s
                cmax = jnp.max(s, axis=0, keepdims=True)
                mx = cmax if mx is None else jnp.maximum(mx, cmax)
        if a is not None:
            mx_ref[a_slot, h] = mx
        if c is not None and c_last:
            stage_c(h)


def _reset_state(bufs, m_ref, acc_ref):
    _, p_refs, _, al_ref = bufs
    m_ref[...] = jnp.full(m_ref.shape, NEG_BIG, F32)
    acc_ref[...] = jnp.zeros(acc_ref.shape, F32)
    p_refs[1][...] = jnp.zeros(p_refs[1].shape, BF16)
    al_ref[...] = jnp.ones(al_ref.shape, F32)


def _pair_output(acc_lo, acc_hi):
    o_lo = acc_lo[0:HEAD_DIM, :] / acc_lo[HEAD_DIM:HEAD_DIM + 1, :]
    o_hi = acc_hi[0:HEAD_DIM, :] / acc_hi[HEAD_DIM:HEAD_DIM + 1, :]
    return jnp.transpose(jnp.concatenate([o_lo, o_hi], axis=0))


def _top_k_rows(val, n_f, k, sentinel):
    chosen = jnp.zeros(val.shape, F32)
    for _ in range(k):
        mx = jnp.max(val, axis=0, keepdims=True)
        idx = jnp.min(jnp.where(val == mx, n_f, sentinel), axis=0, keepdims=True)
        pick = n_f == idx
        chosen = jnp.where(pick, 1.0, chosen)
        val = jnp.where(pick, -jnp.inf, val)
    return chosen


def _moba_kernel(qt_ref, k_ref, vt_ref, km_ref, bt_ref, o_ref, *scratch):
    npair = qt_ref.shape[0] // LANES
    nh = 2 * npair
    k_refs, v_refs = scratch[0:nh], scratch[nh:2 * nh]
    mb_ref, qs_ref, m_ref, acc_ref, s0_ref, s1_ref, p0_ref, p1_ref, mx_ref, al_ref = scratch[2 * nh:]
    cur = pl.program_id(2)
    bufs = ((s0_ref, s1_ref), (p0_ref, p1_ref), mx_ref, al_ref)
    nb = km_ref.shape[0]
    sk = s0_ref.shape[1]
    per_block = KT // STRIP

    @pl.when(cur == 0)
    def _():
        for pp in range(npair):
            _fill_v_with_ones(vt_ref, v_refs[2 * pp], v_refs[2 * pp + 1], row0=pp * LANES)
            _fill_k_with_block_ids(k_ref, k_refs[2 * pp], k_refs[2 * pp + 1], MOBA_BLOCK, 2, lane0=pp * LANES)

    n_f = lax.broadcasted_iota(jnp.int32, (nb, TQ), 0).astype(F32)
    past = n_f < cur.astype(F32)
    for pp in range(npair):
        km = km_ref[:, pp * LANES:(pp + 1) * LANES].astype(BF16)
        for hh, qh in enumerate(_split_heads_t(qt_ref[pp * LANES:(pp + 1) * LANES, :])):
            gate = jnp.where(past, _dot(km, qh), -jnp.inf)
            chosen = _top_k_rows(gate, n_f, min(MOBA_TOPK, nb), float(nb))
            mb_ref[2 * pp + hh] = jnp.where((chosen > 0.5) & past, 0.0, NEG_BIG)
            qs_ref[2 * pp + hh] = qh
    _reset_state(bufs, m_ref, acc_ref)

    def first_block(u):
        return jnp.maximum(cur - 2 * u - 1, 0)

    def score_fn(u, first_rows, second_rows):
        idx = first_block(u)
        ks = pl.multiple_of(idx * KT, KT)
        first_even = (idx & 1) == 0

        def f(h):
            low = h % 2 == 0
            even = jnp.where(first_even, first_rows[h], second_rows[h])
            odd = jnp.where(first_even, second_rows[h], first_rows[h])
            q_rows = qs_ref[h, 0:HEAD_DIM, :] if low else qs_ref[h, HEAD_DIM:2 * HEAD_DIM, :]
            return _dot(k_refs[h][pl.ds(ks, sk), :], _q_with_mask_rows(q_rows, (even, odd), low))
        return f

    def vt_fn(u):
        ks = pl.multiple_of(first_block(jnp.maximum(u, 0)) * KT, KT)
        return lambda h: v_refs[h][:, pl.ds(ks, sk)]

    has_prev = cur >= 1
    kind_lo = jnp.where(has_prev, BT_PREV, BT_OWN)
    kind_hi = jnp.where(has_prev, BT_OWN, BT_MASKED)
    keep = jnp.where(has_prev, 1.0, 0.0)
    prev_rows = [mb_ref[h, pl.ds(jnp.maximum(cur - 1, 0), 1), :] * keep for h in range(nh)]
    no_rows = [jnp.zeros((1, TQ), F32)] * nh

    def near_bias(h, i):
        half, part = divmod(i, per_block)
        return bt_ref[h, kind_lo if half == 0 else kind_hi, part * STRIP:(part + 1) * STRIP, :]

    def far_scores(u):
        b_lo = cur - 2 * u - 1
        idx = jnp.maximum(b_lo, 0)
        pen_lo = jnp.where(b_lo >= -1, 0.0, NEG_BIG)
        pen_hi = jnp.where(b_lo >= 0, 0.0, NEG_BIG)
        return score_fn(u, [mb_ref[h, pl.ds(idx, 1), :] + pen_lo for h in range(nh)],
                        [mb_ref[h, pl.ds(idx + 1, 1), :] + pen_hi for h in range(nh)])

    pipe = functools.partial(_pipe, bufs, m_ref, acc_ref, nh)
    pipe(a=(0, score_fn(0, prev_rows, no_rows), near_bias))
    n_far = lax.shift_right_logical(cur, 1)

    def two_jobs(t, carry):
        u0 = 2 * t + 1
        pipe(a=(1, far_scores(u0), None), b=(0, 0), c=(1, 0, vt_fn(u0 - 2)))
        pipe(a=(0, far_scores(u0 + 1), None), b=(1, 0), c=(0, 0, vt_fn(u0 - 1)))
        return carry

    n_iter = lax.shift_right_logical(n_far + 1, 1)
    lax.fori_loop(0, n_iter, two_jobs, 0)
    pipe(b=(0, 0), c=(1, 0, vt_fn(2 * n_iter - 1)))
    pipe(c=(0, 0, vt_fn(2 * n_iter)))
    for pp in range(npair):
        o_ref[:, pp * LANES:(pp + 1) * LANES] = _pair_output(
            acc_ref[2 * pp], acc_ref[2 * pp + 1]).astype(o_ref.dtype)


def _moba_attention(tok, fm, kmean, bias_tiles, pairs_per_step=2):
    b, s, _ = tok.shape
    nb = s // MOBA_BLOCK
    nh = 2 * pairs_per_step
    w = pairs_per_step * LANES
    sk = 2 * KT
    return pl.pallas_call(
        _moba_kernel,
        grid=(b, MOBA_W // w, s // TQ),
        in_specs=[
            pl.BlockSpec((None, w, TQ), lambda i, p, c: (i, ROW_MQ // w + p, c)),
            pl.BlockSpec((None, s, w), lambda i, p, c: (i, 0, COL_MK // w + p)),
            pl.BlockSpec((None, w, s), lambda i, p, c: (i, ROW_MV // w + p, 0)),
            pl.BlockSpec((None, nb, w), lambda i, p, c: (i, 0, p)),
            pl.BlockSpec((nh, N_BIAS_TILES, KT, TQ), lambda i, p, c: (p, 0, 0, 0)),
        ],
        out_specs=pl.BlockSpec((None, TQ, w), lambda i, p, c: (i, c, p)),
        out_shape=jax.ShapeDtypeStruct((b, s, MOBA_W), BF16),
        scratch_shapes=[pltpu.VMEM((s, LANES), BF16)] * nh + [pltpu.VMEM((VROWS, s), BF16)] * nh + [
            pltpu.VMEM((nh, nb, TQ), F32),
            pltpu.VMEM((nh, LANES, TQ), BF16),
            pltpu.VMEM((nh, 1, TQ), F32),
            pltpu.VMEM((nh, VROWS, TQ), F32),
            pltpu.VMEM((nh, sk, TQ), F32), pltpu.VMEM((nh, sk, TQ), F32),
            pltpu.VMEM((nh, sk, TQ), BF16), pltpu.VMEM((nh, sk, TQ), BF16),
            pltpu.VMEM((2, nh, 1, TQ), F32), pltpu.VMEM((2, nh, 1, TQ), F32),
        ],
        compiler_params=_params(3),
        name="moba_attention",
    )(fm, tok, fm, kmean, bias_tiles)


def _compress_kernel(ck_ref, cv_ref, pk_ref, pv_ref, w1k_ref, w1v_ref, w2k_ref, w2v_ref,
                     kc_ref, vct_ref):
    nch = ck_ref.shape[1]
    half = CMP_STRIDE * HEAD_DIM

    def run(c_ref, pos_ref, w1_ref, w2_ref):
        acc = jnp.zeros((nch, LANES), F32)
        for g in range(NSA_GROUPS):
            c = c_ref[g].astype(F32)
            lo = (c + pos_ref[0:1, :]).astype(BF16)
            hi = (c + pos_ref[1:2, :]).astype(BF16)
            pre = _dot(lo, w1_ref[:half, :]) + pltpu.roll(_dot(hi, w1_ref[half:, :]), nch - 1, axis=0)
            acc = acc + _dot(jax.nn.gelu(pre).astype(BF16), w2_ref[g])
        return acc

    kc_ref[...] = run(ck_ref, pk_ref, w1k_ref, w2k_ref).astype(kc_ref.dtype)
    vct_ref[...] = jnp.transpose(run(cv_ref, pv_ref, w1v_ref, w2v_ref)).astype(vct_ref.dtype)


def _compress(chunks, pos_k, pos_v, w1k, w1v, w2k, w2v):
    b, _, nch, width = chunks.shape
    g = NSA_GROUPS
    full = lambda shape: pl.BlockSpec(shape, lambda i: (0,) * len(shape))
    return pl.pallas_call(
        _compress_kernel,
        grid=(b,),
        in_specs=[pl.BlockSpec((None, g, nch, width), lambda i: (i, 0, 0, 0)),
                  pl.BlockSpec((None, g, nch, width), lambda i: (i, 1, 0, 0)),
                  full(pos_k.shape), full(pos_v.shape), full(w1k.shape), full(w1v.shape),
                  full(w2k.shape), full(w2v.shape)],
        out_specs=[pl.BlockSpec((None, nch, LANES), lambda i: (i, 0, 0)),
                   pl.BlockSpec((None, LANES, nch), lambda i: (i, 0, 0))],
        out_shape=[jax.ShapeDtypeStruct((b, nch, LANES), BF16),
                   jax.ShapeDtypeStruct((b, LANES, nch), BF16)],
        compiler_params=_params(1),
        name="nsa_compress",
    )(chunks, chunks, pos_k, pos_v, w1k, w1v, w2k, w2v)


def _cmp_select_kernel(qt_ref, kc_ref, vct_ref, mimp_ref, o_ref, sel_ref, *, nsb, nq):
    ci = pl.program_id(1)
    nch = kc_ref.shape[0]
    nsbp = mimp_ref.shape[0]
    parts = 4 if (nch // 4) % LANES == 0 and (nsbp // 4) % 8 == 0 else 1
    for v in range(1, parts + 1):
        @pl.when((ci * parts >= (v - 1) * nq) & (ci * parts < v * nq))
        def _(v=v):
            _cmp_select_body(qt_ref, kc_ref, vct_ref, mimp_ref, o_ref, sel_ref,
                             nsb=nsb, nch=nch * v // parts, nrow=nsbp * v // parts)


def _cmp_select_body(qt_ref, kc_ref, vct_ref, mimp_ref, o_ref, sel_ref, *, nsb, nch, nrow):
    qs = pl.program_id(1) * TQ
    n_row = lax.broadcasted_iota(jnp.int32, (nch, TQ), 0)
    t_col = qs + lax.broadcasted_iota(jnp.int32, (nch, TQ), 1)
    hidden = jnp.where((n_row * CMP_STRIDE + (CMP_BLOCK - 1)) <= t_col, 0.0, NEG_BIG)
    any_visible = t_col[0:1, :] >= (CMP_BLOCK - 1)
    kc = kc_ref[0:nch, :]
    vct = vct_ref[:, 0:nch]
    vrow = lax.broadcasted_iota(jnp.int32, vct.shape, 0)
    mimp = mimp_ref[0:nrow, 0:nch]

    j_row = lax.broadcasted_iota(jnp.int32, (nrow, TQ), 0)
    t_sel = qs + lax.broadcasted_iota(jnp.int32, (nrow, TQ), 1)
    cur = lax.shift_right_logical(t_sel, int(math.log2(SLC_BLOCK)))
    j_f = j_row.astype(F32)
    forced = (j_row == 0) | (j_row == cur) | (j_row == cur - 1)
    allowed = (j_row <= cur) & (j_row < nsb)

    q_heads = [_split_heads_t(qt_ref[r * LANES:(r + 1) * LANES, :]) for r in range(NSA_HPG)]
    out_t = [jnp.zeros((LANES, TQ), F32) for _ in range(NSA_HPG)]
    for g in range(NSA_GROUPS):
        vin_g = (vrow < HEAD_DIM) if g == 0 else (vrow >= HEAD_DIM)
        vct_g = jnp.where(vin_g, vct, jnp.zeros_like(vct))
        p_grp = jnp.zeros((nch, TQ), F32)
        for r in range(NSA_HPG):
            s = _dot(kc, q_heads[r][g]) + hidden
            e = jnp.exp2(s - jnp.max(s, axis=0, keepdims=True))
            p = e * jnp.where(any_visible, 1.0 / jnp.sum(e, axis=0, keepdims=True), 0.0)
            p_grp = p_grp + p
            out_t[r] = out_t[r] + _dot(vct_g, p.astype(BF16))
        p_hi = p_grp.astype(BF16)
        p_lo = (p_grp - p_hi.astype(F32)).astype(BF16)
        imp = _dot(mimp, p_hi) + _dot(mimp, p_lo)
        val = jnp.where(forced & allowed, jnp.inf, jnp.where(allowed, imp, -jnp.inf))
        chosen = _top_k_rows(val, j_f, min(SLC_TOPN, nsb), float(nrow))
        sel_ref[g, 0:nrow, :] = jnp.where(allowed & (chosen > 0.5), 0.0, NEG_BIG)
        if nrow < sel_ref.shape[1]:
            sel_ref[g, nrow:, :] = jnp.full((sel_ref.shape[1] - nrow, TQ), NEG_BIG, F32)
    for r in range(NSA_HPG):
        o_ref[:, r * LANES:(r + 1) * LANES] = jnp.transpose(out_t[r]).astype(o_ref.dtype)


def _cmp_select(fm, kc, vct, mimp_t, nsb):
    b, _, s = fm.shape
    nch = kc.shape[1]
    nsbp = mimp_t.shape[0]
    return pl.pallas_call(
        functools.partial(_cmp_select_kernel, nsb=nsb, nq=s // TQ),
        grid=(b, s // TQ),
        in_specs=[pl.BlockSpec((None, NSA_W, TQ), lambda i, c: (i, ROW_NQ // NSA_W, c)),
                  pl.BlockSpec((None, nch, LANES), lambda i, c: (i, 0, 0)),
                  pl.BlockSpec((None, LANES, nch), lambda i, c: (i, 0, 0)),
                  pl.BlockSpec((nsbp, nch), lambda i, c: (0, 0))],
        out_specs=[pl.BlockSpec((None, TQ, NSA_W), lambda i, c: (i, c, 0)),
                   pl.BlockSpec((None, NSA_GROUPS, nsbp, TQ), lambda i, c: (i, 0, 0, c))],
        out_shape=[jax.ShapeDtypeStruct((b, s, NSA_W), BF16),
                   jax.ShapeDtypeStruct((b, NSA_GROUPS, nsbp, s), F32)],
        compiler_params=_params(2),
        name="nsa_cmp_select",
    )(fm, kc, vct, mimp_t)


def _slc_win_kernel(qt_ref, ks_ref, vst_ref, kw_ref, vwt_ref, sel_ref, bt_ref,
                    oslc_ref, owin_ref,
                    ks0_ref, ks1_ref, vs0_ref, vs1_ref, vw0_ref, vw1_ref, qs_ref, m_ref, acc_ref,
                    s0_ref, s1_ref, p0_ref, p1_ref, mx_ref, al_ref):
    cur = pl.program_id(1)
    bufs = ((s0_ref, s1_ref), (p0_ref, p1_ref), mx_ref, al_ref)
    per_tile = KT // SLC_BLOCK

    @pl.when(cur == 0)
    def _():
        _fill_v_with_ones(vst_ref, vs0_ref, vs1_ref)
        _fill_v_with_ones(vwt_ref, vw0_ref, vw1_ref)
        _fill_k_with_block_ids(ks_ref, ks0_ref, ks1_ref, SLC_BLOCK, per_tile)

    nh = NSA_HEADS
    for r in range(NSA_HPG):
        lo, hi = _split_heads_t(qt_ref[r * LANES:(r + 1) * LANES, :])
        qs_ref[r] = lo
        qs_ref[NSA_HPG + r] = hi
    _reset_state(bufs, m_ref, acc_ref)
    ks_refs = (ks0_ref, ks1_ref)
    vs_refs = (vs0_ref, vs1_ref)
    vw_refs = (vw0_ref, vw1_ref)
    sel_base, win_base = 0, nh

    def tile_start(n):
        return pl.multiple_of(jnp.maximum(n, 0) * KT, KT)

    def win_scores(n):
        ks = tile_start(n)
        return lambda h: _dot(kw_ref[pl.ds(ks, KT), :], qs_ref[h])

    def sel_scores(n, penalty):
        ks = tile_start(n)
        base = jnp.maximum(n, 0) * per_tile
        rows = [[sel_ref[g, pl.ds(base + b, 1), :] + penalty for b in range(per_tile)]
                for g in range(NSA_GROUPS)]

        def f(h):
            g = h // NSA_HPG
            q_rows = qs_ref[h, 0:HEAD_DIM, :] if g == 0 else qs_ref[h, HEAD_DIM:2 * HEAD_DIM, :]
            return _dot(ks_refs[g][pl.ds(ks, KT), :], _q_with_mask_rows(q_rows, rows[g], g == 0))
        return f

    def vt_fn(v_refs, n):
        ks = tile_start(n)
        return lambda h: v_refs[h // NSA_HPG][:, pl.ds(ks, KT)]

    def near_bias(kind):
        return lambda h, i: bt_ref[h, kind, i * STRIP:(i + 1) * STRIP, :]

    pipe = functools.partial(_pipe, bufs, m_ref, acc_ref, nh, c_last=True)
    kind_prev = jnp.where(cur >= 1, BT_PREV, BT_MASKED)
    kind_upper = jnp.where(cur >= 2, BT_UPPER, BT_MASKED)
    pipe(a=(0, win_scores(cur), near_bias(BT_OWN)))
    pipe(a=(1, win_scores(cur - 1), near_bias(kind_prev)), b=(0, win_base))
    pipe(a=(0, win_scores(cur - 2), near_bias(kind_upper)), b=(1, win_base),
         c=(0, win_base, vt_fn(vw_refs, cur)))
    pipe(a=(1, sel_scores(cur, 0.0), near_bias(BT_OWN)), b=(0, win_base),
         c=(1, win_base, vt_fn(vw_refs, cur - 1)))
    pipe(a=(0, sel_scores(cur - 1, 0.0), near_bias(kind_prev)), b=(1, sel_base),
         c=(0, win_base, vt_fn(vw_refs, cur - 2)))

    def far_scores(n):
        return sel_scores(n, jnp.where(n >= 0, 0.0, NEG_BIG))

    def two_tiles(t, carry):
        n0 = cur - 2 - 2 * t
        pipe(a=(1, far_scores(n0), None), b=(0, sel_base), c=(1, sel_base, vt_fn(vs_refs, n0 + 2)))
        pipe(a=(0, far_scores(n0 - 1), None), b=(1, sel_base), c=(0, sel_base, vt_fn(vs_refs, n0 + 1)))
        return carry

    n_iter = lax.shift_right_logical(jnp.maximum(cur - 1, 0) + 1, 1)
    lax.fori_loop(0, n_iter, two_tiles, 0)
    pipe(b=(0, sel_base), c=(1, sel_base, vt_fn(vs_refs, cur - 2 * n_iter)))
    pipe(c=(0, sel_base, vt_fn(vs_refs, cur - 1 - 2 * n_iter)))

    for r in range(NSA_HPG):
        cols = slice(r * LANES, (r + 1) * LANES)
        oslc_ref[:, cols] = _pair_output(acc_ref[r], acc_ref[NSA_HPG + r]).astype(oslc_ref.dtype)
        owin_ref[:, cols] = _pair_output(acc_ref[nh + r], acc_ref[nh + NSA_HPG + r]).astype(owin_ref.dtype)


def _slc_win(tok, fm, sel, bias_tiles):
    b, s, _ = tok.shape
    nsbp = sel.shape[2]
    return pl.pallas_call(
        _slc_win_kernel,
        grid=(b, s // TQ),
        in_specs=[pl.BlockSpec((None, NSA_W, TQ), lambda i, c: (i, ROW_NQ // NSA_W, c)),
                  pl.BlockSpec((None, s, LANES), lambda i, c: (i, 0, COL_NKS // LANES)),
                  pl.BlockSpec((None, LANES, s), lambda i, c: (i, ROW_NVS // LANES, 0)),
                  pl.BlockSpec((None, s, LANES), lambda i, c: (i, 0, COL_NKW // LANES)),
                  pl.BlockSpec((None, LANES, s), lambda i, c: (i, ROW_NVW // LANES, 0)),
                  pl.BlockSpec((None, NSA_GROUPS, nsbp, TQ), lambda i, c: (i, 0, 0, c)),
                  pl.BlockSpec((NSA_HEADS, N_BIAS_TILES, KT, TQ), lambda i, c: (1, 0, 0, 0))],
        out_specs=[pl.BlockSpec((None, TQ, NSA_W), lambda i, c: (i, c, 0)),
                   pl.BlockSpec((None, TQ, NSA_W), lambda i, c: (i, c, 0))],
        out_shape=[jax.ShapeDtypeStruct((b, s, NSA_W), BF16),
                   jax.ShapeDtypeStruct((b, s, NSA_W), BF16)],
        scratch_shapes=[pltpu.VMEM((s, LANES), BF16)] * 2 + [pltpu.VMEM((VROWS, s), BF16)] * 4 + [
            pltpu.VMEM((NSA_HEADS, LANES, TQ), BF16),
            pltpu.VMEM((2 * NSA_HEADS, 1, TQ), F32),
            pltpu.VMEM((2 * NSA_HEADS, VROWS, TQ), F32),
            pltpu.VMEM((NSA_HEADS, KT, TQ), F32), pltpu.VMEM((NSA_HEADS, KT, TQ), F32),
            pltpu.VMEM((NSA_HEADS, KT, TQ), BF16), pltpu.VMEM((NSA_HEADS, KT, TQ), BF16),
            pltpu.VMEM((2, NSA_HEADS, 1, TQ), F32), pltpu.VMEM((2, NSA_HEADS, 1, TQ), F32)],
        compiler_params=_params(2),
        name="nsa_slc_win",
    )(fm, tok, fm, tok, fm, sel, bias_tiles)


def _merge_kernel(x_ref, oa_ref, oc_ref, os_ref, ow_ref, ng_ref, ga_ref, gb_ref,
                  eg_ref, wa_ref, wb_ref, wo_ref, o_ref):
    sig = jax.nn.sigmoid(ng_ref[...].astype(F32))
    s_hi = sig.astype(BF16)
    s_lo = (sig - s_hi.astype(F32)).astype(BF16)
    eg = eg_ref[...]
    gexp = _dot(s_hi, eg) + _dot(s_lo, eg)
    a = _dot(oa_ref[...], wa_ref[...])
    o_b = (gexp[:, :NSA_W] * oc_ref[...].astype(F32)
           + gexp[:, NSA_W:2 * NSA_W] * os_ref[...].astype(F32)
           + gexp[:, 2 * NSA_W:] * ow_ref[...].astype(F32))
    bb = _dot(o_b.astype(BF16), wb_ref[...])
    merged = (jax.nn.sigmoid(ga_ref[...].astype(F32)) * a
              + jax.nn.sigmoid(gb_ref[...].astype(F32)) * bb)
    o_ref[...] = x_ref[...] + _dot(merged.astype(BF16), wo_ref[...])


def _merge(x2d, tok2d, o_a, o_c, o_s, o_w, egate, wa, wb, wo, tm):
    m, d = x2d.shape
    row = lambda w, col=0: pl.BlockSpec((tm, w), lambda i, col=col: (i, col))
    full = lambda a: pl.BlockSpec(a.shape, lambda i: (0, 0))
    return pl.pallas_call(
        _merge_kernel,
        grid=(m // tm,),
        in_specs=[row(d), row(MOBA_W), row(NSA_W), row(NSA_W), row(NSA_W),
                  row(GATE_PAD, COL_NGATE // GATE_PAD),
                  row(D_MODEL, COL_GA // D_MODEL), row(D_MODEL, COL_GB // D_MODEL),
                  full(egate), full(wa), full(wb), full(wo)],
        out_specs=row(d),
        out_shape=jax.ShapeDtypeStruct((m, d), F32),
        compiler_params=_params(1),
        name="merge_out",
    )(x2d, o_a, o_c, o_s, o_w, tok2d, tok2d, tok2d, egate, wa, wb, wo)


def _xattn_kernel(x_ref, g_ref, wq_ref, kv_ref, wo_ref, o_ref, *, nsplit):
    part = x_ref.shape[0] // nsplit
    g = g_ref[...]
    rows = [slice(i * part, (i + 1) * part) for i in range(nsplit)]
    xs = [x_ref[r, :] for r in rows]
    qs = [_dot(_rms(x, g).astype(BF16), wq_ref[...]).astype(BF16) for x in xs]
    for r, x, q in zip(rows, xs, qs):
        cols = [slice(h * XATTN_HEAD_DIM, (h + 1) * XATTN_HEAD_DIM) for h in range(XATTN_HEADS)]
        scores = [_dot_nt(q[:, c], kv_ref[:, c]) for c in cols]
        heads = []
        for h, s in enumerate(scores):
            v = kv_ref[:, XATTN_W + h * XATTN_HEAD_DIM:XATTN_W + (h + 1) * XATTN_HEAD_DIM]
            e = jnp.exp2(s - jnp.max(s, axis=1, keepdims=True))
            p = e * (1.0 / jnp.sum(e, axis=1, keepdims=True))
            heads.append(_dot(p.astype(BF16), v).astype(BF16))
        o_ref[r, :] = x + _dot(jnp.concatenate(heads, axis=1), wo_ref[...])


def _xattn(x3d, g, wq, memkv, wo, tm):
    b, s, d = x3d.shape
    mem_len = memkv.shape[1]
    return pl.pallas_call(
        functools.partial(_xattn_kernel, nsplit=1),
        grid=(b, s // tm),
        in_specs=[pl.BlockSpec((None, tm, d), lambda i, j: (i, j, 0)),
                  pl.BlockSpec((1, d), lambda i, j: (0, 0)),
                  pl.BlockSpec(wq.shape, lambda i, j: (0, 0)),
                  pl.BlockSpec((None, mem_len, 2 * XATTN_W), lambda i, j: (i, 0, 0)),
                  pl.BlockSpec(wo.shape, lambda i, j: (0, 0))],
        out_specs=pl.BlockSpec((None, tm, d), lambda i, j: (i, j, 0)),
        out_shape=jax.ShapeDtypeStruct((b, s, d), F32),
        compiler_params=_params(2),
        name="xattn",
    )(x3d, g.reshape(1, d), wq, memkv, wo)


def _ffn_kernel(x_ref, halo_ref, g_ref, wg_ref, wu_ref, cw_ref, cb_ref, wd_ref, gf_ref, o_ref,
                y_ref, *, chunk, halo_rows):
    j = pl.program_id(1)
    x = x_ref[...]
    tm = x.shape[0]
    g = g_ref[...]
    hf = _rms(x, g).astype(BF16)
    keep = jnp.where(j > 0, 1.0, 0.0)
    halo = (_rms(halo_ref[...], g) * keep).astype(BF16)
    ext = jnp.concatenate([halo, hf], axis=0)
    row = lax.broadcasted_iota(jnp.int32, (tm, chunk), 0)
    starts = list(range(0, D_FF, chunk))

    def gate_up(c0):
        cols = slice(c0, c0 + chunk)
        return _dot(ext, wg_ref[:, cols]), _dot(hf, wu_ref[:, cols])

    nxt = gate_up(starts[0])
    for k, c0 in enumerate(starts):
        cols = slice(c0, c0 + chunk)
        ue, up = nxt
        if k + 1 < len(starts):
            nxt = gate_up(starts[k + 1])
        u = ue[halo_rows:, :]
        prev1 = ue[halo_rows - 1:halo_rows, :]
        prev2 = ue[halo_rows - 2:halo_rows - 1, :]
        u1 = jnp.where(row == 0, prev1, pltpu.roll(u, 1, axis=0))
        u2 = jnp.where(row == 0, prev2, jnp.where(row == 1, prev1, pltpu.roll(u, 2, axis=0)))
        a = cw_ref[0:1, cols] * u2 + cw_ref[1:2, cols] * u1 + cw_ref[2:3, cols] * u + cb_ref[:, cols]
        y_ref[:, cols] = (jax.nn.gelu(a) * up).astype(BF16)
    o_ref[...] = _rms(x + _dot(y_ref[...], wd_ref[...]), gf_ref[...])


def _ffn(x3d, g, wg, wu, cw, cb, wd, gf, tm):
    b, s, d = x3d.shape
    halo_rows = 16
    per = tm // halo_rows
    return pl.pallas_call(
        functools.partial(_ffn_kernel, chunk=256, halo_rows=halo_rows),
        grid=(b, s // tm),
        in_specs=[pl.BlockSpec((None, tm, d), lambda i, j: (i, j, 0)),
                  pl.BlockSpec((None, halo_rows, d), lambda i, j: (i, jnp.maximum(j * per - 1, 0), 0)),
                  pl.BlockSpec((1, d), lambda i, j: (0, 0)),
                  pl.BlockSpec(wg.shape, lambda i, j: (0, 0)),
                  pl.BlockSpec(wu.shape, lambda i, j: (0, 0)),
                  pl.BlockSpec(cw.shape, lambda i, j: (0, 0)),
                  pl.BlockSpec((1, D_FF), lambda i, j: (0, 0)),
                  pl.BlockSpec(wd.shape, lambda i, j: (0, 0)),
                  pl.BlockSpec((1, d), lambda i, j: (0, 0))],
        out_specs=pl.BlockSpec((None, tm, d), lambda i, j: (i, j, 0)),
        out_shape=jax.ShapeDtypeStruct((b, s, d), F32),
        scratch_shapes=[pltpu.VMEM((tm, D_FF), BF16)],
        compiler_params=_params(2),
        name="conv_ffn",
    )(x3d, x3d, g.reshape(1, d), wg, wu, cw, cb.reshape(1, D_FF), wd, gf.reshape(1, d))


def _nq_perm():
    idx = []
    for r in range(NSA_HPG):
        for g in range(NSA_GROUPS):
            h = g * NSA_HPG + r
            idx.extend(range(h * HEAD_DIM, (h + 1) * HEAD_DIM))
    return np.asarray(idx, np.int32)


def _gate_expand():
    e = np.zeros((GATE_PAD, 3 * NSA_W), np.float32)
    for br in range(3):
        for g in range(NSA_GROUPS):
            for r in range(NSA_HPG):
                c0 = br * NSA_W + r * LANES + g * HEAD_DIM
                e[br * NSA_HEADS + g * NSA_HPG + r, c0:c0 + HEAD_DIM] = 1.0
    return e


def _importance_matrix(nch, nsb, nsbp):
    ratio = SLC_BLOCK // CMP_STRIDE
    nc = nch - 1
    m = np.zeros((nsbp, nch), np.float32)
    for j in range(nsb):
        for a in range(ratio):
            for bb in range(CMP_BLOCK // CMP_STRIDE):
                c = j * ratio + a - bb
                if 0 <= c < nc:
                    m[j, c] += 1.0
    return m


def _split_w_in(w_in):
    widths = [MOBA_W] * 3 + [NSA_W] + [NSA_KV_W] * 6 + [3 * NSA_HEADS, D_MODEL, D_MODEL]
    cuts = np.cumsum(widths)[:-1]
    mq, mk, mv, nq, nkc, nvc, nks, nvs, nkw, nvw, ngate, ga, gb = jnp.split(w_in, cuts, axis=1)
    ngate = jnp.pad(ngate, ((0, 0), (0, GATE_PAD - 3 * NSA_HEADS)))
    q_scale = HEAD_DIM ** -0.5 * LOG2E
    mq, nq = mq * q_scale, nq * q_scale
    w_tok = jnp.concatenate([ga, gb, mk, nks, nkw, ngate], axis=1)
    w_fm = jnp.concatenate([mq, mv, nq[:, _nq_perm()], nvs, nvw], axis=1)
    w_cmp = jnp.concatenate([nkc, nvc], axis=1)
    return w_tok.astype(BF16), jnp.transpose(w_fm).astype(BF16), w_cmp.astype(BF16)


def _layer(x, mem, bias_tiles, norm_mix_g, w_in, cmp_pos_k, cmp_w1_k, cmp_w2_k, cmp_pos_v,
           cmp_w1_v, cmp_w2_v, w_branch_a, w_branch_b, w_out, norm_xattn_g, norm_mem_g, w_xq, w_xkv, w_xo,
           norm_ffn_g, w_gate, w_up, conv_w, conv_b, w_down, norm_out_g):
    b, s, d = x.shape
    assert s % 1024 == 0 and d == D_MODEL
    rows = b * s
    nch = s // CMP_STRIDE
    nsb = s // SLC_BLOCK
    nsbp = max(LANES, nsb)

    w_tok, w_fm_t, w_cmp = _split_w_in(w_in)
    tok, fm, cmp_kv = _in_proj(x, norm_mix_g, w_tok, w_fm_t, w_cmp, tm=512)
    tok2d = tok.reshape(rows, TOK_W)

    kmean = _moba_kmean(tok)
    o_a = _moba_attention(tok, fm, kmean, bias_tiles)

    def w2_pad(w2):
        z = jnp.zeros_like(w2)
        return jnp.stack([jnp.concatenate([w2, z], axis=1), jnp.concatenate([z, w2], axis=1)]).astype(BF16)

    kc, vct = _compress(cmp_kv.reshape(b, CMP_W // HEAD_DIM, nch, CMP_STRIDE * HEAD_DIM),
                        cmp_pos_k.reshape(2, -1), cmp_pos_v.reshape(2, -1),
                        cmp_w1_k.astype(BF16), cmp_w1_v.astype(BF16), w2_pad(cmp_w2_k), w2_pad(cmp_w2_v))
    mimp_t = jnp.asarray(_importance_matrix(nch, nsb, nsbp), BF16)
    o_c, sel = _cmp_select(fm, kc, vct, mimp_t, nsb)
    o_s, o_w = _slc_win(tok, fm, sel, bias_tiles)

    x1 = _merge(x.reshape(rows, d), tok2d, o_a.reshape(rows, MOBA_W), o_c.reshape(rows, NSA_W),
                o_s.reshape(rows, NSA_W), o_w.reshape(rows, NSA_W), jnp.asarray(_gate_expand(), BF16),
                w_branch_a.astype(BF16), w_branch_b[_nq_perm(), :].astype(BF16), w_out.astype(BF16), tm=512)

    mem_len = mem.shape[1]
    memkv = _norm_matmul(mem.reshape(b * mem_len, d), norm_mem_g, w_xkv.astype(BF16), tm=mem_len,
                         name="mem_kv").reshape(b, mem_len, 2 * XATTN_W)
    xq_scale = XATTN_HEAD_DIM ** -0.5 * LOG2E
    x2 = _xattn(x1.reshape(b, s, d), norm_xattn_g, (w_xq * xq_scale).astype(BF16), memkv,
                w_xo.astype(BF16), tm=512)

    return _ffn(x2, norm_ffn_g, w_gate.astype(BF16), w_up.astype(BF16), conv_w, conv_b,
                w_down.astype(BF16), norm_out_g, tm=512)


def kernel(x, mem, rel_bias, norm_mix_g, w_in, cmp_pos_k, cmp_w1_k, cmp_w2_k, cmp_pos_v, cmp_w1_v, cmp_w2_v, w_branch_a, w_branch_b, w_out, norm_xattn_g, norm_mem_g, w_xq, w_xkv, w_xo, norm_ffn_g, w_gate, w_up, conv_w, conv_b, w_down, norm_final_g):
    depth = w_in.shape[0]
    assert depth == 1, "the final norm is fused into the last layer's FFN kernel"
    bias_tiles = _bias_tiles(rel_bias)
    l = 0
    return _layer(x, mem, bias_tiles, norm_mix_g[l], w_in[l], cmp_pos_k[l], cmp_w1_k[l],
                  cmp_w2_k[l], cmp_pos_v[l], cmp_w1_v[l], cmp_w2_v[l], w_branch_a[l], w_branch_b[l], w_out[l],
                  norm_xattn_g[l], norm_mem_g[l], w_xq[l], w_xkv[l], w_xo[l],
                  norm_ffn_g[l], w_gate[l], w_up[l], conv_w[l], conv_b[l], w_down[l], norm_final_g)
```

```python
import functools
import math

import numpy as np
import jax
import jax.numpy as jnp
from jax import lax
from jax.experimental import pallas as pl
from jax.experimental.pallas import tpu as pltpu

F32 = jnp.float32
BF16 = jnp.bfloat16

D_MODEL = 1024
HEAD_DIM = 64
MOBA_HEADS = 8
MOBA_BLOCK = 256
MOBA_TOPK = 3
NSA_HEADS = 8
NSA_GROUPS = 2
NSA_HPG = NSA_HEADS // NSA_GROUPS
CMP_BLOCK = 32
CMP_STRIDE = 16
CMP_HIDDEN = 256
SLC_BLOCK = 64
SLC_TOPN = 16
WINDOW = 512
REL_BUCKETS = 32
REL_MAX_DIST = 128
XATTN_HEADS = 4
XATTN_HEAD_DIM = 128
D_FF = 2816
CONV_WIDTH = 3
Q_BLOCK = 128
RMS_EPS = 1e-6
NEG_BIG = -1e30
LOG2E = math.log2(math.e)

LANES = 128
TQ = 256
KT = 256
STRIP = 256
ONES_ROWS = 16
VROWS = HEAD_DIM + ONES_ROWS
VMEM_LIMIT = 56 * 1024 * 1024

MOBA_W = MOBA_HEADS * HEAD_DIM
NSA_W = NSA_HEADS * HEAD_DIM
NSA_KV_W = NSA_GROUPS * HEAD_DIM
XATTN_W = XATTN_HEADS * XATTN_HEAD_DIM
GATE_PAD = LANES

COL_GA = 0
COL_GB = COL_GA + D_MODEL
COL_MK = COL_GB + D_MODEL
COL_NKS = COL_MK + MOBA_W
COL_NKW = COL_NKS + NSA_KV_W
COL_NGATE = COL_NKW + NSA_KV_W
TOK_W = COL_NGATE + GATE_PAD
CMP_W = 2 * NSA_KV_W
ROW_MQ = 0
ROW_MV = ROW_MQ + MOBA_W
ROW_NQ = ROW_MV + MOBA_W
ROW_NVS = ROW_NQ + NSA_W
ROW_NVW = ROW_NVS + NSA_KV_W
FM_W = ROW_NVW + NSA_KV_W

BT_OWN, BT_PREV, BT_UPPER, BT_MASKED = 0, 1, 2, 3
N_BIAS_TILES = 4


def _bucket_thresholds():
    n = np.arange(0, 4 * REL_MAX_DIST)
    exact = REL_BUCKETS // 2
    nf = np.maximum(n, 1).astype(np.float64)
    large = exact + (np.log(nf / exact) / math.log(REL_MAX_DIST / exact) * (REL_BUCKETS - exact)).astype(np.int64)
    large = np.minimum(large, REL_BUCKETS - 1)
    b = np.where(n < exact, n, large)
    return [int(np.argmax(b >= k)) for k in range(REL_BUCKETS)]


_THR = _bucket_thresholds()
assert _THR[REL_BUCKETS - 1] <= LANES


def _dot(a, b):
    return jnp.dot(a, b, preferred_element_type=F32)


def _dot_nt(a, b):
    return lax.dot_general(a, b, (((1,), (1,)), ((), ())), preferred_element_type=F32)


def _rms(x, g):
    return x * lax.rsqrt(jnp.mean(x * x, axis=-1, keepdims=True) + RMS_EPS) * g


def _params(n_axes):
    return pltpu.CompilerParams(dimension_semantics=("arbitrary",) * n_axes,
                                vmem_limit_bytes=VMEM_LIMIT)


def _bias_kernel(rb_ref, out_ref):
    h = pl.program_id(0)
    j = lax.broadcasted_iota(jnp.int32, (KT, TQ), 0)
    i = lax.broadcasted_iota(jnp.int32, (KT, TQ), 1)
    far = rb_ref[REL_BUCKETS - 1, h]

    def table(d):
        t = jnp.full((KT, TQ), (rb_ref[0, h] - far) * LOG2E, F32)
        for k in range(1, REL_BUCKETS):
            t = jnp.where(d >= _THR[k], (rb_ref[k, h] - far) * LOG2E, t)
        return t

    neg = jnp.full((KT, TQ), NEG_BIG, F32)
    out_ref[0, BT_OWN] = jnp.where(i >= j, table(i - j), neg)
    out_ref[0, BT_PREV] = table(i - j + KT)
    out_ref[0, BT_UPPER] = jnp.where(j > i, 0.0, neg)
    out_ref[0, BT_MASKED] = neg


def _bias_tiles(rel_bias):
    nh = rel_bias.shape[1]
    return pl.pallas_call(
        _bias_kernel,
        grid=(nh,),
        in_specs=[pl.BlockSpec(memory_space=pltpu.SMEM)],
        out_specs=pl.BlockSpec((1, N_BIAS_TILES, KT, TQ), lambda h: (h, 0, 0, 0)),
        out_shape=jax.ShapeDtypeStruct((nh, N_BIAS_TILES, KT, TQ), F32),
        compiler_params=_params(1),
        name="bias_tiles",
    )(rel_bias)


def _inproj_kernel(x_ref, g_ref, wt_ref, wf_ref, wc_ref, tok_ref, fm_ref, cmp_ref, *, chunk):
    h = _rms(x_ref[...], g_ref[...]).astype(BF16)
    c = _dot(h, wc_ref[...]).astype(cmp_ref.dtype)
    for i in range(cmp_ref.shape[0]):
        cmp_ref[i] = c[:, i * HEAD_DIM:(i + 1) * HEAD_DIM]
    n = wt_ref.shape[1]
    for c0 in range(0, n, chunk):
        c1 = min(c0 + chunk, n)
        tok_ref[:, c0:c1] = _dot(h, wt_ref[:, c0:c1]).astype(tok_ref.dtype)
    n = wf_ref.shape[0]
    for c0 in range(0, n, chunk):
        c1 = min(c0 + chunk, n)
        fm_ref[c0:c1, :] = _dot_nt(wf_ref[c0:c1, :], h).astype(fm_ref.dtype)


def _in_proj(x, g, w_tok, w_fm_t, w_cmp, tm):
    b, s, d = x.shape
    ncmp = CMP_W // HEAD_DIM
    return pl.pallas_call(
        functools.partial(_inproj_kernel, chunk=512),
        grid=(b, s // tm),
        in_specs=[pl.BlockSpec((None, tm, d), lambda i, j: (i, j, 0)),
                  pl.BlockSpec((1, d), lambda i, j: (0, 0)),
                  pl.BlockSpec(w_tok.shape, lambda i, j: (0, 0)),
                  pl.BlockSpec(w_fm_t.shape, lambda i, j: (0, 0)),
                  pl.BlockSpec(w_cmp.shape, lambda i, j: (0, 0))],
        out_specs=[pl.BlockSpec((None, tm, TOK_W), lambda i, j: (i, j, 0)),
                   pl.BlockSpec((None, FM_W, tm), lambda i, j: (i, 0, j)),
                   pl.BlockSpec((None, ncmp, tm, HEAD_DIM), lambda i, j: (i, 0, j, 0))],
        out_shape=[jax.ShapeDtypeStruct((b, s, TOK_W), BF16),
                   jax.ShapeDtypeStruct((b, FM_W, s), BF16),
                   jax.ShapeDtypeStruct((b, ncmp, s, HEAD_DIM), BF16)],
        compiler_params=_params(2),
        name="in_proj",
    )(x, g.reshape(1, d), w_tok, w_fm_t, w_cmp)


def _norm_matmul_kernel(x_ref, g_ref, w_ref, o_ref):
    o_ref[...] = _dot(_rms(x_ref[...], g_ref[...]).astype(BF16), w_ref[...]).astype(o_ref.dtype)


def _norm_matmul(x2d, g, w, tm, name):
    m, d = x2d.shape
    n = w.shape[1]
    return pl.pallas_call(
        _norm_matmul_kernel,
        grid=(m // tm,),
        in_specs=[pl.BlockSpec((tm, d), lambda i: (i, 0)),
                  pl.BlockSpec((1, d), lambda i: (0, 0)),
                  pl.BlockSpec((d, n), lambda i: (0, 0))],
        out_specs=pl.BlockSpec((tm, n), lambda i: (i, 0)),
        out_shape=jax.ShapeDtypeStruct((m, n), BF16),
        compiler_params=_params(1),
        name=name,
    )(x2d, g.reshape(1, d), w)


def _kmean_kernel(k_ref, o_ref):
    nb = o_ref.shape[0]
    for n in range(nb):
        blk = k_ref[n * MOBA_BLOCK:(n + 1) * MOBA_BLOCK, :].astype(F32)
        o_ref[n:n + 1, :] = jnp.mean(blk, axis=0, keepdims=True)


def _moba_kmean(tok):
    b, s, _ = tok.shape
    nb = s // MOBA_BLOCK
    return pl.pallas_call(
        _kmean_kernel,
        grid=(b,),
        in_specs=[pl.BlockSpec((None, s, MOBA_W), lambda i: (i, 0, COL_MK // MOBA_W))],
        out_specs=pl.BlockSpec((None, nb, MOBA_W), lambda i: (i, 0, 0)),
        out_shape=jax.ShapeDtypeStruct((b, nb, MOBA_W), F32),
        compiler_params=_params(1),
        name="moba_kmean",
    )(tok)


def _fill_v_with_ones(vt_ref, lo_ref, hi_ref, row0=0, cols=1024):
    s = vt_ref.shape[1]
    ones = jnp.ones((ONES_ROWS, cols), BF16)

    def body(c, carry):
        c0 = pl.multiple_of(c * cols, cols)
        lo_ref[0:HEAD_DIM, pl.ds(c0, cols)] = vt_ref[row0:row0 + HEAD_DIM, pl.ds(c0, cols)]
        lo_ref[HEAD_DIM:VROWS, pl.ds(c0, cols)] = ones
        hi_ref[0:HEAD_DIM, pl.ds(c0, cols)] = vt_ref[row0 + HEAD_DIM:row0 + 2 * HEAD_DIM, pl.ds(c0, cols)]
        hi_ref[HEAD_DIM:VROWS, pl.ds(c0, cols)] = ones
        return carry

    lax.fori_loop(0, s // cols, body, 0)


def _split_heads_t(q_t):
    row = lax.broadcasted_iota(jnp.int32, q_t.shape, 0)
    zero = jnp.zeros_like(q_t)
    return jnp.where(row < HEAD_DIM, q_t, zero), jnp.where(row >= HEAD_DIM, q_t, zero)


def _fill_k_with_block_ids(k_ref, lo_ref, hi_ref, block, nblock, lane0=0, rows=512):
    s = k_ref.shape[0]
    lane = lax.broadcasted_iota(jnp.int32, (rows, LANES), 1)
    shift = int(math.log2(block))

    def body(c, carry):
        r0 = pl.multiple_of(c * rows, rows)
        key = r0 + lax.broadcasted_iota(jnp.int32, (rows, LANES), 0)
        bid = lax.shift_right_logical(key, shift) & (nblock - 1)
        k = k_ref[pl.ds(r0, rows), lane0:lane0 + LANES]
        lo_ref[pl.ds(r0, rows), :] = jnp.where(lane < HEAD_DIM, k, (lane - HEAD_DIM == bid).astype(BF16))
        hi_ref[pl.ds(r0, rows), :] = jnp.where(lane >= HEAD_DIM, k, (lane == bid).astype(BF16))
        return carry

    lax.fori_loop(0, s // rows, body, 0)


def _q_with_mask_rows(q_rows, mask_rows, low_head):
    r = lax.broadcasted_iota(jnp.int32, (ONES_ROWS, TQ), 0)
    m = jnp.zeros((ONES_ROWS, TQ), F32)
    for b, row in enumerate(mask_rows):
        m = jnp.where(r == b, row, m)
    pad = jnp.zeros((HEAD_DIM - ONES_ROWS, TQ), BF16)
    parts = [q_rows, m.astype(BF16), pad] if low_head else [m.astype(BF16), pad, q_rows]
    return jnp.concatenate(parts, axis=0)


def _pipe(bufs, m_ref, acc_ref, nheads, a=None, b=None, c=None, c_last=False, group=1):
    s_refs, p_refs, mx_ref, al_ref = bufs
    rows = s_refs[0].shape[1]

    def stage_c(h):
        c_slot, c_base, vt_fn = c
        acc_ref[c_base + h] = (al_ref[c_slot, h] * acc_ref[c_base + h]
                               + _dot(vt_fn(h), p_refs[c_slot][h]))

    for g0 in range(0, nheads, group):
        hs = range(g0, min(g0 + group, nheads))
        if c is not None and not c_last:
            for h in hs:
                stage_c(h)
        if a is not None:
            a_slot, score_fn, bias_fn = a
            raws = {h: score_fn(h) for h in hs}
        for h in hs:
            if b is not None:
                b_slot, b_base = b
                m_prev = m_ref[b_base + h]
                m_next = jnp.maximum(m_prev, mx_ref[b_slot, h])
                al_ref[b_slot, h] = jnp.exp2(m_prev - m_next)
                m_ref[b_base + h] = m_next
            mx = None
            strips = [slice(i * STRIP, (i + 1) * STRIP) for i in range(rows // STRIP)]
            if b is not None:
                for r in strips:
                    p_refs[b_slot][h, r, :] = jnp.exp2(s_refs[b_slot][h, r, :] - m_next).astype(BF16)
            if a is not None:
                for i, r in enumerate(strips):
                    s = raws[h][r, :] if bias_fn is None else raws[h][r, :] + bias_fn(h, i)
                    s_refs[a_slot][h, r, :] = s
                    cmax = jnp.max(s, axis=0, keepdims=True)
                    mx = cmax if mx is None else jnp.maximum(mx, cmax)
            if a is not None:
                mx_ref[a_slot, h] = mx
        if c is not None and c_last:
            for h in hs:
                stage_c(h)


def _reset_state(bufs, m_ref, acc_ref):
    _, p_refs, _, al_ref = bufs
    m_ref[...] = jnp.full(m_ref.shape, NEG_BIG, F32)
    acc_ref[...] = jnp.zeros(acc_ref.shape, F32)
    p_refs[1][...] = jnp.zeros(p_refs[1].shape, BF16)
    al_ref[...] = jnp.ones(al_ref.shape, F32)


def _pair_output(acc_lo, acc_hi):
    o_lo = acc_lo[0:HEAD_DIM, :] / acc_lo[HEAD_DIM:HEAD_DIM + 1, :]
    o_hi = acc_hi[0:HEAD_DIM, :] / acc_hi[HEAD_DIM:HEAD_DIM + 1, :]
    return jnp.transpose(jnp.concatenate([o_lo, o_hi], axis=0))


def _top_k_rows(val, n_f, k, sentinel):
    chosen = jnp.zeros(val.shape, F32)
    for _ in range(k):
        mx = jnp.max(val, axis=0, keepdims=True)
        idx = jnp.min(jnp.where(val == mx, n_f, sentinel), axis=0, keepdims=True)
        pick = n_f == idx
        chosen = jnp.where(pick, 1.0, chosen)
        val = jnp.where(pick, -jnp.inf, val)
    return chosen


def _moba_kernel(qt_ref, k_ref, vt_ref, km_ref, bt_ref, o_ref, *scratch):
    npair = qt_ref.shape[0] // LANES
    nh = 2 * npair
    k_refs, v_refs = scratch[0:nh], scratch[nh:2 * nh]
    mb_ref, qs_ref, m_ref, acc_ref, s0_ref, s1_ref, p0_ref, p1_ref, mx_ref, al_ref = scratch[2 * nh:]
    cur = pl.program_id(2)
    bufs = ((s0_ref, s1_ref), (p0_ref, p1_ref), mx_ref, al_ref)
    nb = km_ref.shape[0]
    sk = s0_ref.shape[1]
    per_block = KT // STRIP

    @pl.when(cur == 0)
    def _():
        for pp in range(npair):
            _fill_v_with_ones(vt_ref, v_refs[2 * pp], v_refs[2 * pp + 1], row0=pp * LANES)
            _fill_k_with_block_ids(k_ref, k_refs[2 * pp], k_refs[2 * pp + 1], MOBA_BLOCK, 2, lane0=pp * LANES)

    n_f = lax.broadcasted_iota(jnp.int32, (nb, TQ), 0).astype(F32)
    past = n_f < cur.astype(F32)
    for pp in range(npair):
        km = km_ref[:, pp * LANES:(pp + 1) * LANES].astype(BF16)
        for hh, qh in enumerate(_split_heads_t(qt_ref[pp * LANES:(pp + 1) * LANES, :])):
            gate = jnp.where(past, _dot(km, qh), -jnp.inf)
            chosen = _top_k_rows(gate, n_f, min(MOBA_TOPK, nb), float(nb))
            mb_ref[2 * pp + hh] = jnp.where((chosen > 0.5) & past, 0.0, NEG_BIG)
            qs_ref[2 * pp + hh] = qh
    _reset_state(bufs, m_ref, acc_ref)

    def first_block(u):
        return jnp.maximum(cur - 2 * u - 1, 0)

    def score_fn(u, first_rows, second_rows):
        idx = first_block(u)
        ks = pl.multiple_of(idx * KT, KT)
        first_even = (idx & 1) == 0

        def f(h):
            low = h % 2 == 0
            even = jnp.where(first_even, first_rows[h], second_rows[h])
            odd = jnp.where(first_even, second_rows[h], first_rows[h])
            q_rows = qs_ref[h, 0:HEAD_DIM, :] if low else qs_ref[h, HEAD_DIM:2 * HEAD_DIM, :]
            return _dot(k_refs[h][pl.ds(ks, sk), :], _q_with_mask_rows(q_rows, (even, odd), low))
        return f

    def vt_fn(u):
        ks = pl.multiple_of(first_block(jnp.maximum(u, 0)) * KT, KT)
        return lambda h: v_refs[h][:, pl.ds(ks, sk)]

    has_prev = cur >= 1
    kind_lo = jnp.where(has_prev, BT_PREV, BT_OWN)
    kind_hi = jnp.where(has_prev, BT_OWN, BT_MASKED)
    keep = jnp.where(has_prev, 1.0, 0.0)
    prev_rows = [mb_ref[h, pl.ds(jnp.maximum(cur - 1, 0), 1), :] * keep for h in range(nh)]
    no_rows = [jnp.zeros((1, TQ), F32)] * nh

    def near_bias(h, i):
        half, part = divmod(i, per_block)
        return bt_ref[h, kind_lo if half == 0 else kind_hi, part * STRIP:(part + 1) * STRIP, :]

    def far_scores(u):
        b_lo = cur - 2 * u - 1
        idx = jnp.maximum(b_lo, 0)
        pen_lo = jnp.where(b_lo >= -1, 0.0, NEG_BIG)
        pen_hi = jnp.where(b_lo >= 0, 0.0, NEG_BIG)
        return score_fn(u, [mb_ref[h, pl.ds(idx, 1), :] + pen_lo for h in range(nh)],
                        [mb_ref[h, pl.ds(idx + 1, 1), :] + pen_hi for h in range(nh)])

    pipe = functools.partial(_pipe, bufs, m_ref, acc_ref, nh, c_last=True)
    pipe(a=(0, score_fn(0, prev_rows, no_rows), near_bias))
    n_far = lax.shift_right_logical(cur, 1)

    def two_jobs(t, carry):
        u0 = 2 * t + 1
        pipe(a=(1, far_scores(u0), None), b=(0, 0), c=(1, 0, vt_fn(u0 - 2)))
        pipe(a=(0, far_scores(u0 + 1), None), b=(1, 0), c=(0, 0, vt_fn(u0 - 1)))
        return carry

    n_iter = lax.shift_right_logical(n_far + 1, 1)
    lax.fori_loop(0, n_iter, two_jobs, 0)
    pipe(b=(0, 0), c=(1, 0, vt_fn(2 * n_iter - 1)))
    pipe(c=(0, 0, vt_fn(2 * n_iter)))
    for pp in range(npair):
        o_ref[:, pp * LANES:(pp + 1) * LANES] = _pair_output(
            acc_ref[2 * pp], acc_ref[2 * pp + 1]).astype(o_ref.dtype)


def _moba_attention(tok, fm, kmean, bias_tiles, pairs_per_step=2):
    b, s, _ = tok.shape
    nb = s // MOBA_BLOCK
    nh = 2 * pairs_per_step
    w = pairs_per_step * LANES
    sk = 2 * KT
    return pl.pallas_call(
        _moba_kernel,
        grid=(b, MOBA_W // w, s // TQ),
        in_specs=[
            pl.BlockSpec((None, w, TQ), lambda i, p, c: (i, ROW_MQ // w + p, c)),
            pl.BlockSpec((None, s, w), lambda i, p, c: (i, 0, COL_MK // w + p)),
            pl.BlockSpec((None, w, s), lambda i, p, c: (i, ROW_MV // w + p, 0)),
            pl.BlockSpec((None, nb, w), lambda i, p, c: (i, 0, p)),
            pl.BlockSpec((nh, N_BIAS_TILES, KT, TQ), lambda i, p, c: (p, 0, 0, 0)),
        ],
        out_specs=pl.BlockSpec((None, TQ, w), lambda i, p, c: (i, c, p)),
        out_shape=jax.ShapeDtypeStruct((b, s, MOBA_W), BF16),
        scratch_shapes=[pltpu.VMEM((s, LANES), BF16)] * nh + [pltpu.VMEM((VROWS, s), BF16)] * nh + [
            pltpu.VMEM((nh, nb, TQ), F32),
            pltpu.VMEM((nh, LANES, TQ), BF16),
            pltpu.VMEM((nh, 1, TQ), F32),
            pltpu.VMEM((nh, VROWS, TQ), F32),
            pltpu.VMEM((nh, sk, TQ), F32), pltpu.VMEM((nh, sk, TQ), F32),
            pltpu.VMEM((nh, sk, TQ), BF16), pltpu.VMEM((nh, sk, TQ), BF16),
            pltpu.VMEM((2, nh, 1, TQ), F32), pltpu.VMEM((2, nh, 1, TQ), F32),
        ],
        compiler_params=_params(3),
        name="moba_attention",
    )(fm, tok, fm, kmean, bias_tiles)


def _compress_kernel(ck_ref, cv_ref, pk_ref, pv_ref, w1k_ref, w1v_ref, w2k_ref, w2v_ref,
                     kc_ref, vct_ref):
    nch = ck_ref.shape[1]
    half = CMP_STRIDE * HEAD_DIM

    def run(c_ref, pos_ref, w1_ref, w2_ref):
        acc = jnp.zeros((nch, LANES), F32)
        for g in range(NSA_GROUPS):
            c = c_ref[g].astype(F32)
            lo = (c + pos_ref[0:1, :]).astype(BF16)
            hi = (c + pos_ref[1:2, :]).astype(BF16)
            pre = _dot(lo, w1_ref[:half, :]) + pltpu.roll(_dot(hi, w1_ref[half:, :]), nch - 1, axis=0)
            acc = acc + _dot(jax.nn.gelu(pre).astype(BF16), w2_ref[g])
        return acc

    kc_ref[...] = run(ck_ref, pk_ref, w1k_ref, w2k_ref).astype(kc_ref.dtype)
    vct_ref[...] = jnp.transpose(run(cv_ref, pv_ref, w1v_ref, w2v_ref)).astype(vct_ref.dtype)


def _compress(chunks, pos_k, pos_v, w1k, w1v, w2k, w2v):
    b, _, nch, width = chunks.shape
    g = NSA_GROUPS
    full = lambda shape: pl.BlockSpec(shape, lambda i: (0,) * len(shape))
    return pl.pallas_call(
        _compress_kernel,
        grid=(b,),
        in_specs=[pl.BlockSpec((None, g, nch, width), lambda i: (i, 0, 0, 0)),
                  pl.BlockSpec((None, g, nch, width), lambda i: (i, 1, 0, 0)),
                  full(pos_k.shape), full(pos_v.shape), full(w1k.shape), full(w1v.shape),
                  full(w2k.shape), full(w2v.shape)],
        out_specs=[pl.BlockSpec((None, nch, LANES), lambda i: (i, 0, 0)),
                   pl.BlockSpec((None, LANES, nch), lambda i: (i, 0, 0))],
        out_shape=[jax.ShapeDtypeStruct((b, nch, LANES), BF16),
                   jax.ShapeDtypeStruct((b, LANES, nch), BF16)],
        compiler_params=_params(1),
        name="nsa_compress",
    )(chunks, chunks, pos_k, pos_v, w1k, w1v, w2k, w2v)


def _cmp_select_kernel(qt_ref, kc_ref, vct_ref, mimp_ref, o_ref, sel_ref, *, nsb, nq):
    ci = pl.program_id(1)
    nch = kc_ref.shape[0]
    nsbp = mimp_ref.shape[0]
    parts = 4 if (nch // 4) % LANES == 0 and (nsbp // 4) % 8 == 0 else 1
    for v in range(1, parts + 1):
        @pl.when((ci * parts >= (v - 1) * nq) & (ci * parts < v * nq))
        def _(v=v):
            _cmp_select_body(qt_ref, kc_ref, vct_ref, mimp_ref, o_ref, sel_ref,
                             nsb=nsb, nch=nch * v // parts, nrow=nsbp * v // parts)


def _cmp_select_body(qt_ref, kc_ref, vct_ref, mimp_ref, o_ref, sel_ref, *, nsb, nch, nrow):
    qs = pl.program_id(1) * TQ
    n_row = lax.broadcasted_iota(jnp.int32, (nch, TQ), 0)
    t_col = qs + lax.broadcasted_iota(jnp.int32, (nch, TQ), 1)
    hidden = jnp.where((n_row * CMP_STRIDE + (CMP_BLOCK - 1)) <= t_col, 0.0, NEG_BIG)
    any_visible = t_col[0:1, :] >= (CMP_BLOCK - 1)
    kc = kc_ref[0:nch, :]
    vct = vct_ref[:, 0:nch]
    vrow = lax.broadcasted_iota(jnp.int32, vct.shape, 0)
    mimp = mimp_ref[0:nrow, 0:nch]

    j_row = lax.broadcasted_iota(jnp.int32, (nrow, TQ), 0)
    t_sel = qs + lax.broadcasted_iota(jnp.int32, (nrow, TQ), 1)
    cur = lax.shift_right_logical(t_sel, int(math.log2(SLC_BLOCK)))
    j_f = j_row.astype(F32)
    forced = (j_row == 0) | (j_row == cur) | (j_row == cur - 1)
    allowed = (j_row <= cur) & (j_row < nsb)

    q_heads = [_split_heads_t(qt_ref[r * LANES:(r + 1) * LANES, :]) for r in range(NSA_HPG)]
    out_t = [jnp.zeros((LANES, TQ), F32) for _ in range(NSA_HPG)]
    for g in range(NSA_GROUPS):
        vin_g = (vrow < HEAD_DIM) if g == 0 else (vrow >= HEAD_DIM)
        vct_g = jnp.where(vin_g, vct, jnp.zeros_like(vct))
        p_grp = jnp.zeros((nch, TQ), F32)
        for r in range(NSA_HPG):
            s = _dot(kc, q_heads[r][g]) + hidden
            e = jnp.exp2(s - jnp.max(s, axis=0, keepdims=True))
            p = e * jnp.where(any_visible, 1.0 / jnp.sum(e, axis=0, keepdims=True), 0.0)
            p_grp = p_grp + p
            out_t[r] = out_t[r] + _dot(vct_g, p.astype(BF16))
        p_hi = p_grp.astype(BF16)
        p_lo = (p_grp - p_hi.astype(F32)).astype(BF16)
        imp = _dot(mimp, p_hi) + _dot(mimp, p_lo)
        val = jnp.where(forced & allowed, jnp.inf, jnp.where(allowed, imp, -jnp.inf))
        chosen = _top_k_rows(val, j_f, min(SLC_TOPN, nsb), float(nrow))
        sel_ref[g, 0:nrow, :] = jnp.where(allowed & (chosen > 0.5), 0.0, NEG_BIG)
        if nrow < sel_ref.shape[1]:
            sel_ref[g, nrow:, :] = jnp.full((sel_ref.shape[1] - nrow, TQ), NEG_BIG, F32)
    for r in range(NSA_HPG):
        o_ref[:, r * LANES:(r + 1) * LANES] = jnp.transpose(out_t[r]).astype(o_ref.dtype)


def _cmp_select(fm, kc, vct, mimp_t, nsb):
    b, _, s = fm.shape
    nch = kc.shape[1]
    nsbp = mimp_t.shape[0]
    return pl.pallas_call(
        functools.partial(_cmp_select_kernel, nsb=nsb, nq=s // TQ),
        grid=(b, s // TQ),
        in_specs=[pl.BlockSpec((None, NSA_W, TQ), lambda i, c: (i, ROW_NQ // NSA_W, c)),
                  pl.BlockSpec((None, nch, LANES), lambda i, c: (i, 0, 0)),
                  pl.BlockSpec((None, LANES, nch), lambda i, c: (i, 0, 0)),
                  pl.BlockSpec((nsbp, nch), lambda i, c: (0, 0))],
        out_specs=[pl.BlockSpec((None, TQ, NSA_W), lambda i, c: (i, c, 0)),
                   pl.BlockSpec((None, NSA_GROUPS, nsbp, TQ), lambda i, c: (i, 0, 0, c))],
        out_shape=[jax.ShapeDtypeStruct((b, s, NSA_W), BF16),
                   jax.ShapeDtypeStruct((b, NSA_GROUPS, nsbp, s), F32)],
        compiler_params=_params(2),
        name="nsa_cmp_select",
    )(fm, kc, vct, mimp_t)


def _slc_win_kernel(qt_ref, ks_ref, vst_ref, kw_ref, vwt_ref, sel_ref, bt_ref,
                    oslc_ref, owin_ref,
                    ks0_ref, ks1_ref, vs0_ref, vs1_ref, vw0_ref, vw1_ref, qs_ref, m_ref, acc_ref,
                    s0_ref, s1_ref, p0_ref, p1_ref, mx_ref, al_ref):
    cur = pl.program_id(1)
    bufs = ((s0_ref, s1_ref), (p0_ref, p1_ref), mx_ref, al_ref)
    per_tile = KT // SLC_BLOCK

    @pl.when(cur == 0)
    def _():
        _fill_v_with_ones(vst_ref, vs0_ref, vs1_ref)
        _fill_v_with_ones(vwt_ref, vw0_ref, vw1_ref)
        _fill_k_with_block_ids(ks_ref, ks0_ref, ks1_ref, SLC_BLOCK, per_tile)

    nh = NSA_HEADS
    for r in range(NSA_HPG):
        lo, hi = _split_heads_t(qt_ref[r * LANES:(r + 1) * LANES, :])
        qs_ref[r] = lo
        qs_ref[NSA_HPG + r] = hi
    _reset_state(bufs, m_ref, acc_ref)
    ks_refs = (ks0_ref, ks1_ref)
    vs_refs = (vs0_ref, vs1_ref)
    vw_refs = (vw0_ref, vw1_ref)
    sel_base, win_base = 0, nh

    def tile_start(n):
        return pl.multiple_of(jnp.maximum(n, 0) * KT, KT)

    def win_scores(n):
        ks = tile_start(n)
        return lambda h: _dot(kw_ref[pl.ds(ks, KT), :], qs_ref[h])

    def sel_scores(n, penalty):
        ks = tile_start(n)
        base = jnp.maximum(n, 0) * per_tile
        rows = [[sel_ref[g, pl.ds(base + b, 1), :] + penalty for b in range(per_tile)]
                for g in range(NSA_GROUPS)]

        def f(h):
            g = h // NSA_HPG
            q_rows = qs_ref[h, 0:HEAD_DIM, :] if g == 0 else qs_ref[h, HEAD_DIM:2 * HEAD_DIM, :]
            return _dot(ks_refs[g][pl.ds(ks, KT), :], _q_with_mask_rows(q_rows, rows[g], g == 0))
        return f

    def vt_fn(v_refs, n):
        ks = tile_start(n)
        return lambda h: v_refs[h // NSA_HPG][:, pl.ds(ks, KT)]

    def near_bias(kind):
        return lambda h, i: bt_ref[h, kind, i * STRIP:(i + 1) * STRIP, :]

    pipe = functools.partial(_pipe, bufs, m_ref, acc_ref, nh, c_last=True)
    kind_prev = jnp.where(cur >= 1, BT_PREV, BT_MASKED)
    kind_upper = jnp.where(cur >= 2, BT_UPPER, BT_MASKED)
    pipe(a=(0, win_scores(cur), near_bias(BT_OWN)))
    pipe(a=(1, win_scores(cur - 1), near_bias(kind_prev)), b=(0, win_base))
    pipe(a=(0, win_scores(cur - 2), near_bias(kind_upper)), b=(1, win_base),
         c=(0, win_base, vt_fn(vw_refs, cur)))
    pipe(a=(1, sel_scores(cur, 0.0), near_bias(BT_OWN)), b=(0, win_base),
         c=(1, win_base, vt_fn(vw_refs, cur - 1)))
    pipe(a=(0, sel_scores(cur - 1, 0.0), near_bias(kind_prev)), b=(1, sel_base),
         c=(0, win_base, vt_fn(vw_refs, cur - 2)))

    def far_scores(n):
        return sel_scores(n, jnp.where(n >= 0, 0.0, NEG_BIG))

    def two_tiles(t, carry):
        n0 = cur - 2 - 2 * t
        pipe(a=(1, far_scores(n0), None), b=(0, sel_base), c=(1, sel_base, vt_fn(vs_refs, n0 + 2)))
        pipe(a=(0, far_scores(n0 - 1), None), b=(1, sel_base), c=(0, sel_base, vt_fn(vs_refs, n0 + 1)))
        return carry

    n_iter = lax.shift_right_logical(jnp.maximum(cur - 1, 0) + 1, 1)
    lax.fori_loop(0, n_iter, two_tiles, 0)
    pipe(b=(0, sel_base), c=(1, sel_base, vt_fn(vs_refs, cur - 2 * n_iter)))
    pipe(c=(0, sel_base, vt_fn(vs_refs, cur - 1 - 2 * n_iter)))

    for r in range(NSA_HPG):
        cols = slice(r * LANES, (r + 1) * LANES)
        oslc_ref[:, cols] = _pair_output(acc_ref[r], acc_ref[NSA_HPG + r]).astype(oslc_ref.dtype)
        owin_ref[:, cols] = _pair_output(acc_ref[nh + r], acc_ref[nh + NSA_HPG + r]).astype(owin_ref.dtype)


def _slc_win(tok, fm, sel, bias_tiles):
    b, s, _ = tok.shape
    nsbp = sel.shape[2]
    return pl.pallas_call(
        _slc_win_kernel,
        grid=(b, s // TQ),
        in_specs=[pl.BlockSpec((None, NSA_W, TQ), lambda i, c: (i, ROW_NQ // NSA_W, c)),
                  pl.BlockSpec((None, s, LANES), lambda i, c: (i, 0, COL_NKS // LANES)),
                  pl.BlockSpec((None, LANES, s), lambda i, c: (i, ROW_NVS // LANES, 0)),
                  pl.BlockSpec((None, s, LANES), lambda i, c: (i, 0, COL_NKW // LANES)),
                  pl.BlockSpec((None, LANES, s), lambda i, c: (i, ROW_NVW // LANES, 0)),
                  pl.BlockSpec((None, NSA_GROUPS, nsbp, TQ), lambda i, c: (i, 0, 0, c)),
                  pl.BlockSpec((NSA_HEADS, N_BIAS_TILES, KT, TQ), lambda i, c: (1, 0, 0, 0))],
        out_specs=[pl.BlockSpec((None, TQ, NSA_W), lambda i, c: (i, c, 0)),
                   pl.BlockSpec((None, TQ, NSA_W), lambda i, c: (i, c, 0))],
        out_shape=[jax.ShapeDtypeStruct((b, s, NSA_W), BF16),
                   jax.ShapeDtypeStruct((b, s, NSA_W), BF16)],
        scratch_shapes=[pltpu.VMEM((s, LANES), BF16)] * 2 + [pltpu.VMEM((VROWS, s), BF16)] * 4 + [
            pltpu.VMEM((NSA_HEADS, LANES, TQ), BF16),
            pltpu.VMEM((2 * NSA_HEADS, 1, TQ), F32),
            pltpu.VMEM((2 * NSA_HEADS, VROWS, TQ), F32),
            pltpu.VMEM((NSA_HEADS, KT, TQ), F32), pltpu.VMEM((NSA_HEADS, KT, TQ), F32),
            pltpu.VMEM((NSA_HEADS, KT, TQ), BF16), pltpu.VMEM((NSA_HEADS, KT, TQ), BF16),
            pltpu.VMEM((2, NSA_HEADS, 1, TQ), F32), pltpu.VMEM((2, NSA_HEADS, 1, TQ), F32)],
        compiler_params=_params(2),
        name="nsa_slc_win",
    )(fm, tok, fm, tok, fm, sel, bias_tiles)


def _merge_kernel(x_ref, oa_ref, oc_ref, os_ref, ow_ref, ng_ref, ga_ref, gb_ref,
                  eg_ref, wa_ref, wb_ref, wo_ref, o_ref):
    sig = jax.nn.sigmoid(ng_ref[...].astype(F32))
    s_hi = sig.astype(BF16)
    s_lo = (sig - s_hi.astype(F32)).astype(BF16)
    eg = eg_ref[...]
    gexp = _dot(s_hi, eg) + _dot(s_lo, eg)
    a = _dot(oa_ref[...], wa_ref[...])
    o_b = (gexp[:, :NSA_W] * oc_ref[...].astype(F32)
           + gexp[:, NSA_W:2 * NSA_W] * os_ref[...].astype(F32)
           + gexp[:, 2 * NSA_W:] * ow_ref[...].astype(F32))
    bb = _dot(o_b.astype(BF16), wb_ref[...])
    merged = (jax.nn.sigmoid(ga_ref[...].astype(F32)) * a
              + jax.nn.sigmoid(gb_ref[...].astype(F32)) * bb)
    o_ref[...] = x_ref[...] + _dot(merged.astype(BF16), wo_ref[...])


def _merge(x2d, tok2d, o_a, o_c, o_s, o_w, egate, wa, wb, wo, tm):
    m, d = x2d.shape
    row = lambda w, col=0: pl.BlockSpec((tm, w), lambda i, col=col: (i, col))
    full = lambda a: pl.BlockSpec(a.shape, lambda i: (0, 0))
    return pl.pallas_call(
        _merge_kernel,
        grid=(m // tm,),
        in_specs=[row(d), row(MOBA_W), row(NSA_W), row(NSA_W), row(NSA_W),
                  row(GATE_PAD, COL_NGATE // GATE_PAD),
                  row(D_MODEL, COL_GA // D_MODEL), row(D_MODEL, COL_GB // D_MODEL),
                  full(egate), full(wa), full(wb), full(wo)],
        out_specs=row(d),
        out_shape=jax.ShapeDtypeStruct((m, d), F32),
        compiler_params=_params(1),
        name="merge_out",
    )(x2d, o_a, o_c, o_s, o_w, tok2d, tok2d, tok2d, egate, wa, wb, wo)


def _xattn_kernel(x_ref, g_ref, wq_ref, kv_ref, wo_ref, o_ref, *, nsplit):
    part = x_ref.shape[0] // nsplit
    g = g_ref[...]
    rows = [slice(i * part, (i + 1) * part) for i in range(nsplit)]
    xs = [x_ref[r, :] for r in rows]
    qs = [_dot(_rms(x, g).astype(BF16), wq_ref[...]).astype(BF16) for x in xs]
    for r, x, q in zip(rows, xs, qs):
        cols = [slice(h * XATTN_HEAD_DIM, (h + 1) * XATTN_HEAD_DIM) for h in range(XATTN_HEADS)]
        scores = [_dot_nt(q[:, c], kv_ref[:, c]) for c in cols]
        heads = []
        for h, s in enumerate(scores):
            v = kv_ref[:, XATTN_W + h * XATTN_HEAD_DIM:XATTN_W + (h + 1) * XATTN_HEAD_DIM]
            e = jnp.exp2(s - jnp.max(s, axis=1, keepdims=True))
            p = e * (1.0 / jnp.sum(e, axis=1, keepdims=True))
            heads.append(_dot(p.astype(BF16), v).astype(BF16))
        o_ref[r, :] = x + _dot(jnp.concatenate(heads, axis=1), wo_ref[...])


def _xattn(x3d, g, wq, memkv, wo, tm):
    b, s, d = x3d.shape
    mem_len = memkv.shape[1]
    return pl.pallas_call(
        functools.partial(_xattn_kernel, nsplit=1),
        grid=(b, s // tm),
        in_specs=[pl.BlockSpec((None, tm, d), lambda i, j: (i, j, 0)),
                  pl.BlockSpec((1, d), lambda i, j: (0, 0)),
                  pl.BlockSpec(wq.shape, lambda i, j: (0, 0)),
                  pl.BlockSpec((None, mem_len, 2 * XATTN_W), lambda i, j: (i, 0, 0)),
                  pl.BlockSpec(wo.shape, lambda i, j: (0, 0))],
        out_specs=pl.BlockSpec((None, tm, d), lambda i, j: (i, j, 0)),
        out_shape=jax.ShapeDtypeStruct((b, s, d), F32),
        compiler_params=_params(2),
        name="xattn",
    )(x3d, g.reshape(1, d), wq, memkv, wo)


def _ffn_kernel(x_ref, halo_ref, g_ref, wg_ref, wu_ref, cw_ref, cb_ref, wd_ref, gf_ref, o_ref,
                y_ref, *, chunk, halo_rows):
    j = pl.program_id(1)
    x = x_ref[...]
    tm = x.shape[0]
    g = g_ref[...]
    hf = _rms(x, g).astype(BF16)
    keep = jnp.where(j > 0, 1.0, 0.0)
    halo = (_rms(halo_ref[...], g) * keep).astype(BF16)
    ext = jnp.concatenate([halo, hf], axis=0)
    row = lax.broadcasted_iota(jnp.int32, (tm, chunk), 0)
    starts = list(range(0, D_FF, chunk))

    def gate_up(c0):
        cols = slice(c0, c0 + chunk)
        return _dot(ext, wg_ref[:, cols]), _dot(hf, wu_ref[:, cols])

    nxt = gate_up(starts[0])
    for k, c0 in enumerate(starts):
        cols = slice(c0, c0 + chunk)
        ue, up = nxt
        if k + 1 < len(starts):
            nxt = gate_up(starts[k + 1])
        u = ue[halo_rows:, :]
        prev1 = ue[halo_rows - 1:halo_rows, :]
        prev2 = ue[halo_rows - 2:halo_rows - 1, :]
        u1 = jnp.where(row == 0, prev1, pltpu.roll(u, 1, axis=0))
        u2 = jnp.where(row == 0, prev2, jnp.where(row == 1, prev1, pltpu.roll(u, 2, axis=0)))
        a = cw_ref[0:1, cols] * u2 + cw_ref[1:2, cols] * u1 + cw_ref[2:3, cols] * u + cb_ref[:, cols]
        y_ref[:, cols] = (jax.nn.gelu(a) * up).astype(BF16)
    o_ref[...] = _rms(x + _dot(y_ref[...], wd_ref[...]), gf_ref[...])


def _ffn(x3d, g, wg, wu, cw, cb, wd, gf, tm):
    b, s, d = x3d.shape
    halo_rows = 16
    per = tm // halo_rows
    return pl.pallas_call(
        functools.partial(_ffn_kernel, chunk=256, halo_rows=halo_rows),
        grid=(b, s // tm),
        in_specs=[pl.BlockSpec((None, tm, d), lambda i, j: (i, j, 0)),
                  pl.BlockSpec((None, halo_rows, d), lambda i, j: (i, jnp.maximum(j * per - 1, 0), 0)),
                  pl.BlockSpec((1, d), lambda i, j: (0, 0)),
                  pl.BlockSpec(wg.shape, lambda i, j: (0, 0)),
                  pl.BlockSpec(wu.shape, lambda i, j: (0, 0)),
                  pl.BlockSpec(cw.shape, lambda i, j: (0, 0)),
                  pl.BlockSpec((1, D_FF), lambda i, j: (0, 0)),
                  pl.BlockSpec(wd.shape, lambda i, j: (0, 0)),
                  pl.BlockSpec((1, d), lambda i, j: (0, 0))],
        out_specs=pl.BlockSpec((None, tm, d), lambda i, j: (i, j, 0)),
        out_shape=jax.ShapeDtypeStruct((b, s, d), F32),
        scratch_shapes=[pltpu.VMEM((tm, D_FF), BF16)],
        compiler_params=_params(2),
        name="conv_ffn",
    )(x3d, x3d, g.reshape(1, d), wg, wu, cw, cb.reshape(1, D_FF), wd, gf.reshape(1, d))


def _nq_perm():
    idx = []
    for r in range(NSA_HPG):
        for g in range(NSA_GROUPS):
            h = g * NSA_HPG + r
            idx.extend(range(h * HEAD_DIM, (h + 1) * HEAD_DIM))
    return np.asarray(idx, np.int32)


def _gate_expand():
    e = np.zeros((GATE_PAD, 3 * NSA_W), np.float32)
    for br in range(3):
        for g in range(NSA_GROUPS):
            for r in range(NSA_HPG):
                c0 = br * NSA_W + r * LANES + g * HEAD_DIM
                e[br * NSA_HEADS + g * NSA_HPG + r, c0:c0 + HEAD_DIM] = 1.0
    return e


def _importance_matrix(nch, nsb, nsbp):
    ratio = SLC_BLOCK // CMP_STRIDE
    nc = nch - 1
    m = np.zeros((nsbp, nch), np.float32)
    for j in range(nsb):
        for a in range(ratio):
            for bb in range(CMP_BLOCK // CMP_STRIDE):
                c = j * ratio + a - bb
                if 0 <= c < nc:
                    m[j, c] += 1.0
    return m


def _split_w_in(w_in):
    widths = [MOBA_W] * 3 + [NSA_W] + [NSA_KV_W] * 6 + [3 * NSA_HEADS, D_MODEL, D_MODEL]
    cuts = np.cumsum(widths)[:-1]
    mq, mk, mv, nq, nkc, nvc, nks, nvs, nkw, nvw, ngate, ga, gb = jnp.split(w_in, cuts, axis=1)
    ngate = jnp.pad(ngate, ((0, 0), (0, GATE_PAD - 3 * NSA_HEADS)))
    q_scale = HEAD_DIM ** -0.5 * LOG2E
    mq, nq = mq * q_scale, nq * q_scale
    w_tok = jnp.concatenate([ga, gb, mk, nks, nkw, ngate], axis=1)
    w_fm = jnp.concatenate([mq, mv, nq[:, _nq_perm()], nvs, nvw], axis=1)
    w_cmp = jnp.concatenate([nkc, nvc], axis=1)
    return w_tok.astype(BF16), jnp.transpose(w_fm).astype(BF16), w_cmp.astype(BF16)


def _layer(x, mem, bias_tiles, norm_mix_g, w_in, cmp_pos_k, cmp_w1_k, cmp_w2_k, cmp_pos_v,
           cmp_w1_v, cmp_w2_v, w_branch_a, w_branch_b, w_out, norm_xattn_g, norm_mem_g, w_xq, w_xkv, w_xo,
           norm_ffn_g, w_gate, w_up, conv_w, conv_b, w_down, norm_out_g):
    b, s, d = x.shape
    assert s % 1024 == 0 and d == D_MODEL
    rows = b * s
    nch = s // CMP_STRIDE
    nsb = s // SLC_BLOCK
    nsbp = max(LANES, nsb)

    w_tok, w_fm_t, w_cmp = _split_w_in(w_in)
    tok, fm, cmp_kv = _in_proj(x, norm_mix_g, w_tok, w_fm_t, w_cmp, tm=512)
    tok2d = tok.reshape(rows, TOK_W)

    kmean = _moba_kmean(tok)
    o_a = _moba_attention(tok, fm, kmean, bias_tiles)

    def w2_pad(w2):
        z = jnp.zeros_like(w2)
        return jnp.stack([jnp.concatenate([w2, z], axis=1), jnp.concatenate([z, w2], axis=1)]).astype(BF16)

    kc, vct = _compress(cmp_kv.reshape(b, CMP_W // HEAD_DIM, nch, CMP_STRIDE * HEAD_DIM),
                        cmp_pos_k.reshape(2, -1), cmp_pos_v.reshape(2, -1),
                        cmp_w1_k.astype(BF16), cmp_w1_v.astype(BF16), w2_pad(cmp_w2_k), w2_pad(cmp_w2_v))
    mimp_t = jnp.asarray(_importance_matrix(nch, nsb, nsbp), BF16)
    o_c, sel = _cmp_select(fm, kc, vct, mimp_t, nsb)
    o_s, o_w = _slc_win(tok, fm, sel, bias_tiles)

    x1 = _merge(x.reshape(rows, d), tok2d, o_a.reshape(rows, MOBA_W), o_c.reshape(rows, NSA_W),
                o_s.reshape(rows, NSA_W), o_w.reshape(rows, NSA_W), jnp.asarray(_gate_expand(), BF16),
                w_branch_a.astype(BF16), w_branch_b[_nq_perm(), :].astype(BF16), w_out.astype(BF16), tm=512)

    mem_len = mem.shape[1]
    memkv = _norm_matmul(mem.reshape(b * mem_len, d), norm_mem_g, w_xkv.astype(BF16), tm=mem_len,
                         name="mem_kv").reshape(b, mem_len, 2 * XATTN_W)
    xq_scale = XATTN_HEAD_DIM ** -0.5 * LOG2E
    x2 = _xattn(x1.reshape(b, s, d), norm_xattn_g, (w_xq * xq_scale).astype(BF16), memkv,
                w_xo.astype(BF16), tm=512)

    return _ffn(x2, norm_ffn_g, w_gate.astype(BF16), w_up.astype(BF16), conv_w, conv_b,
                w_down.astype(BF16), norm_out_g, tm=512)


def kernel(x, mem, rel_bias, norm_mix_g, w_in, cmp_pos_k, cmp_w1_k, cmp_w2_k, cmp_pos_v, cmp_w1_v, cmp_w2_v, w_branch_a, w_branch_b, w_out, norm_xattn_g, norm_mem_g, w_xq, w_xkv, w_xo, norm_ffn_g, w_gate, w_up, conv_w, conv_b, w_down, norm_final_g):
    depth = w_in.shape[0]
    assert depth == 1, "the final norm is fused into the last layer's FFN kernel"
    bias_tiles = _bias_tiles(rel_bias)
    l = 0
    return _layer(x, mem, bias_tiles, norm_mix_g[l], w_in[l], cmp_pos_k[l], cmp_w1_k[l],
                  cmp_w2_k[l], cmp_pos_v[l], cmp_w1_v[l], cmp_w2_v[l], w_branch_a[l], w_branch_b[l], w_out[l],
                  norm_xattn_g[l], norm_mem_g[l], w_xq[l], w_xkv[l], w_xo[l],
                  norm_ffn_g[l], w_gate[l], w_up[l], conv_w[l], conv_b[l], w_down[l], norm_final_g)
```

```python
import functools
import math

import numpy as np
import jax
import jax.numpy as jnp
from jax import lax
from jax.experimental import pallas as pl
from jax.experimental.pallas import tpu as pltpu

F32 = jnp.float32
BF16 = jnp.bfloat16

D_MODEL = 1024
HEAD_DIM = 64
MOBA_HEADS = 8
MOBA_BLOCK = 256
MOBA_TOPK = 3
NSA_HEADS = 8
NSA_GROUPS = 2
NSA_HPG = NSA_HEADS // NSA_GROUPS
CMP_BLOCK = 32
CMP_STRIDE = 16
CMP_HIDDEN = 256
SLC_BLOCK = 64
SLC_TOPN = 16
WINDOW = 512
REL_BUCKETS = 32
REL_MAX_DIST = 128
XATTN_HEADS = 4
XATTN_HEAD_DIM = 128
D_FF = 2816
CONV_WIDTH = 3
Q_BLOCK = 128
RMS_EPS = 1e-6
NEG_BIG = -1e30
LOG2E = math.log2(math.e)

LANES = 128
TQ = 256
KT = 256
STRIP = 512
ONES_ROWS = 16
VROWS = HEAD_DIM + ONES_ROWS
VMEM_LIMIT = 56 * 1024 * 1024

MOBA_W = MOBA_HEADS * HEAD_DIM
NSA_W = NSA_HEADS * HEAD_DIM
NSA_KV_W = NSA_GROUPS * HEAD_DIM
XATTN_W = XATTN_HEADS * XATTN_HEAD_DIM
GATE_PAD = LANES

COL_GA = 0
COL_GB = COL_GA + D_MODEL
COL_MK = COL_GB + D_MODEL
COL_NKS = COL_MK + MOBA_W
COL_NKW = COL_NKS + NSA_KV_W
COL_NGATE = COL_NKW + NSA_KV_W
TOK_W = COL_NGATE + GATE_PAD
CMP_W = 2 * NSA_KV_W
ROW_MQ = 0
ROW_MV = ROW_MQ + MOBA_W
ROW_NQ = ROW_MV + MOBA_W
ROW_NVS = ROW_NQ + NSA_W
ROW_NVW = ROW_NVS + NSA_KV_W
FM_W = ROW_NVW + NSA_KV_W

BT_OWN, BT_PREV, BT_UPPER, BT_MASKED = 0, 1, 2, 3
N_BIAS_TILES = 4


def _bucket_thresholds():
    n = np.arange(0, 4 * REL_MAX_DIST)
    exact = REL_BUCKETS // 2
    nf = np.maximum(n, 1).astype(np.float64)
    large = exact + (np.log(nf / exact) / math.log(REL_MAX_DIST / exact) * (REL_BUCKETS - exact)).astype(np.int64)
    large = np.minimum(large, REL_BUCKETS - 1)
    b = np.where(n < exact, n, large)
    return [int(np.argmax(b >= k)) for k in range(REL_BUCKETS)]


_THR = _bucket_thresholds()
assert _THR[REL_BUCKETS - 1] <= LANES


def _dot(a, b):
    return jnp.dot(a, b, preferred_element_type=F32)


def _dot_nt(a, b):
    return lax.dot_general(a, b, (((1,), (1,)), ((), ())), preferred_element_type=F32)


def _rms(x, g):
    return x * lax.rsqrt(jnp.mean(x * x, axis=-1, keepdims=True) + RMS_EPS) * g


def _params(n_axes):
    return pltpu.CompilerParams(dimension_semantics=("arbitrary",) * n_axes,
                                vmem_limit_bytes=VMEM_LIMIT)


def _bias_kernel(rb_ref, out_ref):
    h = pl.program_id(0)
    j = lax.broadcasted_iota(jnp.int32, (KT, TQ), 0)
    i = lax.broadcasted_iota(jnp.int32, (KT, TQ), 1)
    far = rb_ref[REL_BUCKETS - 1, h]

    def table(d):
        t = jnp.full((KT, TQ), (rb_ref[0, h] - far) * LOG2E, F32)
        for k in range(1, REL_BUCKETS):
            t = jnp.where(d >= _THR[k], (rb_ref[k, h] - far) * LOG2E, t)
        return t

    neg = jnp.full((KT, TQ), NEG_BIG, F32)
    out_ref[0, BT_OWN] = jnp.where(i >= j, table(i - j), neg)
    out_ref[0, BT_PREV] = table(i - j + KT)
    out_ref[0, BT_UPPER] = jnp.where(j > i, 0.0, neg)
    out_ref[0, BT_MASKED] = neg


def _bias_tiles(rel_bias):
    nh = rel_bias.shape[1]
    return pl.pallas_call(
        _bias_kernel,
        grid=(nh,),
        in_specs=[pl.BlockSpec(memory_space=pltpu.SMEM)],
        out_specs=pl.BlockSpec((1, N_BIAS_TILES, KT, TQ), lambda h: (h, 0, 0, 0)),
        out_shape=jax.ShapeDtypeStruct((nh, N_BIAS_TILES, KT, TQ), F32),
        compiler_params=_params(1),
        name="bias_tiles",
    )(rel_bias)


def _inproj_kernel(x_ref, g_ref, wt_ref, wf_ref, wc_ref, tok_ref, fm_ref, cmp_ref, *, chunk):
    h = _rms(x_ref[...], g_ref[...]).astype(BF16)
    c = _dot(h, wc_ref[...]).astype(cmp_ref.dtype)
    for i in range(cmp_ref.shape[0]):
        cmp_ref[i] = c[:, i * HEAD_DIM:(i + 1) * HEAD_DIM]
    n = wt_ref.shape[1]
    for c0 in range(0, n, chunk):
        c1 = min(c0 + chunk, n)
        tok_ref[:, c0:c1] = _dot(h, wt_ref[:, c0:c1]).astype(tok_ref.dtype)
    n = wf_ref.shape[0]
    for c0 in range(0, n, chunk):
        c1 = min(c0 + chunk, n)
        fm_ref[c0:c1, :] = _dot_nt(wf_ref[c0:c1, :], h).astype(fm_ref.dtype)


def _in_proj(x, g, w_tok, w_fm_t, w_cmp, tm):
    b, s, d = x.shape
    ncmp = CMP_W // HEAD_DIM
    return pl.pallas_call(
        functools.partial(_inproj_kernel, chunk=512),
        grid=(b, s // tm),
        in_specs=[pl.BlockSpec((None, tm, d), lambda i, j: (i, j, 0)),
                  pl.BlockSpec((1, d), lambda i, j: (0, 0)),
                  pl.BlockSpec(w_tok.shape, lambda i, j: (0, 0)),
                  pl.BlockSpec(w_fm_t.shape, lambda i, j: (0, 0)),
                  pl.BlockSpec(w_cmp.shape, lambda i, j: (0, 0))],
        out_specs=[pl.BlockSpec((None, tm, TOK_W), lambda i, j: (i, j, 0)),
                   pl.BlockSpec((None, FM_W, tm), lambda i, j: (i, 0, j)),
                   pl.BlockSpec((None, ncmp, tm, HEAD_DIM), lambda i, j: (i, 0, j, 0))],
        out_shape=[jax.ShapeDtypeStruct((b, s, TOK_W), BF16),
                   jax.ShapeDtypeStruct((b, FM_W, s), BF16),
                   jax.ShapeDtypeStruct((b, ncmp, s, HEAD_DIM), BF16)],
        compiler_params=_params(2),
        name="in_proj",
    )(x, g.reshape(1, d), w_tok, w_fm_t, w_cmp)


def _norm_matmul_kernel(x_ref, g_ref, w_ref, o_ref):
    o_ref[...] = _dot(_rms(x_ref[...], g_ref[...]).astype(BF16), w_ref[...]).astype(o_ref.dtype)


def _norm_matmul(x2d, g, w, tm, name):
    m, d = x2d.shape
    n = w.shape[1]
    return pl.pallas_call(
        _norm_matmul_kernel,
        grid=(m // tm,),
        in_specs=[pl.BlockSpec((tm, d), lambda i: (i, 0)),
                  pl.BlockSpec((1, d), lambda i: (0, 0)),
                  pl.BlockSpec((d, n), lambda i: (0, 0))],
        out_specs=pl.BlockSpec((tm, n), lambda i: (i, 0)),
        out_shape=jax.ShapeDtypeStruct((m, n), BF16),
        compiler_params=_params(1),
        name=name,
    )(x2d, g.reshape(1, d), w)


def _kmean_kernel(k_ref, o_ref):
    nb = o_ref.shape[0]
    for n in range(nb):
        blk = k_ref[n * MOBA_BLOCK:(n + 1) * MOBA_BLOCK, :].astype(F32)
        o_ref[n:n + 1, :] = jnp.mean(blk, axis=0, keepdims=True)


def _moba_kmean(tok):
    b, s, _ = tok.shape
    nb = s // MOBA_BLOCK
    return pl.pallas_call(
        _kmean_kernel,
        grid=(b,),
        in_specs=[pl.BlockSpec((None, s, MOBA_W), lambda i: (i, 0, COL_MK // MOBA_W))],
        out_specs=pl.BlockSpec((None, nb, MOBA_W), lambda i: (i, 0, 0)),
        out_shape=jax.ShapeDtypeStruct((b, nb, MOBA_W), F32),
        compiler_params=_params(1),
        name="moba_kmean",
    )(tok)


def _fill_v_with_ones(vt_ref, lo_ref, hi_ref, row0=0, cols=1024):
    s = vt_ref.shape[1]
    ones = jnp.ones((ONES_ROWS, cols), BF16)

    def body(c, carry):
        c0 = pl.multiple_of(c * cols, cols)
        lo_ref[0:HEAD_DIM, pl.ds(c0, cols)] = vt_ref[row0:row0 + HEAD_DIM, pl.ds(c0, cols)]
        lo_ref[HEAD_DIM:VROWS, pl.ds(c0, cols)] = ones
        hi_ref[0:HEAD_DIM, pl.ds(c0, cols)] = vt_ref[row0 + HEAD_DIM:row0 + 2 * HEAD_DIM, pl.ds(c0, cols)]
        hi_ref[HEAD_DIM:VROWS, pl.ds(c0, cols)] = ones
        return carry

    lax.fori_loop(0, s // cols, body, 0)


def _split_heads_t(q_t):
    row = lax.broadcasted_iota(jnp.int32, q_t.shape, 0)
    zero = jnp.zeros_like(q_t)
    return jnp.where(row < HEAD_DIM, q_t, zero), jnp.where(row >= HEAD_DIM, q_t, zero)


def _fill_k_with_block_ids(k_ref, lo_ref, hi_ref, block, nblock, lane0=0, rows=512):
    s = k_ref.shape[0]
    lane = lax.broadcasted_iota(jnp.int32, (rows, LANES), 1)
    shift = int(math.log2(block))

    def body(c, carry):
        r0 = pl.multiple_of(c * rows, rows)
        key = r0 + lax.broadcasted_iota(jnp.int32, (rows, LANES), 0)
        bid = lax.shift_right_logical(key, shift) & (nblock - 1)
        k = k_ref[pl.ds(r0, rows), lane0:lane0 + LANES]
        lo_ref[pl.ds(r0, rows), :] = jnp.where(lane < HEAD_DIM, k, (lane - HEAD_DIM == bid).astype(BF16))
        hi_ref[pl.ds(r0, rows), :] = jnp.where(lane >= HEAD_DIM, k, (lane == bid).astype(BF16))
        return carry

    lax.fori_loop(0, s // rows, body, 0)


def _q_with_mask_rows(q_rows, mask_rows, low_head):
    r = lax.broadcasted_iota(jnp.int32, (ONES_ROWS, TQ), 0)
    m = jnp.zeros((ONES_ROWS, TQ), F32)
    for b, row in enumerate(mask_rows):
        m = jnp.where(r == b, row, m)
    pad = jnp.zeros((HEAD_DIM - ONES_ROWS, TQ), BF16)
    parts = [q_rows, m.astype(BF16), pad] if low_head else [m.astype(BF16), pad, q_rows]
    return jnp.concatenate(parts, axis=0)


def _pipe(bufs, m_ref, acc_ref, nheads, a=None, b=None, c=None, c_last=False, group=1):
    s_refs, p_refs, mx_ref, al_ref = bufs
    rows = s_refs[0].shape[1]
    strip = min(STRIP, rows)

    def stage_c(h):
        c_slot, c_base, vt_fn = c
        acc_ref[c_base + h] = (al_ref[c_slot, h] * acc_ref[c_base + h]
                               + _dot(vt_fn(h), p_refs[c_slot][h]))

    for g0 in range(0, nheads, group):
        hs = range(g0, min(g0 + group, nheads))
        if c is not None and not c_last:
            for h in hs:
                stage_c(h)
        if a is not None:
            a_slot, score_fn, bias_fn = a
            raws = {h: score_fn(h) for h in hs}
        for h in hs:
            if b is not None:
                b_slot, b_base = b
                m_prev = m_ref[b_base + h]
                m_next = jnp.maximum(m_prev, mx_ref[b_slot, h])
                al_ref[b_slot, h] = jnp.exp2(m_prev - m_next)
                m_ref[b_base + h] = m_next
            mx = None
            strips = [slice(i * strip, (i + 1) * strip) for i in range(rows // strip)]
            if b is not None:
                for r in strips:
                    p_refs[b_slot][h, r, :] = jnp.exp2(s_refs[b_slot][h, r, :] - m_next).astype(BF16)
            if a is not None:
                for i, r in enumerate(strips):
                    s = raws[h][r, :] if bias_fn is None else raws[h][r, :] + bias_fn(h, i)
                    s_refs[a_slot][h, r, :] = s
                    cmax = jnp.max(s, axis=0, keepdims=True)
                    mx = cmax if mx is None else jnp.maximum(mx, cmax)
            if a is not None:
                mx_ref[a_slot, h] = mx
        if c is not None and c_last:
            for h in hs:
                stage_c(h)


def _reset_state(bufs, m_ref, acc_ref):
    _, p_refs, _, al_ref = bufs
    m_ref[...] = jnp.full(m_ref.shape, NEG_BIG, F32)
    acc_ref[...] = jnp.zeros(acc_ref.shape, F32)
    p_refs[1][...] = jnp.zeros(p_refs[1].shape, BF16)
    al_ref[...] = jnp.ones(al_ref.shape, F32)


def _pair_output(acc_lo, acc_hi):
    o_lo = acc_lo[0:HEAD_DIM, :] / acc_lo[HEAD_DIM:HEAD_DIM + 1, :]
    o_hi = acc_hi[0:HEAD_DIM, :] / acc_hi[HEAD_DIM:HEAD_DIM + 1, :]
    return jnp.transpose(jnp.concatenate([o_lo, o_hi], axis=0))


def _top_k_rows(val, n_f, k, sentinel):
    chosen = jnp.zeros(val.shape, F32)
    for _ in range(k):
        mx = jnp.max(val, axis=0, keepdims=True)
        idx = jnp.min(jnp.where(val == mx, n_f, sentinel), axis=0, keepdims=True)
        pick = n_f == idx
        chosen = jnp.where(pick, 1.0, chosen)
        val = jnp.where(pick, -jnp.inf, val)
    return chosen


def _moba_kernel(qt_ref, k_ref, vt_ref, km_ref, bt_ref, o_ref, *scratch):
    npair = qt_ref.shape[0] // LANES
    nh = 2 * npair
    k_refs, v_refs = scratch[0:nh], scratch[nh:2 * nh]
    mb_ref, qs_ref, m_ref, acc_ref, s0_ref, s1_ref, p0_ref, p1_ref, mx_ref, al_ref = scratch[2 * nh:]
    cur = pl.program_id(2)
    bufs = ((s0_ref, s1_ref), (p0_ref, p1_ref), mx_ref, al_ref)
    nb = km_ref.shape[0]
    sk = s0_ref.shape[1]
    strip = min(STRIP, sk)

    @pl.when(cur == 0)
    def _():
        for pp in range(npair):
            _fill_v_with_ones(vt_ref, v_refs[2 * pp], v_refs[2 * pp + 1], row0=pp * LANES)
            _fill_k_with_block_ids(k_ref, k_refs[2 * pp], k_refs[2 * pp + 1], MOBA_BLOCK, 2, lane0=pp * LANES)

    n_f = lax.broadcasted_iota(jnp.int32, (nb, TQ), 0).astype(F32)
    past = n_f < cur.astype(F32)
    for pp in range(npair):
        km = km_ref[:, pp * LANES:(pp + 1) * LANES].astype(BF16)
        for hh, qh in enumerate(_split_heads_t(qt_ref[pp * LANES:(pp + 1) * LANES, :])):
            gate = jnp.where(past, _dot(km, qh), -jnp.inf)
            chosen = _top_k_rows(gate, n_f, min(MOBA_TOPK, nb), float(nb))
            mb_ref[2 * pp + hh] = jnp.where((chosen > 0.5) & past, 0.0, NEG_BIG)
            qs_ref[2 * pp + hh] = qh
    _reset_state(bufs, m_ref, acc_ref)

    def first_block(u):
        return jnp.maximum(cur - 2 * u - 1, 0)

    def score_fn(u, first_rows, second_rows):
        idx = first_block(u)
        ks = pl.multiple_of(idx * KT, KT)
        first_even = (idx & 1) == 0

        def f(h):
            low = h % 2 == 0
            even = jnp.where(first_even, first_rows[h], second_rows[h])
            odd = jnp.where(first_even, second_rows[h], first_rows[h])
            q_rows = qs_ref[h, 0:HEAD_DIM, :] if low else qs_ref[h, HEAD_DIM:2 * HEAD_DIM, :]
            return _dot(k_refs[h][pl.ds(ks, sk), :], _q_with_mask_rows(q_rows, (even, odd), low))
        return f

    def vt_fn(u):
        ks = pl.multiple_of(first_block(jnp.maximum(u, 0)) * KT, KT)
        return lambda h: v_refs[h][:, pl.ds(ks, sk)]

    has_prev = cur >= 1
    kind_lo = jnp.where(has_prev, BT_PREV, BT_OWN)
    kind_hi = jnp.where(has_prev, BT_OWN, BT_MASKED)
    keep = jnp.where(has_prev, 1.0, 0.0)
    prev_rows = [mb_ref[h, pl.ds(jnp.maximum(cur - 1, 0), 1), :] * keep for h in range(nh)]
    no_rows = [jnp.zeros((1, TQ), F32)] * nh

    def near_bias(h, i):
        if strip == sk:
            return jnp.concatenate([bt_ref[h, kind_lo], bt_ref[h, kind_hi]], axis=0)
        half, part = divmod(i, KT // strip)
        return bt_ref[h, kind_lo if half == 0 else kind_hi, part * strip:(part + 1) * strip, :]

    def far_scores(u):
        b_lo = cur - 2 * u - 1
        idx = jnp.maximum(b_lo, 0)
        pen_lo = jnp.where(b_lo >= -1, 0.0, NEG_BIG)
        pen_hi = jnp.where(b_lo >= 0, 0.0, NEG_BIG)
        return score_fn(u, [mb_ref[h, pl.ds(idx, 1), :] + pen_lo for h in range(nh)],
                        [mb_ref[h, pl.ds(idx + 1, 1), :] + pen_hi for h in range(nh)])

    pipe = functools.partial(_pipe, bufs, m_ref, acc_ref, nh, c_last=True)
    pipe(a=(0, score_fn(0, prev_rows, no_rows), near_bias))
    n_far = lax.shift_right_logical(cur, 1)

    def two_jobs(t, carry):
        u0 = 2 * t + 1
        pipe(a=(1, far_scores(u0), None), b=(0, 0), c=(1, 0, vt_fn(u0 - 2)))
        pipe(a=(0, far_scores(u0 + 1), None), b=(1, 0), c=(0, 0, vt_fn(u0 - 1)))
        return carry

    n_iter = lax.shift_right_logical(n_far + 1, 1)
    lax.fori_loop(0, n_iter, two_jobs, 0)
    pipe(b=(0, 0), c=(1, 0, vt_fn(2 * n_iter - 1)))
    pipe(c=(0, 0, vt_fn(2 * n_iter)))
    for pp in range(npair):
        o_ref[:, pp * LANES:(pp + 1) * LANES] = _pair_output(
            acc_ref[2 * pp], acc_ref[2 * pp + 1]).astype(o_ref.dtype)


def _moba_attention(tok, fm, kmean, bias_tiles, pairs_per_step=2):
    b, s, _ = tok.shape
    nb = s // MOBA_BLOCK
    nh = 2 * pairs_per_step
    w = pairs_per_step * LANES
    sk = 2 * KT
    return pl.pallas_call(
        _moba_kernel,
        grid=(b, MOBA_W // w, s // TQ),
        in_specs=[
            pl.BlockSpec((None, w, TQ), lambda i, p, c: (i, ROW_MQ // w + p, c)),
            pl.BlockSpec((None, s, w), lambda i, p, c: (i, 0, COL_MK // w + p)),
            pl.BlockSpec((None, w, s), lambda i, p, c: (i, ROW_MV // w + p, 0)),
            pl.BlockSpec((None, nb, w), lambda i, p, c: (i, 0, p)),
            pl.BlockSpec((nh, N_BIAS_TILES, KT, TQ), lambda i, p, c: (p, 0, 0, 0)),
        ],
        out_specs=pl.BlockSpec((None, TQ, w), lambda i, p, c: (i, c, p)),
        out_shape=jax.ShapeDtypeStruct((b, s, MOBA_W), BF16),
        scratch_shapes=[pltpu.VMEM((s, LANES), BF16)] * nh + [pltpu.VMEM((VROWS, s), BF16)] * nh + [
            pltpu.VMEM((nh, nb, TQ), F32),
            pltpu.VMEM((nh, LANES, TQ), BF16),
            pltpu.VMEM((nh, 1, TQ), F32),
            pltpu.VMEM((nh, VROWS, TQ), F32),
            pltpu.VMEM((nh, sk, TQ), F32), pltpu.VMEM((nh, sk, TQ), F32),
            pltpu.VMEM((nh, sk, TQ), BF16), pltpu.VMEM((nh, sk, TQ), BF16),
            pltpu.VMEM((2, nh, 1, TQ), F32), pltpu.VMEM((2, nh, 1, TQ), F32),
        ],
        compiler_params=_params(3),
        name="moba_attention",
    )(fm, tok, fm, kmean, bias_tiles)


def _compress_kernel(ck_ref, cv_ref, pk_ref, pv_ref, w1k_ref, w1v_ref, w2k_ref, w2v_ref,
                     kc_ref, vct_ref):
    nch = ck_ref.shape[1]
    half = CMP_STRIDE * HEAD_DIM

    def run(c_ref, pos_ref, w1_ref, w2_ref):
        acc = jnp.zeros((nch, LANES), F32)
        for g in range(NSA_GROUPS):
            c = c_ref[g].astype(F32)
            lo = (c + pos_ref[0:1, :]).astype(BF16)
            hi = (c + pos_ref[1:2, :]).astype(BF16)
            pre = _dot(lo, w1_ref[:half, :]) + pltpu.roll(_dot(hi, w1_ref[half:, :]), nch - 1, axis=0)
            acc = acc + _dot(jax.nn.gelu(pre).astype(BF16), w2_ref[g])
        return acc

    kc_ref[...] = run(ck_ref, pk_ref, w1k_ref, w2k_ref).astype(kc_ref.dtype)
    vct_ref[...] = jnp.transpose(run(cv_ref, pv_ref, w1v_ref, w2v_ref)).astype(vct_ref.dtype)


def _compress(chunks, pos_k, pos_v, w1k, w1v, w2k, w2v):
    b, _, nch, width = chunks.shape
    g = NSA_GROUPS
    full = lambda shape: pl.BlockSpec(shape, lambda i: (0,) * len(shape))
    return pl.pallas_call(
        _compress_kernel,
        grid=(b,),
        in_specs=[pl.BlockSpec((None, g, nch, width), lambda i: (i, 0, 0, 0)),
                  pl.BlockSpec((None, g, nch, width), lambda i: (i, 1, 0, 0)),
                  full(pos_k.shape), full(pos_v.shape), full(w1k.shape), full(w1v.shape),
                  full(w2k.shape), full(w2v.shape)],
        out_specs=[pl.BlockSpec((None, nch, LANES), lambda i: (i, 0, 0)),
                   pl.BlockSpec((None, LANES, nch), lambda i: (i, 0, 0))],
        out_shape=[jax.ShapeDtypeStruct((b, nch, LANES), BF16),
                   jax.ShapeDtypeStruct((b, LANES, nch), BF16)],
        compiler_params=_params(1),
        name="nsa_compress",
    )(chunks, chunks, pos_k, pos_v, w1k, w1v, w2k, w2v)


def _cmp_select_kernel(qt_ref, kc_ref, vct_ref, mimp_ref, o_ref, sel_ref, *, nsb, nq):
    ci = pl.program_id(1)
    nch = kc_ref.shape[0]
    nsbp = mimp_ref.shape[0]
    parts = 4 if (nch // 4) % LANES == 0 and (nsbp // 4) % 8 == 0 else 1
    for v in range(1, parts + 1):
        @pl.when((ci * parts >= (v - 1) * nq) & (ci * parts < v * nq))
        def _(v=v):
            _cmp_select_body(qt_ref, kc_ref, vct_ref, mimp_ref, o_ref, sel_ref,
                             nsb=nsb, nch=nch * v // parts, nrow=nsbp * v // parts)


def _cmp_select_body(qt_ref, kc_ref, vct_ref, mimp_ref, o_ref, sel_ref, *, nsb, nch, nrow):
    qs = pl.program_id(1) * TQ
    n_row = lax.broadcasted_iota(jnp.int32, (nch, TQ), 0)
    t_col = qs + lax.broadcasted_iota(jnp.int32, (nch, TQ), 1)
    hidden = jnp.where((n_row * CMP_STRIDE + (CMP_BLOCK - 1)) <= t_col, 0.0, NEG_BIG)
    any_visible = t_col[0:1, :] >= (CMP_BLOCK - 1)
    kc = kc_ref[0:nch, :]
    vct = vct_ref[:, 0:nch]
    vrow = lax.broadcasted_iota(jnp.int32, vct.shape, 0)
    mimp = mimp_ref[0:nrow, 0:nch]

    j_row = lax.broadcasted_iota(jnp.int32, (nrow, TQ), 0)
    t_sel = qs + lax.broadcasted_iota(jnp.int32, (nrow, TQ), 1)
    cur = lax.shift_right_logical(t_sel, int(math.log2(SLC_BLOCK)))
    j_f = j_row.astype(F32)
    forced = (j_row == 0) | (j_row == cur) | (j_row == cur - 1)
    allowed = (j_row <= cur) & (j_row < nsb)

    q_heads = [_split_heads_t(qt_ref[r * LANES:(r + 1) * LANES, :]) for r in range(NSA_HPG)]
    out_t = [jnp.zeros((LANES, TQ), F32) for _ in range(NSA_HPG)]
    for g in range(NSA_GROUPS):
        vin_g = (vrow < HEAD_DIM) if g == 0 else (vrow >= HEAD_DIM)
        vct_g = jnp.where(vin_g, vct, jnp.zeros_like(vct))
        p_grp = jnp.zeros((nch, TQ), F32)
        for r in range(NSA_HPG):
            s = _dot(kc, q_heads[r][g]) + hidden
            e = jnp.exp2(s - jnp.max(s, axis=0, keepdims=True))
            p = e * jnp.where(any_visible, 1.0 / jnp.sum(e, axis=0, keepdims=True), 0.0)
            p_grp = p_grp + p
            out_t[r] = out_t[r] + _dot(vct_g, p.astype(BF16))
        p_hi = p_grp.astype(BF16)
        p_lo = (p_grp - p_hi.astype(F32)).astype(BF16)
        imp = _dot(mimp, p_hi) + _dot(mimp, p_lo)
        val = jnp.where(forced & allowed, jnp.inf, jnp.where(allowed, imp, -jnp.inf))
        chosen = _top_k_rows(val, j_f, min(SLC_TOPN, nsb), float(nrow))
        sel_ref[g, 0:nrow, :] = jnp.where(allowed & (chosen > 0.5), 0.0, NEG_BIG)
        if nrow < sel_ref.shape[1]:
            sel_ref[g, nrow:, :] = jnp.full((sel_ref.shape[1] - nrow, TQ), NEG_BIG, F32)
    for r in range(NSA_HPG):
        o_ref[:, r * LANES:(r + 1) * LANES] = jnp.transpose(out_t[r]).astype(o_ref.dtype)


def _cmp_select(fm, kc, vct, mimp_t, nsb):
    b, _, s = fm.shape
    nch = kc.shape[1]
    nsbp = mimp_t.shape[0]
    return pl.pallas_call(
        functools.partial(_cmp_select_kernel, nsb=nsb, nq=s // TQ),
        grid=(b, s // TQ),
        in_specs=[pl.BlockSpec((None, NSA_W, TQ), lambda i, c: (i, ROW_NQ // NSA_W, c)),
                  pl.BlockSpec((None, nch, LANES), lambda i, c: (i, 0, 0)),
                  pl.BlockSpec((None, LANES, nch), lambda i, c: (i, 0, 0)),
                  pl.BlockSpec((nsbp, nch), lambda i, c: (0, 0))],
        out_specs=[pl.BlockSpec((None, TQ, NSA_W), lambda i, c: (i, c, 0)),
                   pl.BlockSpec((None, NSA_GROUPS, nsbp, TQ), lambda i, c: (i, 0, 0, c))],
        out_shape=[jax.ShapeDtypeStruct((b, s, NSA_W), BF16),
                   jax.ShapeDtypeStruct((b, NSA_GROUPS, nsbp, s), F32)],
        compiler_params=_params(2),
        name="nsa_cmp_select",
    )(fm, kc, vct, mimp_t)


def _slc_win_kernel(qt_ref, ks_ref, vst_ref, kw_ref, vwt_ref, sel_ref, bt_ref,
                    oslc_ref, owin_ref,
                    ks0_ref, ks1_ref, vs0_ref, vs1_ref, vw0_ref, vw1_ref, qs_ref, m_ref, acc_ref,
                    s0_ref, s1_ref, p0_ref, p1_ref, mx_ref, al_ref):
    cur = pl.program_id(1)
    bufs = ((s0_ref, s1_ref), (p0_ref, p1_ref), mx_ref, al_ref)
    per_tile = KT // SLC_BLOCK

    @pl.when(cur == 0)
    def _():
        _fill_v_with_ones(vst_ref, vs0_ref, vs1_ref)
        _fill_v_with_ones(vwt_ref, vw0_ref, vw1_ref)
        _fill_k_with_block_ids(ks_ref, ks0_ref, ks1_ref, SLC_BLOCK, per_tile)

    nh = NSA_HEADS
    for r in range(NSA_HPG):
        lo, hi = _split_heads_t(qt_ref[r * LANES:(r + 1) * LANES, :])
        qs_ref[r] = lo
        qs_ref[NSA_HPG + r] = hi
    _reset_state(bufs, m_ref, acc_ref)
    ks_refs = (ks0_ref, ks1_ref)
    vs_refs = (vs0_ref, vs1_ref)
    vw_refs = (vw0_ref, vw1_ref)
    sel_base, win_base = 0, nh

    def tile_start(n):
        return pl.multiple_of(jnp.maximum(n, 0) * KT, KT)

    def win_scores(n):
        ks = tile_start(n)
        return lambda h: _dot(kw_ref[pl.ds(ks, KT), :], qs_ref[h])

    def sel_scores(n, penalty):
        ks = tile_start(n)
        base = jnp.maximum(n, 0) * per_tile
        rows = [[sel_ref[g, pl.ds(base + b, 1), :] + penalty for b in range(per_tile)]
                for g in range(NSA_GROUPS)]

        def f(h):
            g = h // NSA_HPG
            q_rows = qs_ref[h, 0:HEAD_DIM, :] if g == 0 else qs_ref[h, HEAD_DIM:2 * HEAD_DIM, :]
            return _dot(ks_refs[g][pl.ds(ks, KT), :], _q_with_mask_rows(q_rows, rows[g], g == 0))
        return f

    def vt_fn(v_refs, n):
        ks = tile_start(n)
        return lambda h: v_refs[h // NSA_HPG][:, pl.ds(ks, KT)]

    def near_bias(kind):
        strip = min(STRIP, KT)
        return lambda h, i: bt_ref[h, kind, i * strip:(i + 1) * strip, :]

    pipe = functools.partial(_pipe, bufs, m_ref, acc_ref, nh, c_last=True)
    kind_prev = jnp.where(cur >= 1, BT_PREV, BT_MASKED)
    kind_upper = jnp.where(cur >= 2, BT_UPPER, BT_MASKED)
    pipe(a=(0, win_scores(cur), near_bias(BT_OWN)))
    pipe(a=(1, win_scores(cur - 1), near_bias(kind_prev)), b=(0, win_base))
    pipe(a=(0, win_scores(cur - 2), near_bias(kind_upper)), b=(1, win_base),
         c=(0, win_base, vt_fn(vw_refs, cur)))
    pipe(a=(1, sel_scores(cur, 0.0), near_bias(BT_OWN)), b=(0, win_base),
         c=(1, win_base, vt_fn(vw_refs, cur - 1)))
    pipe(a=(0, sel_scores(cur - 1, 0.0), near_bias(kind_prev)), b=(1, sel_base),
         c=(0, win_base, vt_fn(vw_refs, cur - 2)))

    def far_scores(n):
        return sel_scores(n, jnp.where(n >= 0, 0.0, NEG_BIG))

    def two_tiles(t, carry):
        n0 = cur - 2 - 2 * t
        pipe(a=(1, far_scores(n0), None), b=(0, sel_base), c=(1, sel_base, vt_fn(vs_refs, n0 + 2)))
        pipe(a=(0, far_scores(n0 - 1), None), b=(1, sel_base), c=(0, sel_base, vt_fn(vs_refs, n0 + 1)))
        return carry

    n_iter = lax.shift_right_logical(jnp.maximum(cur - 1, 0) + 1, 1)
    lax.fori_loop(0, n_iter, two_tiles, 0)
    pipe(b=(0, sel_base), c=(1, sel_base, vt_fn(vs_refs, cur - 2 * n_iter)))
    pipe(c=(0, sel_base, vt_fn(vs_refs, cur - 1 - 2 * n_iter)))

    for r in range(NSA_HPG):
        cols = slice(r * LANES, (r + 1) * LANES)
        oslc_ref[:, cols] = _pair_output(acc_ref[r], acc_ref[NSA_HPG + r]).astype(oslc_ref.dtype)
        owin_ref[:, cols] = _pair_output(acc_ref[nh + r], acc_ref[nh + NSA_HPG + r]).astype(owin_ref.dtype)


def _slc_win(tok, fm, sel, bias_tiles):
    b, s, _ = tok.shape
    nsbp = sel.shape[2]
    return pl.pallas_call(
        _slc_win_kernel,
        grid=(b, s // TQ),
        in_specs=[pl.BlockSpec((None, NSA_W, TQ), lambda i, c: (i, ROW_NQ // NSA_W, c)),
                  pl.BlockSpec((None, s, LANES), lambda i, c: (i, 0, COL_NKS // LANES)),
                  pl.BlockSpec((None, LANES, s), lambda i, c: (i, ROW_NVS // LANES, 0)),
                  pl.BlockSpec((None, s, LANES), lambda i, c: (i, 0, COL_NKW // LANES)),
                  pl.BlockSpec((None, LANES, s), lambda i, c: (i, ROW_NVW // LANES, 0)),
                  pl.BlockSpec((None, NSA_GROUPS, nsbp, TQ), lambda i, c: (i, 0, 0, c)),
                  pl.BlockSpec((NSA_HEADS, N_BIAS_TILES, KT, TQ), lambda i, c: (1, 0, 0, 0))],
        out_specs=[pl.BlockSpec((None, TQ, NSA_W), lambda i, c: (i, c, 0)),
                   pl.BlockSpec((None, TQ, NSA_W), lambda i, c: (i, c, 0))],
        out_shape=[jax.ShapeDtypeStruct((b, s, NSA_W), BF16),
                   jax.ShapeDtypeStruct((b, s, NSA_W), BF16)],
        scratch_shapes=[pltpu.VMEM((s, LANES), BF16)] * 2 + [pltpu.VMEM((VROWS, s), BF16)] * 4 + [
            pltpu.VMEM((NSA_HEADS, LANES, TQ), BF16),
            pltpu.VMEM((2 * NSA_HEADS, 1, TQ), F32),
            pltpu.VMEM((2 * NSA_HEADS, VROWS, TQ), F32),
            pltpu.VMEM((NSA_HEADS, KT, TQ), F32), pltpu.VMEM((NSA_HEADS, KT, TQ), F32),
            pltpu.VMEM((NSA_HEADS, KT, TQ), BF16), pltpu.VMEM((NSA_HEADS, KT, TQ), BF16),
            pltpu.VMEM((2, NSA_HEADS, 1, TQ), F32), pltpu.VMEM((2, NSA_HEADS, 1, TQ), F32)],
        compiler_params=_params(2),
        name="nsa_slc_win",
    )(fm, tok, fm, tok, fm, sel, bias_tiles)


def _merge_kernel(x_ref, oa_ref, oc_ref, os_ref, ow_ref, ng_ref, ga_ref, gb_ref,
                  eg_ref, wa_ref, wb_ref, wo_ref, o_ref):
    sig = jax.nn.sigmoid(ng_ref[...].astype(F32))
    s_hi = sig.astype(BF16)
    s_lo = (sig - s_hi.astype(F32)).astype(BF16)
    eg = eg_ref[...]
    gexp = _dot(s_hi, eg) + _dot(s_lo, eg)
    a = _dot(oa_ref[...], wa_ref[...])
    o_b = (gexp[:, :NSA_W] * oc_ref[...].astype(F32)
           + gexp[:, NSA_W:2 * NSA_W] * os_ref[...].astype(F32)
           + gexp[:, 2 * NSA_W:] * ow_ref[...].astype(F32))
    bb = _dot(o_b.astype(BF16), wb_ref[...])
    merged = (jax.nn.sigmoid(ga_ref[...].astype(F32)) * a
              + jax.nn.sigmoid(gb_ref[...].astype(F32)) * bb)
    o_ref[...] = x_ref[...] + _dot(merged.astype(BF16), wo_ref[...])


def _merge(x2d, tok2d, o_a, o_c, o_s, o_w, egate, wa, wb, wo, tm):
    m, d = x2d.shape
    row = lambda w, col=0: pl.BlockSpec((tm, w), lambda i, col=col: (i, col))
    full = lambda a: pl.BlockSpec(a.shape, lambda i: (0, 0))
    return pl.pallas_call(
        _merge_kernel,
        grid=(m // tm,),
        in_specs=[row(d), row(MOBA_W), row(NSA_W), row(NSA_W), row(NSA_W),
                  row(GATE_PAD, COL_NGATE // GATE_PAD),
                  row(D_MODEL, COL_GA // D_MODEL), row(D_MODEL, COL_GB // D_MODEL),
                  full(egate), full(wa), full(wb), full(wo)],
        out_specs=row(d),
        out_shape=jax.ShapeDtypeStruct((m, d), F32),
        compiler_params=_params(1),
        name="merge_out",
    )(x2d, o_a, o_c, o_s, o_w, tok2d, tok2d, tok2d, egate, wa, wb, wo)


def _xattn_kernel(x_ref, g_ref, wq_ref, kv_ref, wo_ref, o_ref, *, nsplit):
    part = x_ref.shape[0] // nsplit
    g = g_ref[...]
    rows = [slice(i * part, (i + 1) * part) for i in range(nsplit)]
    xs = [x_ref[r, :] for r in rows]
    qs = [_dot(_rms(x, g).astype(BF16), wq_ref[...]).astype(BF16) for x in xs]
    for r, x, q in zip(rows, xs, qs):
        cols = [slice(h * XATTN_HEAD_DIM, (h + 1) * XATTN_HEAD_DIM) for h in range(XATTN_HEADS)]
        scores = [_dot_nt(q[:, c], kv_ref[:, c]) for c in cols]
        heads = []
        for h, s in enumerate(scores):
            v = kv_ref[:, XATTN_W + h * XATTN_HEAD_DIM:XATTN_W + (h + 1) * XATTN_HEAD_DIM]
            e = jnp.exp2(s - jnp.max(s, axis=1, keepdims=True))
            p = e * (1.0 / jnp.sum(e, axis=1, keepdims=True))
            heads.append(_dot(p.astype(BF16), v).astype(BF16))
        o_ref[r, :] = x + _dot(jnp.concatenate(heads, axis=1), wo_ref[...])


def _xattn(x3d, g, wq, memkv, wo, tm):
    b, s, d = x3d.shape
    mem_len = memkv.shape[1]
    return pl.pallas_call(
        functools.partial(_xattn_kernel, nsplit=1),
        grid=(b, s // tm),
        in_specs=[pl.BlockSpec((None, tm, d), lambda i, j: (i, j, 0)),
                  pl.BlockSpec((1, d), lambda i, j: (0, 0)),
                  pl.BlockSpec(wq.shape, lambda i, j: (0, 0)),
                  pl.BlockSpec((None, mem_len, 2 * XATTN_W), lambda i, j: (i, 0, 0)),
                  pl.BlockSpec(wo.shape, lambda i, j: (0, 0))],
        out_specs=pl.BlockSpec((None, tm, d), lambda i, j: (i, j, 0)),
        out_shape=jax.ShapeDtypeStruct((b, s, d), F32),
        compiler_params=_params(2),
        name="xattn",
    )(x3d, g.reshape(1, d), wq, memkv, wo)


def _ffn_kernel(x_ref, halo_ref, g_ref, wg_ref, wu_ref, cw_ref, cb_ref, wd_ref, gf_ref, o_ref,
                y_ref, *, chunk, halo_rows):
    j = pl.program_id(1)
    x = x_ref[...]
    tm = x.shape[0]
    g = g_ref[...]
    hf = _rms(x, g).astype(BF16)
    keep = jnp.where(j > 0, 1.0, 0.0)
    halo = (_rms(halo_ref[...], g) * keep).astype(BF16)
    ext = jnp.concatenate([halo, hf], axis=0)
    row = lax.broadcasted_iota(jnp.int32, (tm, chunk), 0)
    starts = list(range(0, D_FF, chunk))

    def gate_up(c0):
        cols = slice(c0, c0 + chunk)
        return _dot(ext, wg_ref[:, cols]), _dot(hf, wu_ref[:, cols])

    nxt = gate_up(starts[0])
    for k, c0 in enumerate(starts):
        cols = slice(c0, c0 + chunk)
        ue, up = nxt
        if k + 1 < len(starts):
            nxt = gate_up(starts[k + 1])
        u = ue[halo_rows:, :]
        prev1 = ue[halo_rows - 1:halo_rows, :]
        prev2 = ue[halo_rows - 2:halo_rows - 1, :]
        u1 = jnp.where(row == 0, prev1, pltpu.roll(u, 1, axis=0))
        u2 = jnp.where(row == 0, prev2, jnp.where(row == 1, prev1, pltpu.roll(u, 2, axis=0)))
        a = cw_ref[0:1, cols] * u2 + cw_ref[1:2, cols] * u1 + cw_ref[2:3, cols] * u + cb_ref[:, cols]
        y_ref[:, cols] = (jax.nn.gelu(a) * up).astype(BF16)
    o_ref[...] = _rms(x + _dot(y_ref[...], wd_ref[...]), gf_ref[...])


def _ffn(x3d, g, wg, wu, cw, cb, wd, gf, tm):
    b, s, d = x3d.shape
    halo_rows = 16
    per = tm // halo_rows
    return pl.pallas_call(
        functools.partial(_ffn_kernel, chunk=256, halo_rows=halo_rows),
        grid=(b, s // tm),
        in_specs=[pl.BlockSpec((None, tm, d), lambda i, j: (i, j, 0)),
                  pl.BlockSpec((None, halo_rows, d), lambda i, j: (i, jnp.maximum(j * per - 1, 0), 0)),
                  pl.BlockSpec((1, d), lambda i, j: (0, 0)),
                  pl.BlockSpec(wg.shape, lambda i, j: (0, 0)),
                  pl.BlockSpec(wu.shape, lambda i, j: (0, 0)),
                  pl.BlockSpec(cw.shape, lambda i, j: (0, 0)),
                  pl.BlockSpec((1, D_FF), lambda i, j: (0, 0)),
                  pl.BlockSpec(wd.shape, lambda i, j: (0, 0)),
                  pl.BlockSpec((1, d), lambda i, j: (0, 0))],
        out_specs=pl.BlockSpec((None, tm, d), lambda i, j: (i, j, 0)),
        out_shape=jax.ShapeDtypeStruct((b, s, d), F32),
        scratch_shapes=[pltpu.VMEM((tm, D_FF), BF16)],
        compiler_params=_params(2),
        name="conv_ffn",
    )(x3d, x3d, g.reshape(1, d), wg, wu, cw, cb.reshape(1, D_FF), wd, gf.reshape(1, d))


def _nq_perm():
    idx = []
    for r in range(NSA_HPG):
        for g in range(NSA_GROUPS):
            h = g * NSA_HPG + r
            idx.extend(range(h * HEAD_DIM, (h + 1) * HEAD_DIM))
    return np.asarray(idx, np.int32)


def _gate_expand():
    e = np.zeros((GATE_PAD, 3 * NSA_W), np.float32)
    for br in range(3):
        for g in range(NSA_GROUPS):
            for r in range(NSA_HPG):
                c0 = br * NSA_W + r * LANES + g * HEAD_DIM
                e[br * NSA_HEADS + g * NSA_HPG + r, c0:c0 + HEAD_DIM] = 1.0
    return e


def _importance_matrix(nch, nsb, nsbp):
    ratio = SLC_BLOCK // CMP_STRIDE
    nc = nch - 1
    m = np.zeros((nsbp, nch), np.float32)
    for j in range(nsb):
        for a in range(ratio):
            for bb in range(CMP_BLOCK // CMP_STRIDE):
                c = j * ratio + a - bb
                if 0 <= c < nc:
                    m[j, c] += 1.0
    return m


def _split_w_in(w_in):
    widths = [MOBA_W] * 3 + [NSA_W] + [NSA_KV_W] * 6 + [3 * NSA_HEADS, D_MODEL, D_MODEL]
    cuts = np.cumsum(widths)[:-1]
    mq, mk, mv, nq, nkc, nvc, nks, nvs, nkw, nvw, ngate, ga, gb = jnp.split(w_in, cuts, axis=1)
    ngate = jnp.pad(ngate, ((0, 0), (0, GATE_PAD - 3 * NSA_HEADS)))
    q_scale = HEAD_DIM ** -0.5 * LOG2E
    mq, nq = mq * q_scale, nq * q_scale
    w_tok = jnp.concatenate([ga, gb, mk, nks, nkw, ngate], axis=1)
    w_fm = jnp.concatenate([mq, mv, nq[:, _nq_perm()], nvs, nvw], axis=1)
    w_cmp = jnp.concatenate([nkc, nvc], axis=1)
    return w_tok.astype(BF16), jnp.transpose(w_fm).astype(BF16), w_cmp.astype(BF16)


def _layer(x, mem, bias_tiles, norm_mix_g, w_in, cmp_pos_k, cmp_w1_k, cmp_w2_k, cmp_pos_v,
           cmp_w1_v, cmp_w2_v, w_branch_a, w_branch_b, w_out, norm_xattn_g, norm_mem_g, w_xq, w_xkv, w_xo,
           norm_ffn_g, w_gate, w_up, conv_w, conv_b, w_down, norm_out_g):
    b, s, d = x.shape
    assert s % 1024 == 0 and d == D_MODEL
    rows = b * s
    nch = s // CMP_STRIDE
    nsb = s // SLC_BLOCK
    nsbp = max(LANES, nsb)

    w_tok, w_fm_t, w_cmp = _split_w_in(w_in)
    tok, fm, cmp_kv = _in_proj(x, norm_mix_g, w_tok, w_fm_t, w_cmp, tm=512)
    tok2d = tok.reshape(rows, TOK_W)

    kmean = _moba_kmean(tok)
    o_a = _moba_attention(tok, fm, kmean, bias_tiles)

    def w2_pad(w2):
        z = jnp.zeros_like(w2)
        return jnp.stack([jnp.concatenate([w2, z], axis=1), jnp.concatenate([z, w2], axis=1)]).astype(BF16)

    kc, vct = _compress(cmp_kv.reshape(b, CMP_W // HEAD_DIM, nch, CMP_STRIDE * HEAD_DIM),
                        cmp_pos_k.reshape(2, -1), cmp_pos_v.reshape(2, -1),
                        cmp_w1_k.astype(BF16), cmp_w1_v.astype(BF16), w2_pad(cmp_w2_k), w2_pad(cmp_w2_v))
    mimp_t = jnp.asarray(_importance_matrix(nch, nsb, nsbp), BF16)
    o_c, sel = _cmp_select(fm, kc, vct, mimp_t, nsb)
    o_s, o_w = _slc_win(tok, fm, sel, bias_tiles)

    x1 = _merge(x.reshape(rows, d), tok2d, o_a.reshape(rows, MOBA_W), o_c.reshape(rows, NSA_W),
                o_s.reshape(rows, NSA_W), o_w.reshape(rows, NSA_W), jnp.asarray(_gate_expand(), BF16),
                w_branch_a.astype(BF16), w_branch_b[_nq_perm(), :].astype(BF16), w_out.astype(BF16), tm=512)

    mem_len = mem.shape[1]
    memkv = _norm_matmul(mem.reshape(b * mem_len, d), norm_mem_g, w_xkv.astype(BF16), tm=mem_len,
                         name="mem_kv").reshape(b, mem_len, 2 * XATTN_W)
    xq_scale = XATTN_HEAD_DIM ** -0.5 * LOG2E
    x2 = _xattn(x1.reshape(b, s, d), norm_xattn_g, (w_xq * xq_scale).astype(BF16), memkv,
                w_xo.astype(BF16), tm=512)

    return _ffn(x2, norm_ffn_g, w_gate.astype(BF16), w_up.astype(BF16), conv_w, conv_b,
                w_down.astype(BF16), norm_out_g, tm=512)


def kernel(x, mem, rel_bias, norm_mix_g, w_in, cmp_pos_k, cmp_w1_k, cmp_w2_k, cmp_pos_v, cmp_w1_v, cmp_w2_v, w_branch_a, w_branch_b, w_out, norm_xattn_g, norm_mem_g, w_xq, w_xkv, w_xo, norm_ffn_g, w_gate, w_up, conv_w, conv_b, w_down, norm_final_g):
    depth = w_in.shape[0]
    assert depth == 1, "the final norm is fused into the last layer's FFN kernel"
    bias_tiles = _bias_tiles(rel_bias)
    l = 0
    return _layer(x, mem, bias_tiles, norm_mix_g[l], w_in[l], cmp_pos_k[l], cmp_w1_k[l],
                  cmp_w2_k[l], cmp_pos_v[l], cmp_w1_v[l], cmp_w2_v[l], w_branch_a[l], w_branch_b[l], w_out[l],
                  norm_xattn_g[l], norm_mem_g[l], w_xq[l], w_xkv[l], w_xo[l],
                  norm_ffn_g[l], w_gate[l], w_up[l], conv_w[l], conv_b[l], w_down[l], norm_final_g)
```

```python
import functools
import math

import numpy as np
import jax
import jax.numpy as jnp
from jax import lax
from jax.experimental import pallas as pl
from jax.experimental.pallas import tpu as pltpu

F32 = jnp.float32
BF16 = jnp.bfloat16

D_MODEL = 1024
HEAD_DIM = 64
MOBA_HEADS = 8
MOBA_BLOCK = 256
MOBA_TOPK = 3
NSA_HEADS = 8
NSA_GROUPS = 2
NSA_HPG = NSA_HEADS // NSA_GROUPS
CMP_BLOCK = 32
CMP_STRIDE = 16
CMP_HIDDEN = 256
SLC_BLOCK = 64
SLC_TOPN = 16
WINDOW = 512
REL_BUCKETS = 32
REL_MAX_DIST = 128
XATTN_HEADS = 4
XATTN_HEAD_DIM = 128
D_FF = 2816
CONV_WIDTH = 3
Q_BLOCK = 128
RMS_EPS = 1e-6
NEG_BIG = -1e30
LOG2E = math.log2(math.e)

LANES = 128
TQ = 256
KT = 256
STRIP = 512
ONES_ROWS = 16
VROWS = HEAD_DIM + ONES_ROWS
VMEM_LIMIT = 56 * 1024 * 1024

MOBA_W = MOBA_HEADS * HEAD_DIM
NSA_W = NSA_HEADS * HEAD_DIM
NSA_KV_W = NSA_GROUPS * HEAD_DIM
XATTN_W = XATTN_HEADS * XATTN_HEAD_DIM
GATE_PAD = LANES

COL_GA = 0
COL_GB = COL_GA + D_MODEL
COL_MK = COL_GB + D_MODEL
COL_NKS = COL_MK + MOBA_W
COL_NKW = COL_NKS + NSA_KV_W
COL_NGATE = COL_NKW + NSA_KV_W
TOK_W = COL_NGATE + GATE_PAD
CMP_W = 2 * NSA_KV_W
ROW_MQ = 0
ROW_MV = ROW_MQ + MOBA_W
ROW_NQ = ROW_MV + MOBA_W
ROW_NVS = ROW_NQ + NSA_W
ROW_NVW = ROW_NVS + NSA_KV_W
FM_W = ROW_NVW + NSA_KV_W

BT_OWN, BT_PREV, BT_UPPER, BT_MASKED = 0, 1, 2, 3
N_BIAS_TILES = 4


def _bucket_thresholds():
    n = np.arange(0, 4 * REL_MAX_DIST)
    exact = REL_BUCKETS // 2
    nf = np.maximum(n, 1).astype(np.float64)
    large = exact + (np.log(nf / exact) / math.log(REL_MAX_DIST / exact) * (REL_BUCKETS - exact)).astype(np.int64)
    large = np.minimum(large, REL_BUCKETS - 1)
    b = np.where(n < exact, n, large)
    return [int(np.argmax(b >= k)) for k in range(REL_BUCKETS)]


_THR = _bucket_thresholds()
assert _THR[REL_BUCKETS - 1] <= LANES


def _dot(a, b):
    return jnp.dot(a, b, preferred_element_type=F32)


def _dot_nt(a, b):
    return lax.dot_general(a, b, (((1,), (1,)), ((), ())), preferred_element_type=F32)


def _rms(x, g):
    return x * lax.rsqrt(jnp.mean(x * x, axis=-1, keepdims=True) + RMS_EPS) * g


def _params(n_axes):
    return pltpu.CompilerParams(dimension_semantics=("arbitrary",) * n_axes,
                                vmem_limit_bytes=VMEM_LIMIT)


def _bias_kernel(rb_ref, out_ref):
    h = pl.program_id(0)
    j = lax.broadcasted_iota(jnp.int32, (KT, TQ), 0)
    i = lax.broadcasted_iota(jnp.int32, (KT, TQ), 1)
    far = rb_ref[REL_BUCKETS - 1, h]

    def table(d):
        t = jnp.full((KT, TQ), (rb_ref[0, h] - far) * LOG2E, F32)
        for k in range(1, REL_BUCKETS):
            t = jnp.where(d >= _THR[k], (rb_ref[k, h] - far) * LOG2E, t)
        return t

    neg = jnp.full((KT, TQ), NEG_BIG, F32)
    out_ref[0, BT_OWN] = jnp.where(i >= j, table(i - j), neg)
    out_ref[0, BT_PREV] = table(i - j + KT)
    out_ref[0, BT_UPPER] = jnp.where(j > i, 0.0, neg)
    out_ref[0, BT_MASKED] = neg


def _bias_tiles(rel_bias):
    nh = rel_bias.shape[1]
    return pl.pallas_call(
        _bias_kernel,
        grid=(nh,),
        in_specs=[pl.BlockSpec(memory_space=pltpu.SMEM)],
        out_specs=pl.BlockSpec((1, N_BIAS_TILES, KT, TQ), lambda h: (h, 0, 0, 0)),
        out_shape=jax.ShapeDtypeStruct((nh, N_BIAS_TILES, KT, TQ), F32),
        compiler_params=_params(1),
        name="bias_tiles",
    )(rel_bias)


def _inproj_kernel(x_ref, g_ref, wt_ref, wf_ref, wc_ref, tok_ref, fm_ref, cmp_ref, *, chunk):
    h = _rms(x_ref[...], g_ref[...]).astype(BF16)
    c = _dot(h, wc_ref[...]).astype(cmp_ref.dtype)
    for i in range(cmp_ref.shape[0]):
        cmp_ref[i] = c[:, i * HEAD_DIM:(i + 1) * HEAD_DIM]
    n = wt_ref.shape[1]
    for c0 in range(0, n, chunk):
        c1 = min(c0 + chunk, n)
        tok_ref[:, c0:c1] = _dot(h, wt_ref[:, c0:c1]).astype(tok_ref.dtype)
    n = wf_ref.shape[0]
    for c0 in range(0, n, chunk):
        c1 = min(c0 + chunk, n)
        fm_ref[c0:c1, :] = _dot_nt(wf_ref[c0:c1, :], h).astype(fm_ref.dtype)


def _in_proj(x, g, w_tok, w_fm_t, w_cmp, tm):
    b, s, d = x.shape
    ncmp = CMP_W // HEAD_DIM
    return pl.pallas_call(
        functools.partial(_inproj_kernel, chunk=512),
        grid=(b, s // tm),
        in_specs=[pl.BlockSpec((None, tm, d), lambda i, j: (i, j, 0)),
                  pl.BlockSpec((1, d), lambda i, j: (0, 0)),
                  pl.BlockSpec(w_tok.shape, lambda i, j: (0, 0)),
                  pl.BlockSpec(w_fm_t.shape, lambda i, j: (0, 0)),
                  pl.BlockSpec(w_cmp.shape, lambda i, j: (0, 0))],
        out_specs=[pl.BlockSpec((None, tm, TOK_W), lambda i, j: (i, j, 0)),
                   pl.BlockSpec((None, FM_W, tm), lambda i, j: (i, 0, j)),
                   pl.BlockSpec((None, ncmp, tm, HEAD_DIM), lambda i, j: (i, 0, j, 0))],
        out_shape=[jax.ShapeDtypeStruct((b, s, TOK_W), BF16),
                   jax.ShapeDtypeStruct((b, FM_W, s), BF16),
                   jax.ShapeDtypeStruct((b, ncmp, s, HEAD_DIM), BF16)],
        compiler_params=_params(2),
        name="in_proj",
    )(x, g.reshape(1, d), w_tok, w_fm_t, w_cmp)


def _norm_matmul_kernel(x_ref, g_ref, w_ref, o_ref):
    o_ref[...] = _dot(_rms(x_ref[...], g_ref[...]).astype(BF16), w_ref[...]).astype(o_ref.dtype)


def _norm_matmul(x2d, g, w, tm, name):
    m, d = x2d.shape
    n = w.shape[1]
    return pl.pallas_call(
        _norm_matmul_kernel,
        grid=(m // tm,),
        in_specs=[pl.BlockSpec((tm, d), lambda i: (i, 0)),
                  pl.BlockSpec((1, d), lambda i: (0, 0)),
                  pl.BlockSpec((d, n), lambda i: (0, 0))],
        out_specs=pl.BlockSpec((tm, n), lambda i: (i, 0)),
        out_shape=jax.ShapeDtypeStruct((m, n), BF16),
        compiler_params=_params(1),
        name=name,
    )(x2d, g.reshape(1, d), w)


def _kmean_kernel(k_ref, o_ref):
    nb = o_ref.shape[0]
    for n in range(nb):
        blk = k_ref[n * MOBA_BLOCK:(n + 1) * MOBA_BLOCK, :].astype(F32)
        o_ref[n:n + 1, :] = jnp.mean(blk, axis=0, keepdims=True)


def _moba_kmean(tok):
    b, s, _ = tok.shape
    nb = s // MOBA_BLOCK
    return pl.pallas_call(
        _kmean_kernel,
        grid=(b,),
        in_specs=[pl.BlockSpec((None, s, MOBA_W), lambda i: (i, 0, COL_MK // MOBA_W))],
        out_specs=pl.BlockSpec((None, nb, MOBA_W), lambda i: (i, 0, 0)),
        out_shape=jax.ShapeDtypeStruct((b, nb, MOBA_W), F32),
        compiler_params=_params(1),
        name="moba_kmean",
    )(tok)


def _fill_v_with_ones(vt_ref, lo_ref, hi_ref, row0=0, cols=1024):
    s = vt_ref.shape[1]
    ones = jnp.ones((ONES_ROWS, cols), BF16)

    def body(c, carry):
        c0 = pl.multiple_of(c * cols, cols)
        lo_ref[0:HEAD_DIM, pl.ds(c0, cols)] = vt_ref[row0:row0 + HEAD_DIM, pl.ds(c0, cols)]
        lo_ref[HEAD_DIM:VROWS, pl.ds(c0, cols)] = ones
        hi_ref[0:HEAD_DIM, pl.ds(c0, cols)] = vt_ref[row0 + HEAD_DIM:row0 + 2 * HEAD_DIM, pl.ds(c0, cols)]
        hi_ref[HEAD_DIM:VROWS, pl.ds(c0, cols)] = ones
        return carry

    lax.fori_loop(0, s // cols, body, 0)


def _split_heads_t(q_t):
    row = lax.broadcasted_iota(jnp.int32, q_t.shape, 0)
    zero = jnp.zeros_like(q_t)
    return jnp.where(row < HEAD_DIM, q_t, zero), jnp.where(row >= HEAD_DIM, q_t, zero)


def _fill_k_with_block_ids(k_ref, lo_ref, hi_ref, block, nblock, lane0=0, rows=512):
    s = k_ref.shape[0]
    lane = lax.broadcasted_iota(jnp.int32, (rows, LANES), 1)
    shift = int(math.log2(block))

    def body(c, carry):
        r0 = pl.multiple_of(c * rows, rows)
        key = r0 + lax.broadcasted_iota(jnp.int32, (rows, LANES), 0)
        bid = lax.shift_right_logical(key, shift) & (nblock - 1)
        k = k_ref[pl.ds(r0, rows), lane0:lane0 + LANES]
        lo_ref[pl.ds(r0, rows), :] = jnp.where(lane < HEAD_DIM, k, (lane - HEAD_DIM == bid).astype(BF16))
        hi_ref[pl.ds(r0, rows), :] = jnp.where(lane >= HEAD_DIM, k, (lane == bid).astype(BF16))
        return carry

    lax.fori_loop(0, s // rows, body, 0)


def _q_with_mask_rows(q_rows, mask_rows, low_head):
    r = lax.broadcasted_iota(jnp.int32, (ONES_ROWS, TQ), 0)
    m = jnp.zeros((ONES_ROWS, TQ), F32)
    for b, row in enumerate(mask_rows):
        m = jnp.where(r == b, row, m)
    pad = jnp.zeros((HEAD_DIM - ONES_ROWS, TQ), BF16)
    parts = [q_rows, m.astype(BF16), pad] if low_head else [m.astype(BF16), pad, q_rows]
    return jnp.concatenate(parts, axis=0)


def _pipe(bufs, m_ref, acc_ref, nheads, a=None, b=None, c=None, c_last=False, group=1):
    s_refs, p_refs, mx_ref, al_ref = bufs
    rows = s_refs[0].shape[1]
    strip = min(STRIP, rows)

    def stage_c(h):
        c_slot, c_base, vt_fn = c
        acc_ref[c_base + h] = (al_ref[c_slot, h] * acc_ref[c_base + h]
                               + _dot(vt_fn(h), p_refs[c_slot][h]))

    for g0 in range(0, nheads, group):
        hs = range(g0, min(g0 + group, nheads))
        if c is not None and not c_last:
            for h in hs:
                stage_c(h)
        if a is not None:
            a_slot, score_fn, bias_fn = a
            raws = {h: score_fn(h) for h in hs}
        for h in hs:
            if b is not None:
                b_slot, b_base = b
                m_prev = m_ref[b_base + h]
                m_next = jnp.maximum(m_prev, mx_ref[b_slot, h])
                al_ref[b_slot, h] = jnp.exp2(m_prev - m_next)
                m_ref[b_base + h] = m_next
            mx = None
            strips = [slice(i * strip, (i + 1) * strip) for i in range(rows // strip)]
            if b is not None:
                for r in strips:
                    p_refs[b_slot][h, r, :] = jnp.exp2(s_refs[b_slot][h, r, :] - m_next).astype(BF16)
            if a is not None:
                for i, r in enumerate(strips):
                    s = raws[h][r, :] if bias_fn is None else raws[h][r, :] + bias_fn(h, i)
                    s_refs[a_slot][h, r, :] = s
                    cmax = jnp.max(s, axis=0, keepdims=True)
                    mx = cmax if mx is None else jnp.maximum(mx, cmax)
            if a is not None:
                mx_ref[a_slot, h] = mx
        if c is not None and c_last:
            for h in hs:
                stage_c(h)


def _reset_state(bufs, m_ref, acc_ref):
    _, p_refs, _, al_ref = bufs
    m_ref[...] = jnp.full(m_ref.shape, NEG_BIG, F32)
    acc_ref[...] = jnp.zeros(acc_ref.shape, F32)
    p_refs[1][...] = jnp.zeros(p_refs[1].shape, BF16)
    al_ref[...] = jnp.ones(al_ref.shape, F32)


def _pair_output(acc_lo, acc_hi):
    o_lo = acc_lo[0:HEAD_DIM, :] / acc_lo[HEAD_DIM:HEAD_DIM + 1, :]
    o_hi = acc_hi[0:HEAD_DIM, :] / acc_hi[HEAD_DIM:HEAD_DIM + 1, :]
    return jnp.transpose(jnp.concatenate([o_lo, o_hi], axis=0))


def _top_k_rows(val, n_f, k, sentinel):
    chosen = jnp.zeros(val.shape, F32)
    for _ in range(k):
        mx = jnp.max(val, axis=0, keepdims=True)
        idx = jnp.min(jnp.where(val == mx, n_f, sentinel), axis=0, keepdims=True)
        pick = n_f == idx
        chosen = jnp.where(pick, 1.0, chosen)
        val = jnp.where(pick, -jnp.inf, val)
    return chosen


def _moba_kernel(qt_ref, k_ref, vt_ref, km_ref, bt_ref, o_ref, *scratch):
    npair = qt_ref.shape[0] // LANES
    nh = 2 * npair
    k_refs, v_refs = scratch[0:nh], scratch[nh:2 * nh]
    mb_ref, qs_ref, m_ref, acc_ref, s0_ref, s1_ref, p0_ref, p1_ref, mx_ref, al_ref = scratch[2 * nh:]
    cur = pl.program_id(2)
    bufs = ((s0_ref, s1_ref), (p0_ref, p1_ref), mx_ref, al_ref)
    nb = km_ref.shape[0]
    sk = s0_ref.shape[1]
    strip = min(STRIP, sk)

    @pl.when(cur == 0)
    def _():
        for pp in range(npair):
            _fill_v_with_ones(vt_ref, v_refs[2 * pp], v_refs[2 * pp + 1], row0=pp * LANES)
            _fill_k_with_block_ids(k_ref, k_refs[2 * pp], k_refs[2 * pp + 1], MOBA_BLOCK, 2, lane0=pp * LANES)

    n_f = lax.broadcasted_iota(jnp.int32, (nb, TQ), 0).astype(F32)
    past = n_f < cur.astype(F32)
    for pp in range(npair):
        km = km_ref[:, pp * LANES:(pp + 1) * LANES].astype(BF16)
        for hh, qh in enumerate(_split_heads_t(qt_ref[pp * LANES:(pp + 1) * LANES, :])):
            gate = jnp.where(past, _dot(km, qh), -jnp.inf)
            chosen = _top_k_rows(gate, n_f, min(MOBA_TOPK, nb), float(nb))
            mb_ref[2 * pp + hh] = jnp.where((chosen > 0.5) & past, 0.0, NEG_BIG)
            qs_ref[2 * pp + hh] = qh
    _reset_state(bufs, m_ref, acc_ref)

    def first_block(u):
        return jnp.maximum(cur - 2 * u - 1, 0)

    def score_fn(u, first_rows, second_rows):
        idx = first_block(u)
        ks = pl.multiple_of(idx * KT, KT)
        first_even = (idx & 1) == 0

        def f(h):
            low = h % 2 == 0
            even = jnp.where(first_even, first_rows[h], second_rows[h])
            odd = jnp.where(first_even, second_rows[h], first_rows[h])
            q_rows = qs_ref[h, 0:HEAD_DIM, :] if low else qs_ref[h, HEAD_DIM:2 * HEAD_DIM, :]
            return _dot(k_refs[h][pl.ds(ks, sk), :], _q_with_mask_rows(q_rows, (even, odd), low))
        return f

    def vt_fn(u):
        ks = pl.multiple_of(first_block(jnp.maximum(u, 0)) * KT, KT)
        return lambda h: v_refs[h][:, pl.ds(ks, sk)]

    has_prev = cur >= 1
    kind_lo = jnp.where(has_prev, BT_PREV, BT_OWN)
    kind_hi = jnp.where(has_prev, BT_OWN, BT_MASKED)
    keep = jnp.where(has_prev, 1.0, 0.0)
    prev_rows = [mb_ref[h, pl.ds(jnp.maximum(cur - 1, 0), 1), :] * keep for h in range(nh)]
    no_rows = [jnp.zeros((1, TQ), F32)] * nh

    def near_bias(h, i):
        if strip == sk:
            return jnp.concatenate([bt_ref[h, kind_lo], bt_ref[h, kind_hi]], axis=0)
        half, part = divmod(i, KT // strip)
        return bt_ref[h, kind_lo if half == 0 else kind_hi, part * strip:(part + 1) * strip, :]

    def far_scores(u):
        b_lo = cur - 2 * u - 1
        idx = jnp.maximum(b_lo, 0)
        pen_lo = jnp.where(b_lo >= -1, 0.0, NEG_BIG)
        pen_hi = jnp.where(b_lo >= 0, 0.0, NEG_BIG)
        return score_fn(u, [mb_ref[h, pl.ds(idx, 1), :] + pen_lo for h in range(nh)],
                        [mb_ref[h, pl.ds(idx + 1, 1), :] + pen_hi for h in range(nh)])

    pipe = functools.partial(_pipe, bufs, m_ref, acc_ref, nh, c_last=True)
    pipe(a=(0, score_fn(0, prev_rows, no_rows), near_bias))
    n_far = lax.shift_right_logical(cur, 1)

    def two_jobs(t, carry):
        u0 = 2 * t + 1
        pipe(a=(1, far_scores(u0), None), b=(0, 0), c=(1, 0, vt_fn(u0 - 2)))
        pipe(a=(0, far_scores(u0 + 1), None), b=(1, 0), c=(0, 0, vt_fn(u0 - 1)))
        return carry

    n_iter = lax.shift_right_logical(n_far + 1, 1)
    lax.fori_loop(0, n_iter, two_jobs, 0)
    pipe(b=(0, 0), c=(1, 0, vt_fn(2 * n_iter - 1)))
    pipe(c=(0, 0, vt_fn(2 * n_iter)))
    for pp in range(npair):
        o_ref[:, pp * LANES:(pp + 1) * LANES] = _pair_output(
            acc_ref[2 * pp], acc_ref[2 * pp + 1]).astype(o_ref.dtype)


def _moba_attention(tok, fm, kmean, bias_tiles, pairs_per_step=2):
    b, s, _ = tok.shape
    nb = s // MOBA_BLOCK
    nh = 2 * pairs_per_step
    w = pairs_per_step * LANES
    sk = 2 * KT
    return pl.pallas_call(
        _moba_kernel,
        grid=(b, MOBA_W // w, s // TQ),
        in_specs=[
            pl.BlockSpec((None, w, TQ), lambda i, p, c: (i, ROW_MQ // w + p, c)),
            pl.BlockSpec((None, s, w), lambda i, p, c: (i, 0, COL_MK // w + p)),
            pl.BlockSpec((None, w, s), lambda i, p, c: (i, ROW_MV // w + p, 0)),
            pl.BlockSpec((None, nb, w), lambda i, p, c: (i, 0, p)),
            pl.BlockSpec((nh, N_BIAS_TILES, KT, TQ), lambda i, p, c: (p, 0, 0, 0)),
        ],
        out_specs=pl.BlockSpec((None, TQ, w), lambda i, p, c: (i, c, p)),
        out_shape=jax.ShapeDtypeStruct((b, s, MOBA_W), BF16),
        scratch_shapes=[pltpu.VMEM((s, LANES), BF16)] * nh + [pltpu.VMEM((VROWS, s), BF16)] * nh + [
            pltpu.VMEM((nh, nb, TQ), F32),
            pltpu.VMEM((nh, LANES, TQ), BF16),
            pltpu.VMEM((nh, 1, TQ), F32),
            pltpu.VMEM((nh, VROWS, TQ), F32),
            pltpu.VMEM((nh, sk, TQ), F32), pltpu.VMEM((nh, sk, TQ), F32),
            pltpu.VMEM((nh, sk, TQ), BF16), pltpu.VMEM((nh, sk, TQ), BF16),
            pltpu.VMEM((2, nh, 1, TQ), F32), pltpu.VMEM((2, nh, 1, TQ), F32),
        ],
        compiler_params=_params(3),
        name="moba_attention",
    )(fm, tok, fm, kmean, bias_tiles)


def _compress_kernel(ck_ref, cv_ref, pk_ref, pv_ref, w1k_ref, w1v_ref, w2k_ref, w2v_ref,
                     kc_ref, vct_ref):
    nch = ck_ref.shape[1]
    half = CMP_STRIDE * HEAD_DIM

    def run(c_ref, pos_ref, w1_ref, w2_ref):
        acc = jnp.zeros((nch, LANES), F32)
        for g in range(NSA_GROUPS):
            c = c_ref[g].astype(F32)
            lo = (c + pos_ref[0:1, :]).astype(BF16)
            hi = (c + pos_ref[1:2, :]).astype(BF16)
            pre = _dot(lo, w1_ref[:half, :]) + pltpu.roll(_dot(hi, w1_ref[half:, :]), nch - 1, axis=0)
            acc = acc + _dot(jax.nn.gelu(pre).astype(BF16), w2_ref[g])
        return acc

    kc_ref[...] = run(ck_ref, pk_ref, w1k_ref, w2k_ref).astype(kc_ref.dtype)
    vct_ref[...] = jnp.transpose(run(cv_ref, pv_ref, w1v_ref, w2v_ref)).astype(vct_ref.dtype)


def _compress(chunks, pos_k, pos_v, w1k, w1v, w2k, w2v):
    b, _, nch, width = chunks.shape
    g = NSA_GROUPS
    full = lambda shape: pl.BlockSpec(shape, lambda i: (0,) * len(shape))
    return pl.pallas_call(
        _compress_kernel,
        grid=(b,),
        in_specs=[pl.BlockSpec((None, g, nch, width), lambda i: (i, 0, 0, 0)),
                  pl.BlockSpec((None, g, nch, width), lambda i: (i, 1, 0, 0)),
                  full(pos_k.shape), full(pos_v.shape), full(w1k.shape), full(w1v.shape),
                  full(w2k.shape), full(w2v.shape)],
        out_specs=[pl.BlockSpec((None, nch, LANES), lambda i: (i, 0, 0)),
                   pl.BlockSpec((None, LANES, nch), lambda i: (i, 0, 0))],
        out_shape=[jax.ShapeDtypeStruct((b, nch, LANES), BF16),
                   jax.ShapeDtypeStruct((b, LANES, nch), BF16)],
        compiler_params=_params(1),
        name="nsa_compress",
    )(chunks, chunks, pos_k, pos_v, w1k, w1v, w2k, w2v)


def _cmp_select_kernel(qt_ref, kc_ref, vct_ref, mimp_ref, o_ref, sel_ref, *, nsb, nq):
    ci = pl.program_id(1)
    nch = kc_ref.shape[0]
    nsbp = mimp_ref.shape[0]
    parts = 4 if (nch // 4) % LANES == 0 and (nsbp // 4) % 8 == 0 else 1
    for v in range(1, parts + 1):
        @pl.when((ci * parts >= (v - 1) * nq) & (ci * parts < v * nq))
        def _(v=v):
            _cmp_select_body(qt_ref, kc_ref, vct_ref, mimp_ref, o_ref, sel_ref,
                             nsb=nsb, nch=nch * v // parts, nrow=nsbp * v // parts)


def _cmp_select_body(qt_ref, kc_ref, vct_ref, mimp_ref, o_ref, sel_ref, *, nsb, nch, nrow):
    qs = pl.program_id(1) * TQ
    n_row = lax.broadcasted_iota(jnp.int32, (nch, TQ), 0)
    t_col = qs + lax.broadcasted_iota(jnp.int32, (nch, TQ), 1)
    hidden = jnp.where((n_row * CMP_STRIDE + (CMP_BLOCK - 1)) <= t_col, 0.0, NEG_BIG)
    any_visible = t_col[0:1, :] >= (CMP_BLOCK - 1)
    kc = kc_ref[0:nch, :]
    vct = vct_ref[:, 0:nch]
    vrow = lax.broadcasted_iota(jnp.int32, vct.shape, 0)
    mimp = mimp_ref[0:nrow, 0:nch]

    j_row = lax.broadcasted_iota(jnp.int32, (nrow, TQ), 0)
    t_sel = qs + lax.broadcasted_iota(jnp.int32, (nrow, TQ), 1)
    cur = lax.shift_right_logical(t_sel, int(math.log2(SLC_BLOCK)))
    j_f = j_row.astype(F32)
    forced = (j_row == 0) | (j_row == cur) | (j_row == cur - 1)
    allowed = (j_row <= cur) & (j_row < nsb)

    q_heads = [_split_heads_t(qt_ref[r * LANES:(r + 1) * LANES, :]) for r in range(NSA_HPG)]
    out_t = [jnp.zeros((LANES, TQ), F32) for _ in range(NSA_HPG)]
    for g in range(NSA_GROUPS):
        vin_g = (vrow < HEAD_DIM) if g == 0 else (vrow >= HEAD_DIM)
        vct_g = jnp.where(vin_g, vct, jnp.zeros_like(vct))
        p_grp = jnp.zeros((nch, TQ), F32)
        for r in range(NSA_HPG):
            s = _dot(kc, q_heads[r][g]) + hidden
            e = jnp.exp2(s - jnp.max(s, axis=0, keepdims=True))
            p = e * jnp.where(any_visible, 1.0 / jnp.sum(e, axis=0, keepdims=True), 0.0)
            p_grp = p_grp + p
            out_t[r] = out_t[r] + _dot(vct_g, p.astype(BF16))
        p_hi = p_grp.astype(BF16)
        p_lo = (p_grp - p_hi.astype(F32)).astype(BF16)
        imp = _dot(mimp, p_hi) + _dot(mimp, p_lo)
        val = jnp.where(forced & allowed, jnp.inf, jnp.where(allowed, imp, -jnp.inf))
        chosen = _top_k_rows(val, j_f, min(SLC_TOPN, nsb), float(nrow))
        sel_ref[g, 0:nrow, :] = jnp.where(allowed & (chosen > 0.5), 0.0, NEG_BIG)
        if nrow < sel_ref.shape[1]:
            sel_ref[g, nrow:, :] = jnp.full((sel_ref.shape[1] - nrow, TQ), NEG_BIG, F32)
    for r in range(NSA_HPG):
        o_ref[:, r * LANES:(r + 1) * LANES] = jnp.transpose(out_t[r]).astype(o_ref.dtype)


def _cmp_select(fm, kc, vct, mimp_t, nsb):
    b, _, s = fm.shape
    nch = kc.shape[1]
    nsbp = mimp_t.shape[0]
    return pl.pallas_call(
        functools.partial(_cmp_select_kernel, nsb=nsb, nq=s // TQ),
        grid=(b, s // TQ),
        in_specs=[pl.BlockSpec((None, NSA_W, TQ), lambda i, c: (i, ROW_NQ // NSA_W, c)),
                  pl.BlockSpec((None, nch, LANES), lambda i, c: (i, 0, 0)),
                  pl.BlockSpec((None, LANES, nch), lambda i, c: (i, 0, 0)),
                  pl.BlockSpec((nsbp, nch), lambda i, c: (0, 0))],
        out_specs=[pl.BlockSpec((None, TQ, NSA_W), lambda i, c: (i, c, 0)),
                   pl.BlockSpec((None, NSA_GROUPS, nsbp, TQ), lambda i, c: (i, 0, 0, c))],
        out_shape=[jax.ShapeDtypeStruct((b, s, NSA_W), BF16),
                   jax.ShapeDtypeStruct((b, NSA_GROUPS, nsbp, s), F32)],
        compiler_params=_params(2),
        name="nsa_cmp_select",
    )(fm, kc, vct, mimp_t)


def _slc_win_kernel(qt_ref, ks_ref, vst_ref, kw_ref, vwt_ref, sel_ref, bt_ref,
                    oslc_ref, owin_ref,
                    ks0_ref, ks1_ref, vs0_ref, vs1_ref, vw0_ref, vw1_ref, qs_ref, m_ref, acc_ref,
                    s0_ref, s1_ref, p0_ref, p1_ref, mx_ref, al_ref):
    cur = pl.program_id(1)
    bufs = ((s0_ref, s1_ref), (p0_ref, p1_ref), mx_ref, al_ref)
    per_tile = KT // SLC_BLOCK

    @pl.when(cur == 0)
    def _():
        _fill_v_with_ones(vst_ref, vs0_ref, vs1_ref)
        _fill_v_with_ones(vwt_ref, vw0_ref, vw1_ref)
        _fill_k_with_block_ids(ks_ref, ks0_ref, ks1_ref, SLC_BLOCK, per_tile)

    nh = NSA_HEADS
    for r in range(NSA_HPG):
        lo, hi = _split_heads_t(qt_ref[r * LANES:(r + 1) * LANES, :])
        qs_ref[r] = lo
        qs_ref[NSA_HPG + r] = hi
    _reset_state(bufs, m_ref, acc_ref)
    ks_refs = (ks0_ref, ks1_ref)
    vs_refs = (vs0_ref, vs1_ref)
    vw_refs = (vw0_ref, vw1_ref)
    sel_base, win_base = 0, nh

    def tile_start(n):
        return pl.multiple_of(jnp.maximum(n, 0) * KT, KT)

    def win_scores(n):
        ks = tile_start(n)
        return lambda h: _dot(kw_ref[pl.ds(ks, KT), :], qs_ref[h])

    def sel_scores(n, penalty):
        ks = tile_start(n)
        base = jnp.maximum(n, 0) * per_tile
        rows = [[sel_ref[g, pl.ds(base + b, 1), :] + penalty for b in range(per_tile)]
                for g in range(NSA_GROUPS)]

        def f(h):
            g = h // NSA_HPG
            q_rows = qs_ref[h, 0:HEAD_DIM, :] if g == 0 else qs_ref[h, HEAD_DIM:2 * HEAD_DIM, :]
            return _dot(ks_refs[g][pl.ds(ks, KT), :], _q_with_mask_rows(q_rows, rows[g], g == 0))
        return f

    def vt_fn(v_refs, n):
        ks = tile_start(n)
        return lambda h: v_refs[h // NSA_HPG][:, pl.ds(ks, KT)]

    def near_bias(kind):
        strip = min(STRIP, KT)
        return lambda h, i: bt_ref[h, kind, i * strip:(i + 1) * strip, :]

    pipe = functools.partial(_pipe, bufs, m_ref, acc_ref, nh, c_last=True)
    kind_prev = jnp.where(cur >= 1, BT_PREV, BT_MASKED)
    kind_upper = jnp.where(cur >= 2, BT_UPPER, BT_MASKED)
    pipe(a=(0, win_scores(cur), near_bias(BT_OWN)))
    pipe(a=(1, win_scores(cur - 1), near_bias(kind_prev)), b=(0, win_base))
    pipe(a=(0, win_scores(cur - 2), near_bias(kind_upper)), b=(1, win_base),
         c=(0, win_base, vt_fn(vw_refs, cur)))
    pipe(a=(1, sel_scores(cur, 0.0), near_bias(BT_OWN)), b=(0, win_base),
         c=(1, win_base, vt_fn(vw_refs, cur - 1)))
    pipe(a=(0, sel_scores(cur - 1, 0.0), near_bias(kind_prev)), b=(1, sel_base),
         c=(0, win_base, vt_fn(vw_refs, cur - 2)))

    def far_scores(n):
        return sel_scores(n, jnp.where(n >= 0, 0.0, NEG_BIG))

    def two_tiles(t, carry):
        n0 = cur - 2 - 2 * t
        pipe(a=(1, far_scores(n0), None), b=(0, sel_base), c=(1, sel_base, vt_fn(vs_refs, n0 + 2)))
        pipe(a=(0, far_scores(n0 - 1), None), b=(1, sel_base), c=(0, sel_base, vt_fn(vs_refs, n0 + 1)))
        return carry

    n_iter = lax.shift_right_logical(jnp.maximum(cur - 1, 0) + 1, 1)
    lax.fori_loop(0, n_iter, two_tiles, 0)
    pipe(b=(0, sel_base), c=(1, sel_base, vt_fn(vs_refs, cur - 2 * n_iter)))
    pipe(c=(0, sel_base, vt_fn(vs_refs, cur - 1 - 2 * n_iter)))

    for r in range(NSA_HPG):
        cols = slice(r * LANES, (r + 1) * LANES)
        oslc_ref[:, cols] = _pair_output(acc_ref[r], acc_ref[NSA_HPG + r]).astype(oslc_ref.dtype)
        owin_ref[:, cols] = _pair_output(acc_ref[nh + r], acc_ref[nh + NSA_HPG + r]).astype(owin_ref.dtype)


def _slc_win(tok, fm, sel, bias_tiles):
    b, s, _ = tok.shape
    nsbp = sel.shape[2]
    return pl.pallas_call(
        _slc_win_kernel,
        grid=(b, s // TQ),
        in_specs=[pl.BlockSpec((None, NSA_W, TQ), lambda i, c: (i, ROW_NQ // NSA_W, c)),
                  pl.BlockSpec((None, s, LANES), lambda i, c: (i, 0, COL_NKS // LANES)),
                  pl.BlockSpec((None, LANES, s), lambda i, c: (i, ROW_NVS // LANES, 0)),
                  pl.BlockSpec((None, s, LANES), lambda i, c: (i, 0, COL_NKW // LANES)),
                  pl.BlockSpec((None, LANES, s), lambda i, c: (i, ROW_NVW // LANES, 0)),
                  pl.BlockSpec((None, NSA_GROUPS, nsbp, TQ), lambda i, c: (i, 0, 0, c)),
                  pl.BlockSpec((NSA_HEADS, N_BIAS_TILES, KT, TQ), lambda i, c: (1, 0, 0, 0))],
        out_specs=[pl.BlockSpec((None, TQ, NSA_W), lambda i, c: (i, c, 0)),
                   pl.BlockSpec((None, TQ, NSA_W), lambda i, c: (i, c, 0))],
        out_shape=[jax.ShapeDtypeStruct((b, s, NSA_W), BF16),
                   jax.ShapeDtypeStruct((b, s, NSA_W), BF16)],
        scratch_shapes=[pltpu.VMEM((s, LANES), BF16)] * 2 + [pltpu.VMEM((VROWS, s), BF16)] * 4 + [
            pltpu.VMEM((NSA_HEADS, LANES, TQ), BF16),
            pltpu.VMEM((2 * NSA_HEADS, 1, TQ), F32),
            pltpu.VMEM((2 * NSA_HEADS, VROWS, TQ), F32),
            pltpu.VMEM((NSA_HEADS, KT, TQ), F32), pltpu.VMEM((NSA_HEADS, KT, TQ), F32),
            pltpu.VMEM((NSA_HEADS, KT, TQ), BF16), pltpu.VMEM((NSA_HEADS, KT, TQ), BF16),
            pltpu.VMEM((2, NSA_HEADS, 1, TQ), F32), pltpu.VMEM((2, NSA_HEADS, 1, TQ), F32)],
        compiler_params=_params(2),
        name="nsa_slc_win",
    )(fm, tok, fm, tok, fm, sel, bias_tiles)


def _merge_kernel(x_ref, oa_ref, oc_ref, os_ref, ow_ref, ng_ref, ga_ref, gb_ref,
                  eg_ref, wa_ref, wb_ref, wo_ref, o_ref):
    sig = jax.nn.sigmoid(ng_ref[...].astype(F32))
    s_hi = sig.astype(BF16)
    s_lo = (sig - s_hi.astype(F32)).astype(BF16)
    gexp = _dot(jnp.concatenate([s_hi, s_lo], axis=1), eg_ref[...])
    a = _dot(oa_ref[...], wa_ref[...])
    o_b = (gexp[:, :NSA_W] * oc_ref[...].astype(F32)
           + gexp[:, NSA_W:2 * NSA_W] * os_ref[...].astype(F32)
           + gexp[:, 2 * NSA_W:] * ow_ref[...].astype(F32))
    bb = _dot(o_b.astype(BF16), wb_ref[...])
    merged = (jax.nn.sigmoid(ga_ref[...].astype(F32)) * a
              + jax.nn.sigmoid(gb_ref[...].astype(F32)) * bb)
    o_ref[...] = x_ref[...] + _dot(merged.astype(BF16), wo_ref[...])


def _merge(x2d, tok2d, o_a, o_c, o_s, o_w, egate, wa, wb, wo, tm):
    m, d = x2d.shape
    row = lambda w, col=0: pl.BlockSpec((tm, w), lambda i, col=col: (i, col))
    full = lambda a: pl.BlockSpec(a.shape, lambda i: (0, 0))
    return pl.pallas_call(
        _merge_kernel,
        grid=(m // tm,),
        in_specs=[row(d), row(MOBA_W), row(NSA_W), row(NSA_W), row(NSA_W),
                  row(GATE_PAD, COL_NGATE // GATE_PAD),
                  row(D_MODEL, COL_GA // D_MODEL), row(D_MODEL, COL_GB // D_MODEL),
                  full(egate), full(wa), full(wb), full(wo)],
        out_specs=row(d),
        out_shape=jax.ShapeDtypeStruct((m, d), F32),
        compiler_params=_params(1),
        name="merge_out",
    )(x2d, o_a, o_c, o_s, o_w, tok2d, tok2d, tok2d, egate, wa, wb, wo)


def _xattn_kernel(x_ref, g_ref, wq_ref, kv_ref, wo_ref, o_ref, *, nsplit):
    part = x_ref.shape[0] // nsplit
    g = g_ref[...]
    rows = [slice(i * part, (i + 1) * part) for i in range(nsplit)]
    xs = [x_ref[r, :] for r in rows]
    qs = [_dot(_rms(x, g).astype(BF16), wq_ref[...]).astype(BF16) for x in xs]
    for r, x, q in zip(rows, xs, qs):
        cols = [slice(h * XATTN_HEAD_DIM, (h + 1) * XATTN_HEAD_DIM) for h in range(XATTN_HEADS)]
        scores = [_dot_nt(q[:, c], kv_ref[:, c]) for c in cols]
        heads = []
        for h, s in enumerate(scores):
            v = kv_ref[:, XATTN_W + h * XATTN_HEAD_DIM:XATTN_W + (h + 1) * XATTN_HEAD_DIM]
            e = jnp.exp2(s - jnp.max(s, axis=1, keepdims=True))
            p = e * (1.0 / jnp.sum(e, axis=1, keepdims=True))
            heads.append(_dot(p.astype(BF16), v).astype(BF16))
        o_ref[r, :] = x + _dot(jnp.concatenate(heads, axis=1), wo_ref[...])


def _xattn(x3d, g, wq, memkv, wo, tm):
    b, s, d = x3d.shape
    mem_len = memkv.shape[1]
    return pl.pallas_call(
        functools.partial(_xattn_kernel, nsplit=1),
        grid=(b, s // tm),
        in_specs=[pl.BlockSpec((None, tm, d), lambda i, j: (i, j, 0)),
                  pl.BlockSpec((1, d), lambda i, j: (0, 0)),
                  pl.BlockSpec(wq.shape, lambda i, j: (0, 0)),
                  pl.BlockSpec((None, mem_len, 2 * XATTN_W), lambda i, j: (i, 0, 0)),
                  pl.BlockSpec(wo.shape, lambda i, j: (0, 0))],
        out_specs=pl.BlockSpec((None, tm, d), lambda i, j: (i, j, 0)),
        out_shape=jax.ShapeDtypeStruct((b, s, d), F32),
        compiler_params=_params(2),
        name="xattn",
    )(x3d, g.reshape(1, d), wq, memkv, wo)


def _ffn_kernel(x_ref, halo_ref, g_ref, wg_ref, wu_ref, cw_ref, cb_ref, wd_ref, gf_ref, o_ref,
                y_ref, *, chunk, halo_rows):
    j = pl.program_id(1)
    x = x_ref[...]
    tm = x.shape[0]
    g = g_ref[...]
    hf = _rms(x, g).astype(BF16)
    keep = jnp.where(j > 0, 1.0, 0.0)
    halo = (_rms(halo_ref[...], g) * keep).astype(BF16)
    ext = jnp.concatenate([halo, hf], axis=0)
    row = lax.broadcasted_iota(jnp.int32, (tm, chunk), 0)
    starts = list(range(0, D_FF, chunk))

    def gate_up(c0):
        cols = slice(c0, c0 + chunk)
        return _dot(ext, wg_ref[:, cols]), _dot(hf, wu_ref[:, cols])

    nxt = gate_up(starts[0])
    for k, c0 in enumerate(starts):
        cols = slice(c0, c0 + chunk)
        ue, up = nxt
        if k + 1 < len(starts):
            nxt = gate_up(starts[k + 1])
        u = ue[halo_rows:, :]
        prev1 = ue[halo_rows - 1:halo_rows, :]
        prev2 = ue[halo_rows - 2:halo_rows - 1, :]
        u1 = jnp.where(row == 0, prev1, pltpu.roll(u, 1, axis=0))
        u2 = jnp.where(row == 0, prev2, jnp.where(row == 1, prev1, pltpu.roll(u, 2, axis=0)))
        a = cw_ref[0:1, cols] * u2 + cw_ref[1:2, cols] * u1 + cw_ref[2:3, cols] * u + cb_ref[:, cols]
        y_ref[:, cols] = (jax.nn.gelu(a) * up).astype(BF16)
    o_ref[...] = _rms(x + _dot(y_ref[...], wd_ref[...]), gf_ref[...])


def _ffn(x3d, g, wg, wu, cw, cb, wd, gf, tm):
    b, s, d = x3d.shape
    halo_rows = 16
    per = tm // halo_rows
    return pl.pallas_call(
        functools.partial(_ffn_kernel, chunk=256, halo_rows=halo_rows),
        grid=(b, s // tm),
        in_specs=[pl.BlockSpec((None, tm, d), lambda i, j: (i, j, 0)),
                  pl.BlockSpec((None, halo_rows, d), lambda i, j: (i, jnp.maximum(j * per - 1, 0), 0)),
                  pl.BlockSpec((1, d), lambda i, j: (0, 0)),
                  pl.BlockSpec(wg.shape, lambda i, j: (0, 0)),
                  pl.BlockSpec(wu.shape, lambda i, j: (0, 0)),
                  pl.BlockSpec(cw.shape, lambda i, j: (0, 0)),
                  pl.BlockSpec((1, D_FF), lambda i, j: (0, 0)),
                  pl.BlockSpec(wd.shape, lambda i, j: (0, 0)),
                  pl.BlockSpec((1, d), lambda i, j: (0, 0))],
        out_specs=pl.BlockSpec((None, tm, d), lambda i, j: (i, j, 0)),
        out_shape=jax.ShapeDtypeStruct((b, s, d), F32),
        scratch_shapes=[pltpu.VMEM((tm, D_FF), BF16)],
        compiler_params=_params(2),
        name="conv_ffn",
    )(x3d, x3d, g.reshape(1, d), wg, wu, cw, cb.reshape(1, D_FF), wd, gf.reshape(1, d))


def _nq_perm():
    idx = []
    for r in range(NSA_HPG):
        for g in range(NSA_GROUPS):
            h = g * NSA_HPG + r
            idx.extend(range(h * HEAD_DIM, (h + 1) * HEAD_DIM))
    return np.asarray(idx, np.int32)


def _gate_expand():
    e = np.zeros((GATE_PAD, 3 * NSA_W), np.float32)
    for br in range(3):
        for g in range(NSA_GROUPS):
            for r in range(NSA_HPG):
                c0 = br * NSA_W + r * LANES + g * HEAD_DIM
                e[br * NSA_HEADS + g * NSA_HPG + r, c0:c0 + HEAD_DIM] = 1.0
    return e


def _importance_matrix(nch, nsb, nsbp):
    ratio = SLC_BLOCK // CMP_STRIDE
    nc = nch - 1
    m = np.zeros((nsbp, nch), np.float32)
    for j in range(nsb):
        for a in range(ratio):
            for bb in range(CMP_BLOCK // CMP_STRIDE):
                c = j * ratio + a - bb
                if 0 <= c < nc:
                    m[j, c] += 1.0
    return m


def _split_w_in(w_in):
    widths = [MOBA_W] * 3 + [NSA_W] + [NSA_KV_W] * 6 + [3 * NSA_HEADS, D_MODEL, D_MODEL]
    cuts = np.cumsum(widths)[:-1]
    mq, mk, mv, nq, nkc, nvc, nks, nvs, nkw, nvw, ngate, ga, gb = jnp.split(w_in, cuts, axis=1)
    ngate = jnp.pad(ngate, ((0, 0), (0, GATE_PAD - 3 * NSA_HEADS)))
    q_scale = HEAD_DIM ** -0.5 * LOG2E
    mq, nq = mq * q_scale, nq * q_scale
    w_tok = jnp.concatenate([ga, gb, mk, nks, nkw, ngate], axis=1)
    w_fm = jnp.concatenate([mq, mv, nq[:, _nq_perm()], nvs, nvw], axis=1)
    w_cmp = jnp.concatenate([nkc, nvc], axis=1)
    return w_tok.astype(BF16), jnp.transpose(w_fm).astype(BF16), w_cmp.astype(BF16)


def _layer(x, mem, bias_tiles, norm_mix_g, w_in, cmp_pos_k, cmp_w1_k, cmp_w2_k, cmp_pos_v,
           cmp_w1_v, cmp_w2_v, w_branch_a, w_branch_b, w_out, norm_xattn_g, norm_mem_g, w_xq, w_xkv, w_xo,
           norm_ffn_g, w_gate, w_up, conv_w, conv_b, w_down, norm_out_g):
    b, s, d = x.shape
    assert s % 1024 == 0 and d == D_MODEL
    rows = b * s
    nch = s // CMP_STRIDE
    nsb = s // SLC_BLOCK
    nsbp = max(LANES, nsb)

    w_tok, w_fm_t, w_cmp = _split_w_in(w_in)
    tok, fm, cmp_kv = _in_proj(x, norm_mix_g, w_tok, w_fm_t, w_cmp, tm=512)
    tok2d = tok.reshape(rows, TOK_W)

    kmean = _moba_kmean(tok)
    o_a = _moba_attention(tok, fm, kmean, bias_tiles)

    def w2_pad(w2):
        z = jnp.zeros_like(w2)
        return jnp.stack([jnp.concatenate([w2, z], axis=1), jnp.concatenate([z, w2], axis=1)]).astype(BF16)

    kc, vct = _compress(cmp_kv.reshape(b, CMP_W // HEAD_DIM, nch, CMP_STRIDE * HEAD_DIM),
                        cmp_pos_k.reshape(2, -1), cmp_pos_v.reshape(2, -1),
                        cmp_w1_k.astype(BF16), cmp_w1_v.astype(BF16), w2_pad(cmp_w2_k), w2_pad(cmp_w2_v))
    mimp_t = jnp.asarray(_importance_matrix(nch, nsb, nsbp), BF16)
    o_c, sel = _cmp_select(fm, kc, vct, mimp_t, nsb)
    o_s, o_w = _slc_win(tok, fm, sel, bias_tiles)

    x1 = _merge(x.reshape(rows, d), tok2d, o_a.reshape(rows, MOBA_W), o_c.reshape(rows, NSA_W),
                o_s.reshape(rows, NSA_W), o_w.reshape(rows, NSA_W), jnp.asarray(np.concatenate([_gate_expand()] * 2, axis=0), BF16),
                w_branch_a.astype(BF16), w_branch_b[_nq_perm(), :].astype(BF16), w_out.astype(BF16), tm=512)

    mem_len = mem.shape[1]
    memkv = _norm_matmul(mem.reshape(b * mem_len, d), norm_mem_g, w_xkv.astype(BF16), tm=mem_len,
                         name="mem_kv").reshape(b, mem_len, 2 * XATTN_W)
    xq_scale = XATTN_HEAD_DIM ** -0.5 * LOG2E
    x2 = _xattn(x1.reshape(b, s, d), norm_xattn_g, (w_xq * xq_scale).astype(BF16), memkv,
                w_xo.astype(BF16), tm=512)

    return _ffn(x2, norm_ffn_g, w_gate.astype(BF16), w_up.astype(BF16), conv_w, conv_b,
                w_down.astype(BF16), norm_out_g, tm=512)


def kernel(x, mem, rel_bias, norm_mix_g, w_in, cmp_pos_k, cmp_w1_k, cmp_w2_k, cmp_pos_v, cmp_w1_v, cmp_w2_v, w_branch_a, w_branch_b, w_out, norm_xattn_g, norm_mem_g, w_xq, w_xkv, w_xo, norm_ffn_g, w_gate, w_up, conv_w, conv_b, w_down, norm_final_g):
    depth = w_in.shape[0]
    assert depth == 1, "the final norm is fused into the last layer's FFN kernel"
    bias_tiles = _bias_tiles(rel_bias)
    l = 0
    return _layer(x, mem, bias_tiles, norm_mix_g[l], w_in[l], cmp_pos_k[l], cmp_w1_k[l],
                  cmp_w2_k[l], cmp_pos_v[l], cmp_w1_v[l], cmp_w2_v[l], w_branch_a[l], w_branch_b[l], w_out[l],
                  norm_xattn_g[l], norm_mem_g[l], w_xq[l], w_xkv[l], w_xo[l],
                  norm_ffn_g[l], w_gate[l], w_up[l], conv_w[l], conv_b[l], w_down[l], norm_final_g)
```
